```python
import jax, jax.numpy as jnp
from jax import lax
import numpy as np

D_MODEL = 1024
BATCH = 8
SEQ = 2048
DEPTH = 1
DEC_BATCH = 128
DEC_SEQ = 8
PAST_LEN = 16384
PAGE_SIZE = 128

MIX_WIDTH = D_MODEL
SGU_WIDTH = MIX_WIDTH // 2
SGU_GROUPS = 8
SGU_GROUP_DIM = SGU_WIDTH // SGU_GROUPS
SGU_CHUNK = 128
RET_WIDTH = MIX_WIDTH - SGU_WIDTH
RET_HEADS = 4
RET_DK = RET_WIDTH // RET_HEADS
RET_DV = RET_WIDTH // RET_HEADS
RET_CHUNK = 128
ROPE_THETA = 10000.0
PLE_DIM = 256
RMS_EPS = 1e-6
LN_EPS = 1e-5
IN_WIDTH = 3 * SGU_WIDTH + RET_HEADS * (2 * RET_DK + 2 * RET_DV)

kernel_name = "hybrid_sgu_retention_decode_step"


def rms_norm(x, g):
    xf = x.astype(jnp.float32)
    y = xf * lax.rsqrt(jnp.mean(xf * xf, axis=-1, keepdims=True) + RMS_EPS)
    return (y * g.astype(jnp.float32)).astype(x.dtype)


def layer_norm(x, g):
    xf = x.astype(jnp.float32)
    mu = jnp.mean(xf, axis=-1, keepdims=True)
    var = jnp.mean(jnp.square(xf - mu), axis=-1, keepdims=True)
    return ((xf - mu) * lax.rsqrt(var + LN_EPS) * g.astype(jnp.float32)).astype(x.dtype)


def rotary(x, pos):
    half = x.shape[-1] // 2
    inv = ROPE_THETA ** (-jnp.arange(half, dtype=jnp.float32) / half)
    ang = pos[:, None] * inv[None, :]
    cos = jnp.cos(ang)[None, :, None, :]
    sin = jnp.sin(ang)[None, :, None, :]
    x1, x2 = x[..., :half], x[..., half:]
    return jnp.concatenate([x1 * cos - x2 * sin, x2 * cos + x1 * sin], axis=-1)


def retention(q, k, v, s0):
    b, L, h, dk = q.shape
    dv = v.shape[-1]
    c = min(RET_CHUNK, L)
    n = L // c
    log_g = jnp.log(1.0 - 2.0 ** (-5.0 - jnp.arange(RET_HEADS, dtype=jnp.float32)))
    idx = jnp.arange(c, dtype=jnp.float32)
    rel = idx[:, None] - idx[None, :]
    decay = jnp.where(rel >= 0, jnp.exp(log_g[:, None, None] * jnp.maximum(rel, 0.0)), 0.0)
    qc = q.reshape(b, n, c, h, dk)
    kc = k.reshape(b, n, c, h, dk)
    vc = v.reshape(b, n, c, h, dv)
    scores = jnp.einsum('bnihd,bnjhd->bnhij', qc, kc) * decay[None, None]
    o_intra = jnp.einsum('bnhij,bnjhe->bnihe', scores, vc)
    w_kv = jnp.exp(log_g[:, None] * (c - 1.0 - idx)[None, :])
    u = jnp.einsum('bnjhd,bnjhe,hj->bnhde', kc, vc, w_kv)
    g_chunk = jnp.exp(log_g * c)[:, None, None]

    def step(s, u_n):
        return g_chunk * s + u_n, s

    s_final, s_prev = lax.scan(step, s0, jnp.moveaxis(u, 1, 0))
    s_prev = jnp.moveaxis(s_prev, 0, 1)
    w_q = jnp.exp(log_g[:, None] * (idx + 1.0)[None, :])
    o_inter = jnp.einsum('bnihd,bnhde,hi->bnihe', qc, s_prev, w_q)
    return (o_intra + o_inter).reshape(b, L, h, dv), s_final


def spatial_gating(u, v, w_s, b_s, ln_g):
    b, L, _ = u.shape
    c = min(SGU_CHUNK, L)
    n = L // c
    v = layer_norm(v, ln_g)
    w = jnp.tril(w_s[:, :c, :c])
    vc = v.reshape(b, n, c, SGU_GROUPS, SGU_GROUP_DIM)
    mixed = jnp.einsum('gij,bnjgc->bnigc', w, vc) + jnp.transpose(b_s[:, :c])[None, None, :, :, None]
    return u * mixed.reshape(b, L, SGU_WIDTH), v


def hybrid_layer(x, p, s0, pos, w_in, w_out, norm_pre, norm_post, sgu_w, sgu_b, sgu_ln,
                 ret_gn, w_ple_proj, w_ple_gate):
    b, L, _ = x.shape
    h = rms_norm(x, norm_pre)
    z = h @ w_in
    o1 = SGU_WIDTH
    o2 = o1 + SGU_WIDTH
    o3 = o2 + SGU_WIDTH
    o4 = o3 + RET_HEADS * RET_DK
    o5 = o4 + RET_HEADS * RET_DK
    o6 = o5 + RET_HEADS * RET_DV
    su = jax.nn.gelu(z[..., :o1])
    sv = jax.nn.gelu(z[..., o1:o2])
    sg = z[..., o2:o3]
    rq = z[..., o3:o4].reshape(b, L, RET_HEADS, RET_DK).astype(jnp.float32)
    rk = z[..., o4:o5].reshape(b, L, RET_HEADS, RET_DK).astype(jnp.float32)
    rv = z[..., o5:o6].reshape(b, L, RET_HEADS, RET_DV).astype(jnp.float32)
    rg = z[..., o6:]
    a_out, v_rows = spatial_gating(su, sv, sgu_w, sgu_b, sgu_ln)
    a_out = jax.nn.silu(sg) * a_out
    q = rotary(rq, pos)
    k = rotary(rk, pos) * (RET_DK ** -0.5)
    o, s_final = retention(q, k, rv, s0.astype(jnp.float32))
    mu = jnp.mean(o, axis=-1, keepdims=True)
    var = jnp.mean(jnp.square(o - mu), axis=-1, keepdims=True)
    o = (o - mu) * lax.rsqrt(var + LN_EPS) * ret_gn.astype(jnp.float32).reshape(RET_HEADS, RET_DV)
    b_out = jax.nn.silu(rg) * o.reshape(b, L, RET_HEADS * RET_DV).astype(x.dtype)
    mix = jnp.concatenate([a_out, b_out], axis=-1) @ w_out
    x = x + rms_norm(mix, norm_post)
    x = x + jax.nn.sigmoid(x @ w_ple_gate) * (p @ w_ple_proj)
    return x, s_final.astype(s0.dtype), v_rows


def setup_inputs(seed: int = 0) -> dict:
    key = jax.random.key(seed)
    ks = jax.random.split(key, 16)
    f32 = jnp.float32
    nrm = lambda k, shape, s: jax.random.normal(k, shape, f32) * s
    return {
        "x_prompt": nrm(ks[0], (BATCH, SEQ, D_MODEL), 1.0),
        "x_sample": nrm(ks[1], (DEC_BATCH, DEC_SEQ, D_MODEL), 1.0),
        "state_ret": nrm(ks[2], (DEPTH, DEC_BATCH, RET_HEADS, RET_DK, RET_DV), 0.5),
        "p_prompt": nrm(ks[3], (DEPTH, BATCH, SEQ, PLE_DIM), 1.0),
        "p_sample": nrm(ks[4], (DEPTH, DEC_BATCH, DEC_SEQ, PLE_DIM), 1.0),
        "w_in": nrm(ks[5], (DEPTH, D_MODEL, IN_WIDTH), D_MODEL ** -0.5),
        "w_out": nrm(ks[6], (DEPTH, MIX_WIDTH, D_MODEL), MIX_WIDTH ** -0.5),
        "norm_pre": 1.0 + nrm(ks[7], (DEPTH, D_MODEL), 0.05),
        "norm_post": 1.0 + nrm(ks[8], (DEPTH, D_MODEL), 0.05),
        "sgu_w": nrm(ks[9], (DEPTH, SGU_GROUPS, SGU_CHUNK, SGU_CHUNK), SGU_CHUNK ** -0.5),
        "sgu_b": 1.0 + nrm(ks[10], (DEPTH, SGU_GROUPS, SGU_CHUNK), 0.1),
        "sgu_ln": 1.0 + nrm(ks[11], (DEPTH, SGU_WIDTH), 0.05),
        "ret_gn": 1.0 + nrm(ks[12], (DEPTH, RET_HEADS * RET_DV), 0.05),
        "w_ple_proj": nrm(ks[13], (DEPTH, PLE_DIM, D_MODEL), PLE_DIM ** -0.5),
        "w_ple_gate": nrm(ks[14], (DEPTH, D_MODEL, D_MODEL), D_MODEL ** -0.5),
    }


def reference(x_prompt, x_sample, state_ret, p_prompt, p_sample, w_in, w_out, norm_pre,
              norm_post, sgu_w, sgu_b, sgu_ln, ret_gn, w_ple_proj, w_ple_gate):
    seq = x_prompt.shape[1]
    dec_seq = x_sample.shape[1]
    pos_prompt = jnp.arange(seq, dtype=jnp.float32)
    pos_sample = jnp.arange(dec_seq, dtype=jnp.float32) + PAST_LEN
    s0_prompt = jnp.zeros((x_prompt.shape[0], RET_HEADS, RET_DK, RET_DV), state_ret.dtype)
    xp, xs = x_prompt, x_sample
    st_p, st_s, v_s = [], [], []
    for l in range(DEPTH):
        wl = (w_in[l], w_out[l], norm_pre[l], norm_post[l], sgu_w[l], sgu_b[l], sgu_ln[l],
              ret_gn[l], w_ple_proj[l], w_ple_gate[l])
        xp, sp, _ = hybrid_layer(xp, p_prompt[l], s0_prompt, pos_prompt, *wl)
        xs, ss, vs = hybrid_layer(xs, p_sample[l], state_ret[l], pos_sample, *wl)
        st_p.append(sp)
        st_s.append(ss)
        v_s.append(vs)
    new_state_ret_prompt = jnp.stack(st_p, axis=0)
    new_state_ret_sample = jnp.stack(st_s, axis=0)
    new_sgu_v_sample = jnp.stack(v_s, axis=0)
    return (xp, xs, new_state_ret_prompt, new_state_ret_sample, new_sgu_v_sample)
```

```python
import functools

import numpy as np
import jax
import jax.numpy as jnp
from jax import lax
from jax.experimental import pallas as pl
from jax.experimental.pallas import tpu as pltpu

F32 = jnp.float32
BF16 = jnp.bfloat16

D_MODEL = 1024
PAST_LEN = 16384
SGU_WIDTH = 512
SGU_GROUPS = 8
SGU_GROUP_DIM = SGU_WIDTH // SGU_GROUPS
RET_HEADS = 4
RET_DK = 128
RET_DV = 128
CHUNK = 128
ROPE_THETA = 10000.0
PLE_DIM = 256
RMS_EPS = 1e-6
LN_EPS = 1e-5
IN_WIDTH = 3 * SGU_WIDTH + RET_HEADS * (2 * RET_DK + 2 * RET_DV)
O_SU, O_SV, O_SG = 0, SGU_WIDTH, 2 * SGU_WIDTH
O_Q = 3 * SGU_WIDTH
O_K = O_Q + RET_HEADS * RET_DK
O_V = O_K + RET_HEADS * RET_DK
O_RG = O_V + RET_HEADS * RET_DV

LANES = 128
VMEM_LIMIT_BYTES = 56 * 1024 * 1024

PROMPT_TILE = 512
SAMPLE_SEQS = CHUNK // 8


def _log_gamma():
    return np.log(1.0 - 2.0 ** (-5.0 - np.arange(RET_HEADS, dtype=np.float64)))


def _retention_tables(seg):
    lg = _log_gamma()[:, None, None]
    r = np.arange(CHUNK)
    i, j = r[:, None] % seg, r[None, :] % seg
    same = (r[:, None] // seg) == (r[None, :] // seg)
    decay = np.where(same & (i >= j), np.exp(lg * np.maximum(i - j, 0)), 0.0)
    wq = np.broadcast_to(np.exp(lg * (i + 1.0)), (RET_HEADS, CHUNK, LANES))
    wkv = np.broadcast_to(np.exp(lg * (seg - 1.0 - i)), (RET_HEADS, CHUNK, LANES))
    tabs = np.stack([decay, wq, wkv]).astype(np.float32)
    g_seg = [float(v) for v in np.exp(_log_gamma() * seg)]
    return tabs, g_seg


def _rotary_tables(pos):
    half = RET_DK // 2
    inv = ROPE_THETA ** (-np.arange(half, dtype=np.float64) / half)
    ang = pos.astype(np.float64)[:, None] * inv[None, :]
    cos, sin = np.cos(ang), np.sin(ang)
    return (np.concatenate([cos, cos], axis=1).astype(np.float32),
            np.concatenate([-sin, sin], axis=1).astype(np.float32))


def _mix_mask(seg):
    r = np.arange(CHUNK)
    same = (r[:, None] // seg) == (r[None, :] // seg)
    m = same & ((r[None, :] % seg) <= (r[:, None] % seg))
    return np.concatenate([m, m], axis=1).astype(np.float32)


def _gelu(x):
    return 0.5 * x * (1.0 + jnp.tanh(np.sqrt(2.0 / np.pi) * (x + 0.044715 * (x * x * x))))


def _silu(x):
    return x * jax.nn.sigmoid(x)


def _rotate(x, cos, sin_signed):
    return x * cos + pltpu.roll(x, RET_DK // 2, 1) * sin_signed


def _project_in(x, norm_pre_ref, w_in_ref, z_ref):
    ms = jnp.mean(x * x, axis=-1, keepdims=True)
    h = x * lax.rsqrt(ms + RMS_EPS) * norm_pre_ref[...]
    z_ref[...] = jnp.dot(h.astype(BF16), w_in_ref[...], preferred_element_type=F32)


def _project_out(x, p, mixin_ref, w_out_ref, norm_post_ref, w_gate_ref, w_ple_ref):
    mix = jnp.dot(mixin_ref[...], w_out_ref[...], preferred_element_type=F32)
    ms = jnp.mean(mix * mix, axis=-1, keepdims=True)
    x1 = x + mix * lax.rsqrt(ms + RMS_EPS) * norm_post_ref[...]
    gate = jax.nn.sigmoid(jnp.dot(x1.astype(BF16), w_gate_ref[...], preferred_element_type=F32))
    ple = jnp.dot(p.astype(BF16), w_ple_ref[...], preferred_element_type=F32)
    return x1 + gate * ple


def _spatial_gating(z_ref, rows, ln_g_ref, bias_ref, wpair_ref, mixin_ref, vrows_ref):
    su = _gelu(z_ref[rows, O_SU:O_SU + SGU_WIDTH])
    sv = _gelu(z_ref[rows, O_SV:O_SV + SGU_WIDTH])
    mu = jnp.mean(sv, axis=-1, keepdims=True)
    cen = sv - mu
    var = jnp.mean(cen * cen, axis=-1, keepdims=True)
    vn = cen * lax.rsqrt(var + LN_EPS) * ln_g_ref[...]
    if vrows_ref is not None:
        vrows_ref[rows, :] = vn
    low_lanes = lax.broadcasted_iota(jnp.int32, (CHUNK, LANES), 1) < SGU_GROUP_DIM
    for m in range(SGU_GROUPS // 2):
        cols = slice(m * LANES, (m + 1) * LANES)
        blk = vn[:, cols]
        rhs = jnp.concatenate([jnp.where(low_lanes, blk, 0.0), jnp.where(low_lanes, 0.0, blk)],
                              axis=0).astype(BF16)
        mixed = jnp.dot(wpair_ref[m], rhs, preferred_element_type=F32) + bias_ref[:, cols]
        gate = _silu(z_ref[rows, O_SG + m * LANES:O_SG + (m + 1) * LANES])
        mixin_ref[rows, cols] = (gate * (su[:, cols] * mixed)).astype(BF16)


def _retention_head(z_ref, rows, h, cos, sin_signed, tabs_ref, state_fn, gn_ref, mixin_ref):
    hs = slice(h * RET_DK, (h + 1) * RET_DK)
    q = _rotate(z_ref[rows, O_Q + h * RET_DK:O_Q + (h + 1) * RET_DK], cos, sin_signed)
    k = _rotate(z_ref[rows, O_K + h * RET_DK:O_K + (h + 1) * RET_DK], cos, sin_signed) * (RET_DK ** -0.5)
    v = z_ref[rows, O_V + h * RET_DV:O_V + (h + 1) * RET_DV]
    scores = lax.dot_general(q.astype(BF16), k.astype(BF16), (((1,), (1,)), ((), ())),
                             preferred_element_type=F32)
    scores_b = (scores * tabs_ref[0, h]).astype(BF16)
    o = state_fn(h, scores_b, q * tabs_ref[1, h], k, v, v * tabs_ref[2, h])
    mu = jnp.mean(o, axis=-1, keepdims=True)
    cen = o - mu
    var = jnp.mean(cen * cen, axis=-1, keepdims=True)
    on = cen * lax.rsqrt(var + LN_EPS) * gn_ref[:, hs]
    gate = _silu(z_ref[rows, O_RG + h * RET_DV:O_RG + (h + 1) * RET_DV])
    mixin_ref[rows, SGU_WIDTH + h * RET_DV:SGU_WIDTH + (h + 1) * RET_DV] = (gate * on).astype(BF16)


def _init_masked_pairs(wpair_ref, mask_ref, wpm_ref):
    for m in range(SGU_GROUPS // 2):
        wpm_ref[m] = jnp.where(mask_ref[...] > 0.0, wpair_ref[m], jnp.zeros((), BF16))


def _prompt_kernel(g_chunk, x_ref, p_ref, cos_ref, sin_ref, w_in_ref, w_out_ref, w_gate_ref, w_ple_ref,
                   norm_pre_ref, norm_post_ref, ln_g_ref, gn_ref, bias_ref, wpair_ref, mask_ref, tabs_ref,
                   y_ref, state_ref, z_ref, mixin_ref, wpm_ref):
    b, t = pl.program_id(0), pl.program_id(1)

    @pl.when((b == 0) & (t == 0))
    def _():
        _init_masked_pairs(wpair_ref, mask_ref, wpm_ref)

    @pl.when(t == 0)
    def _():
        state_ref[...] = jnp.zeros_like(state_ref)

    x = x_ref[0]
    _project_in(x, norm_pre_ref, w_in_ref, z_ref)

    def state_fn(h, scores_b, qw, k, v, vw):
        s = state_ref[0, h]
        lhs = jnp.concatenate([scores_b, qw.astype(BF16)], axis=1)
        rhs = jnp.concatenate([v.astype(BF16), s.astype(BF16)], axis=0)
        o = jnp.dot(lhs, rhs, preferred_element_type=F32)
        u = lax.dot_general(k.astype(BF16), vw.astype(BF16), (((0,), (0,)), ((), ())),
                            preferred_element_type=F32)
        state_ref[0, h] = g_chunk[h] * s + u
        return o

    for c in range(PROMPT_TILE // CHUNK):
        rows = slice(c * CHUNK, (c + 1) * CHUNK)
        _spatial_gating(z_ref, rows, ln_g_ref, bias_ref, wpm_ref, mixin_ref, None)
        cos, sin_signed = cos_ref[rows, :], sin_ref[rows, :]
        for h in range(RET_HEADS):
            _retention_head(z_ref, rows, h, cos, sin_signed, tabs_ref, state_fn, gn_ref, mixin_ref)

    y_ref[0] = _project_out(x, p_ref[0], mixin_ref, w_out_ref, norm_post_ref, w_gate_ref, w_ple_ref)


def _sample_kernel(g_seq, x_ref, p_ref, cos_ref, sin_ref, st_in_ref, w_in_ref, w_out_ref, w_gate_ref,
                   w_ple_ref, norm_pre_ref, norm_post_ref, ln_g_ref, gn_ref, bias_ref, wpair_ref, mask_ref,
                   tabs_ref, y_ref, st_out_ref, vrows_ref, z_ref, mixin_ref, wpm_ref):
    @pl.when(pl.program_id(0) == 0)
    def _():
        _init_masked_pairs(wpair_ref, mask_ref, wpm_ref)

    x = x_ref[...]
    _project_in(x, norm_pre_ref, w_in_ref, z_ref)
    seq_len = CHUNK // SAMPLE_SEQS

    def state_fn(h, scores_b, qw, k, v, vw):
        s0 = st_in_ref[:, h]
        o = jnp.dot(scores_b, v.astype(BF16), preferred_element_type=F32)
        per_seq = lambda a: a.reshape(SAMPLE_SEQS, seq_len, a.shape[-1]).astype(BF16)
        o_inter = jnp.einsum('sid,sde->sie', per_seq(qw), s0.astype(BF16), preferred_element_type=F32)
        u = jnp.einsum('sjd,sje->sde', per_seq(k), per_seq(vw), preferred_element_type=F32)
        st_out_ref[:, h] = g_seq[h] * s0 + u
        return o + o_inter.reshape(CHUNK, RET_DV)

    rows = slice(0, CHUNK)
    _spatial_gating(z_ref, rows, ln_g_ref, bias_ref, wpm_ref, mixin_ref, vrows_ref)
    cos, sin_signed = cos_ref[...], sin_ref[...]
    for h in range(RET_HEADS):
        _retention_head(z_ref, rows, h, cos, sin_signed, tabs_ref, state_fn, gn_ref, mixin_ref)

    y_ref[...] = _project_out(x, p_ref[...], mixin_ref, w_out_ref, norm_post_ref, w_gate_ref, w_ple_ref)


def _const_spec(shape):
    return pl.BlockSpec(shape, lambda *_: (0,) * len(shape))


def _pair_groups(w):
    return jnp.concatenate([w[0::2], w[1::2]], axis=2).astype(BF16)


def kernel(x_prompt, x_sample, state_ret, p_prompt, p_sample, w_in, w_out, norm_pre, norm_post, sgu_w, sgu_b,
           sgu_ln, ret_gn, w_ple_proj, w_ple_gate):
    batch, seq, _ = x_prompt.shape
    dec_batch, dec_seq, _ = x_sample.shape
    assert seq % PROMPT_TILE == 0 and CHUNK % dec_seq == 0 and CHUNK // dec_seq == SAMPLE_SEQS
    assert dec_batch % SAMPLE_SEQS == 0 and w_in.shape[0] == 1

    w_in_b, w_out_b = w_in[0].astype(BF16), w_out[0].astype(BF16)
    w_gate_b, w_ple_b = w_ple_gate[0].astype(BF16), w_ple_proj[0].astype(BF16)
    norm_pre2, norm_post2 = norm_pre[0][None, :], norm_post[0][None, :]
    ln_g2, gn2 = sgu_ln[0][None, :], ret_gn[0][None, :]

    weight_specs = [
        _const_spec((D_MODEL, IN_WIDTH)), _const_spec((D_MODEL, D_MODEL)), _const_spec((D_MODEL, D_MODEL)),
        _const_spec((PLE_DIM, D_MODEL)), _const_spec((1, D_MODEL)), _const_spec((1, D_MODEL)),
        _const_spec((1, SGU_WIDTH)), _const_spec((1, RET_HEADS * RET_DV)), _const_spec((CHUNK, SGU_WIDTH)),
        _const_spec((SGU_GROUPS // 2, CHUNK, 2 * CHUNK)), _const_spec((CHUNK, 2 * CHUNK)),
        _const_spec((3, RET_HEADS, CHUNK, LANES)),
    ]
    weights = (w_in_b, w_out_b, w_gate_b, w_ple_b, norm_pre2, norm_post2, ln_g2, gn2)

    tabs_p, g_chunk = _retention_tables(CHUNK)
    cos_p, sin_p = _rotary_tables(np.arange(seq))
    bias_p = jnp.repeat(sgu_b[0].T, SGU_GROUP_DIM, axis=1)
    wpair_p = _pair_groups(sgu_w[0])
    n_tiles = seq // PROMPT_TILE
    tile_spec = lambda width: pl.BlockSpec((1, PROMPT_TILE, width), lambda b, t: (b, t, 0))
    pos_spec = pl.BlockSpec((PROMPT_TILE, LANES), lambda b, t: (t, 0))
    y_prompt, st_prompt = pl.pallas_call(
        functools.partial(_prompt_kernel, g_chunk),
        grid=(batch, n_tiles),
        in_specs=[tile_spec(D_MODEL), tile_spec(PLE_DIM), pos_spec, pos_spec] + weight_specs,
        out_specs=[tile_spec(D_MODEL),
                   pl.BlockSpec((1, RET_HEADS, RET_DK, RET_DV), lambda b, t: (b, 0, 0, 0))],
        out_shape=[jax.ShapeDtypeStruct((batch, seq, D_MODEL), F32),
                   jax.ShapeDtypeStruct((batch, RET_HEADS, RET_DK, RET_DV), F32)],
        scratch_shapes=[pltpu.VMEM((PROMPT_TILE, IN_WIDTH), F32), pltpu.VMEM((PROMPT_TILE, D_MODEL), BF16),
                        pltpu.VMEM((SGU_GROUPS // 2, CHUNK, 2 * CHUNK), BF16)],
        compiler_params=pltpu.CompilerParams(dimension_semantics=("arbitrary", "arbitrary"),
                                             vmem_limit_bytes=VMEM_LIMIT_BYTES),
        name="prompt_layer",
    )(x_prompt, p_prompt[0], cos_p, sin_p, *weights, bias_p, wpair_p, _mix_mask(CHUNK), tabs_p)

    tabs_s, g_seq = _retention_tables(dec_seq)
    cos_s, sin_s = _rotary_tables(PAST_LEN + np.arange(CHUNK) % dec_seq)
    bias_s = jnp.tile(jnp.repeat(sgu_b[0, :, :dec_seq].T, SGU_GROUP_DIM, axis=1), (SAMPLE_SEQS, 1))
    wpair_s = _pair_groups(jnp.tile(sgu_w[0, :, :dec_seq, :dec_seq], (1, SAMPLE_SEQS, SAMPLE_SEQS)))
    n_tok = dec_batch * dec_seq
    tok_spec = lambda width: pl.BlockSpec((CHUNK, width), lambda i: (i, 0))
    state_spec = pl.BlockSpec((SAMPLE_SEQS, RET_HEADS, RET_DK, RET_DV), lambda i: (i, 0, 0, 0))
    y_sample, st_sample, v_sample = pl.pallas_call(
        functools.partial(_sample_kernel, g_seq),
        grid=(n_tok // CHUNK,),
        in_specs=[tok_spec(D_MODEL), tok_spec(PLE_DIM), _const_spec((CHUNK, LANES)), _const_spec((CHUNK, LANES)),
                  state_spec] + weight_specs,
        out_specs=[tok_spec(D_MODEL), state_spec, tok_spec(SGU_WIDTH)],
        out_shape=[jax.ShapeDtypeStruct((n_tok, D_MODEL), F32),
                   jax.ShapeDtypeStruct((dec_batch, RET_HEADS, RET_DK, RET_DV), F32),
                   jax.ShapeDtypeStruct((n_tok, SGU_WIDTH), F32)],
        scratch_shapes=[pltpu.VMEM((CHUNK, IN_WIDTH), F32), pltpu.VMEM((CHUNK, D_MODEL), BF16),
                        pltpu.VMEM((SGU_GROUPS // 2, CHUNK, 2 * CHUNK), BF16)],
        compiler_params=pltpu.CompilerParams(dimension_semantics=("arbitrary",),
                                             vmem_limit_bytes=VMEM_LIMIT_BYTES),
        name="sample_layer",
    )(x_sample.reshape(n_tok, D_MODEL), p_sample[0].reshape(n_tok, PLE_DIM), cos_s, sin_s, state_ret[0],
      *weights, bias_s, wpair_s, _mix_mask(dec_seq), tabs_s)

    return (y_prompt, y_sample.reshape(dec_batch, dec_seq, D_MODEL), st_prompt[None], st_sample[None],
            v_sample.reshape(1, dec_batch, dec_seq, SGU_WIDTH))
```

```python
import functools

import numpy as np
import jax
import jax.numpy as jnp
from jax import lax
from jax.experimental import pallas as pl
from jax.experimental.pallas import tpu as pltpu

F32 = jnp.float32
BF16 = jnp.bfloat16

D_MODEL = 1024
PAST_LEN = 16384
SGU_WIDTH = 512
SGU_GROUPS = 8
SGU_GROUP_DIM = SGU_WIDTH // SGU_GROUPS
RET_HEADS = 4
RET_DK = 128
RET_DV = 128
CHUNK = 128
ROPE_THETA = 10000.0
PLE_DIM = 256
RMS_EPS = 1e-6
LN_EPS = 1e-5
IN_WIDTH = 3 * SGU_WIDTH + RET_HEADS * (2 * RET_DK + 2 * RET_DV)
O_SU, O_SV, O_SG = 0, SGU_WIDTH, 2 * SGU_WIDTH
O_Q = 3 * SGU_WIDTH
O_K = O_Q + RET_HEADS * RET_DK
O_V = O_K + RET_HEADS * RET_DK
O_RG = O_V + RET_HEADS * RET_DV

LANES = 128
VMEM_LIMIT_BYTES = 56 * 1024 * 1024

PROMPT_TILE = 512
PROMPT_BLOCK = 256
PIECE_COLS = 512
SAMPLE_SEQS = CHUNK // 8


def _log_gamma():
    return np.log(1.0 - 2.0 ** (-5.0 - np.arange(RET_HEADS, dtype=np.float64)))


def _retention_tables(seg):
    lg = _log_gamma()[:, None, None]
    r = np.arange(CHUNK)
    i, j = r[:, None] % seg, r[None, :] % seg
    same = (r[:, None] // seg) == (r[None, :] // seg)
    decay = np.where(same & (i >= j), np.exp(lg * np.maximum(i - j, 0)), 0.0)
    wq = np.broadcast_to(np.exp(lg * (i + 1.0)), (RET_HEADS, CHUNK, LANES))
    wkv = np.broadcast_to(np.exp(lg * (seg - 1.0 - i)), (RET_HEADS, CHUNK, LANES))
    tabs = np.stack([decay, wq, wkv]).astype(np.float32)
    g_seg = [float(v) for v in np.exp(_log_gamma() * seg)]
    return tabs, g_seg


def _rotary_tables(pos):
    half = RET_DK // 2
    inv = ROPE_THETA ** (-np.arange(half, dtype=np.float64) / half)
    ang = pos.astype(np.float64)[:, None] * inv[None, :]
    cos, sin = np.cos(ang), np.sin(ang)
    return (np.concatenate([cos, cos], axis=1).astype(np.float32),
            np.concatenate([-sin, sin], axis=1).astype(np.float32))


def _mix_mask(seg):
    r = np.arange(CHUNK)
    same = (r[:, None] // seg) == (r[None, :] // seg)
    m = same & ((r[None, :] % seg) <= (r[:, None] % seg))
    return np.concatenate([m, m], axis=1).astype(np.float32)


def _gelu(x):
    return 0.5 * x * (1.0 + jnp.tanh(np.sqrt(2.0 / np.pi) * (x + 0.044715 * (x * x * x))))


def _silu(x):
    return x * jax.nn.sigmoid(x)


def _rotate(x, cos, sin_signed):
    return x * cos + pltpu.roll(x, RET_DK // 2, 1) * sin_signed


def _normed_bf16(x, norm_ref):
    ms = jnp.mean(x * x, axis=-1, keepdims=True)
    return (x * lax.rsqrt(ms + RMS_EPS) * norm_ref[...]).astype(BF16)


def _in_pieces(x, norm_pre_ref, w_in_ref, z_ref):
    box = {}

    def piece(c0):
        def run():
            if 'h' not in box:
                box['h'] = _normed_bf16(x, norm_pre_ref)
            z_ref[:, c0:c0 + PIECE_COLS] = jnp.dot(box['h'], w_in_ref[:, c0:c0 + PIECE_COLS],
                                                   preferred_element_type=F32)
        return run
    return [piece(c0) for c0 in range(0, IN_WIDTH, PIECE_COLS)]


def _out_pieces(x, p, mixin_ref, w_out_ref, norm_post_ref, w_gate_ref, w_ple_ref, y_ref):
    box = {}
    halves = [slice(c0, c0 + PIECE_COLS) for c0 in range(0, D_MODEL, PIECE_COLS)]

    def ple():
        box['ple'] = jnp.dot(p.astype(BF16), w_ple_ref[...], preferred_element_type=F32)

    def mix(i):
        def run():
            box['mix%d' % i] = jnp.dot(mixin_ref[...], w_out_ref[:, halves[i]], preferred_element_type=F32)
        return run

    def gate(i):
        def run():
            if 'x1' not in box:
                mixes = [box['mix%d' % j] for j in range(len(halves))]
                ms = sum(jnp.sum(m * m, axis=-1, keepdims=True) for m in mixes) * (1.0 / D_MODEL)
                scale = lax.rsqrt(ms + RMS_EPS)
                box['x1'] = [x[:, halves[j]] + mixes[j] * scale * norm_post_ref[:, halves[j]]
                             for j in range(len(halves))]
                box['x1b'] = jnp.concatenate(box['x1'], axis=1).astype(BF16)
            g = jnp.dot(box['x1b'], w_gate_ref[:, halves[i]], preferred_element_type=F32)
            y_ref[:, halves[i]] = box['x1'][i] + jax.nn.sigmoid(g) * box['ple'][:, halves[i]]
        return run
    return [ple] + [mix(i) for i in range(len(halves))] + [gate(i) for i in range(len(halves))]


def _chunk_stages(z_ref, rows, cos_ref, sin_ref, ln_g_ref, gn_ref, bias_ref, wpair_ref, tabs_ref,
                  state_update, state_output, mixin_ref, vrows_ref):
    box = {}
    zs = lambda off, i, width: z_ref[rows, off + i * width:off + (i + 1) * width]

    def gating_in():
        sv = _gelu(z_ref[rows, O_SV:O_SV + SGU_WIDTH])
        mu = jnp.mean(sv, axis=-1, keepdims=True)
        cen = sv - mu
        var = jnp.mean(cen * cen, axis=-1, keepdims=True)
        vn = cen * lax.rsqrt(var + LN_EPS) * ln_g_ref[...]
        if vrows_ref is not None:
            vrows_ref[rows, :] = vn
        low_lanes = lax.broadcasted_iota(jnp.int32, (CHUNK, LANES), 1) < SGU_GROUP_DIM
        for m in range(SGU_GROUPS // 2):
            blk = vn[:, m * LANES:(m + 1) * LANES]
            rhs = jnp.concatenate([jnp.where(low_lanes, blk, 0.0), jnp.where(low_lanes, 0.0, blk)],
                                  axis=0).astype(BF16)
            box['mixed%d' % m] = jnp.dot(wpair_ref[m], rhs, preferred_element_type=F32)

    def retention_in():
        cos, sin_signed = cos_ref[rows, :], sin_ref[rows, :]
        for h in range(RET_HEADS):
            q = _rotate(zs(O_Q, h, RET_DK), cos, sin_signed)
            k = _rotate(zs(O_K, h, RET_DK), cos, sin_signed) * (RET_DK ** -0.5)
            v = zs(O_V, h, RET_DV)
            k_b = k.astype(BF16)
            box['scores%d' % h] = lax.dot_general(q.astype(BF16), k_b, (((1,), (1,)), ((), ())),
                                                  preferred_element_type=F32)
            box['qw%d' % h] = (q * tabs_ref[1, h]).astype(BF16)
            box['v%d' % h] = v.astype(BF16)
            box['u%d' % h] = state_update(h, k_b, (v * tabs_ref[2, h]).astype(BF16))

    def gating_out_retention_mid():
        for m in range(SGU_GROUPS // 2):
            cols = slice(m * LANES, (m + 1) * LANES)
            mixed = box['mixed%d' % m] + bias_ref[:, cols]
            su = _gelu(zs(O_SU, m, LANES))
            mixin_ref[rows, cols] = (_silu(zs(O_SG, m, LANES)) * (su * mixed)).astype(BF16)
        for h in range(RET_HEADS):
            scores_b = (box['scores%d' % h] * tabs_ref[0, h]).astype(BF16)
            box['o%d' % h] = state_output(h, scores_b, box['qw%d' % h], box['v%d' % h], box['u%d' % h])

    def retention_out():
        for h in range(RET_HEADS):
            o = box['o%d' % h]
            mu = jnp.mean(o, axis=-1, keepdims=True)
            cen = o - mu
            var = jnp.mean(cen * cen, axis=-1, keepdims=True)
            on = cen * lax.rsqrt(var + LN_EPS) * gn_ref[:, h * RET_DV:(h + 1) * RET_DV]
            mixin_ref[rows, SGU_WIDTH + h * RET_DV:SGU_WIDTH + (h + 1) * RET_DV] = (
                _silu(zs(O_RG, h, RET_DV)) * on).astype(BF16)

    return [gating_in, retention_in, gating_out_retention_mid, retention_out]


def _interleave(pieces, stages):
    pieces, stages = list(pieces), list(stages)
    while pieces or stages:
        if pieces:
            pieces.pop(0)()
        if stages:
            stages.pop(0)()


def _init_masked_pairs(wpair_ref, mask_ref, wpm_ref):
    for m in range(SGU_GROUPS // 2):
        wpm_ref[m] = jnp.where(mask_ref[...] > 0.0, wpair_ref[m], jnp.zeros((), BF16))


def _prompt_kernel(g_chunk, x_ref, p_ref, cos_ref, sin_ref, w_in_ref, w_out_ref, w_gate_ref, w_ple_ref,
                   norm_pre_ref, norm_post_ref, ln_g_ref, gn_ref, bias_ref, wpair_ref, mask_ref, tabs_ref,
                   y_ref, state_ref, z_ref, mixin_ref, wpm_ref):
    b, t = pl.program_id(0), pl.program_id(1)

    @pl.when((b == 0) & (t == 0))
    def _():
        _init_masked_pairs(wpair_ref, mask_ref, wpm_ref)

    @pl.when(t == 0)
    def _():
        state_ref[...] = jnp.zeros_like(state_ref)

    def state_update(h, k_b, vw_b):
        return lax.dot_general(k_b, vw_b, (((0,), (0,)), ((), ())), preferred_element_type=F32)

    def state_output(h, scores_b, qw_b, v_b, u):
        s = state_ref[0, h]
        o = jnp.dot(jnp.concatenate([scores_b, qw_b], axis=1),
                    jnp.concatenate([v_b, s.astype(BF16)], axis=0), preferred_element_type=F32)
        state_ref[0, h] = g_chunk[h] * s + u
        return o

    def in_pieces(blk):
        return _in_pieces(x_ref[0, blk, :], norm_pre_ref, w_in_ref, z_ref.at[blk, :])

    def out_pieces(blk):
        return _out_pieces(x_ref[0, blk, :], p_ref[0, blk, :], mixin_ref.at[blk, :], w_out_ref, norm_post_ref,
                           w_gate_ref, w_ple_ref, y_ref.at[0, blk, :])

    def stages(blk):
        out = []
        for r in range(blk.start, blk.stop, CHUNK):
            out += _chunk_stages(z_ref, slice(r, r + CHUNK), cos_ref, sin_ref, ln_g_ref, gn_ref, bias_ref,
                                 wpm_ref, tabs_ref, state_update, state_output, mixin_ref, None)
        return out

    blocks = [slice(r, r + PROMPT_BLOCK) for r in range(0, PROMPT_TILE, PROMPT_BLOCK)]
    _interleave(in_pieces(blocks[0]), [])
    for i, blk in enumerate(blocks):
        pieces = out_pieces(blocks[i - 1]) if i > 0 else []
        pieces += in_pieces(blocks[i + 1]) if i + 1 < len(blocks) else []
        _interleave(pieces, stages(blk))
    _interleave(out_pieces(blocks[-1]), [])


def _sample_kernel(g_seq, x_ref, p_ref, cos_ref, sin_ref, st_in_ref, w_in_ref, w_out_ref, w_gate_ref,
                   w_ple_ref, norm_pre_ref, norm_post_ref, ln_g_ref, gn_ref, bias_ref, wpair_ref, mask_ref,
                   tabs_ref, y_ref, st_out_ref, vrows_ref, z_ref, mixin_ref, wpm_ref):
    @pl.when(pl.program_id(0) == 0)
    def _():
        _init_masked_pairs(wpair_ref, mask_ref, wpm_ref)

    seq_len = CHUNK // SAMPLE_SEQS
    per_seq = lambda a: a.astype(F32).reshape(SAMPLE_SEQS, seq_len, a.shape[-1]).astype(BF16)

    def state_update(h, k_b, vw_b):
        u = jnp.einsum('sjd,sje->sde', per_seq(k_b), per_seq(vw_b), preferred_element_type=F32)
        st_out_ref[:, h] = g_seq[h] * st_in_ref[:, h] + u
        return None

    def state_output(h, scores_b, qw_b, v_b, _):
        o = jnp.dot(scores_b, v_b, preferred_element_type=F32)
        o_inter = jnp.einsum('sid,sde->sie', per_seq(qw_b), st_in_ref[:, h].astype(BF16),
                             preferred_element_type=F32)
        return o + o_inter.reshape(CHUNK, RET_DV)

    _interleave(_in_pieces(x_ref[...], norm_pre_ref, w_in_ref, z_ref), [])
    _interleave([], _chunk_stages(z_ref, slice(0, CHUNK), cos_ref, sin_ref, ln_g_ref, gn_ref, bias_ref, wpm_ref,
                                  tabs_ref, state_update, state_output, mixin_ref, vrows_ref))
    _interleave(_out_pieces(x_ref[...], p_ref[...], mixin_ref, w_out_ref, norm_post_ref, w_gate_ref, w_ple_ref,
                            y_ref), [])


def _const_spec(shape):
    return pl.BlockSpec(shape, lambda *_: (0,) * len(shape))


def _pair_groups(w):
    return jnp.concatenate([w[0::2], w[1::2]], axis=2).astype(BF16)


def kernel(x_prompt, x_sample, state_ret, p_prompt, p_sample, w_in, w_out, norm_pre, norm_post, sgu_w, sgu_b,
           sgu_ln, ret_gn, w_ple_proj, w_ple_gate):
    batch, seq, _ = x_prompt.shape
    dec_batch, dec_seq, _ = x_sample.shape
    assert seq % PROMPT_TILE == 0 and CHUNK % dec_seq == 0 and CHUNK // dec_seq == SAMPLE_SEQS
    assert dec_batch % SAMPLE_SEQS == 0 and w_in.shape[0] == 1

    w_in_b, w_out_b = w_in[0].astype(BF16), w_out[0].astype(BF16)
    w_gate_b, w_ple_b = w_ple_gate[0].astype(BF16), w_ple_proj[0].astype(BF16)
    norm_pre2, norm_post2 = norm_pre[0][None, :], norm_post[0][None, :]
    ln_g2, gn2 = sgu_ln[0][None, :], ret_gn[0][None, :]

    weight_specs = [
        _const_spec((D_MODEL, IN_WIDTH)), _const_spec((D_MODEL, D_MODEL)), _const_spec((D_MODEL, D_MODEL)),
        _const_spec((PLE_DIM, D_MODEL)), _const_spec((1, D_MODEL)), _const_spec((1, D_MODEL)),
        _const_spec((1, SGU_WIDTH)), _const_spec((1, RET_HEADS * RET_DV)), _const_spec((CHUNK, SGU_WIDTH)),
        _const_spec((SGU_GROUPS // 2, CHUNK, 2 * CHUNK)), _const_spec((CHUNK, 2 * CHUNK)),
        _const_spec((3, RET_HEADS, CHUNK, LANES)),
    ]
    weights = (w_in_b, w_out_b, w_gate_b, w_ple_b, norm_pre2, norm_post2, ln_g2, gn2)

    tabs_p, g_chunk = _retention_tables(CHUNK)
    cos_p, sin_p = _rotary_tables(np.arange(seq))
    bias_p = jnp.repeat(sgu_b[0].T, SGU_GROUP_DIM, axis=1)
    wpair_p = _pair_groups(sgu_w[0])
    n_tiles = seq // PROMPT_TILE
    tile_spec = lambda width: pl.BlockSpec((1, PROMPT_TILE, width), lambda b, t: (b, t, 0))
    pos_spec = pl.BlockSpec((PROMPT_TILE, LANES), lambda b, t: (t, 0))
    y_prompt, st_prompt = pl.pallas_call(
        functools.partial(_prompt_kernel, g_chunk),
        grid=(batch, n_tiles),
        in_specs=[tile_spec(D_MODEL), tile_spec(PLE_DIM), pos_spec, pos_spec] + weight_specs,
        out_specs=[tile_spec(D_MODEL),
                   pl.BlockSpec((1, RET_HEADS, RET_DK, RET_DV), lambda b, t: (b, 0, 0, 0))],
        out_shape=[jax.ShapeDtypeStruct((batch, seq, D_MODEL), F32),
                   jax.ShapeDtypeStruct((batch, RET_HEADS, RET_DK, RET_DV), F32)],
        scratch_shapes=[pltpu.VMEM((PROMPT_TILE, IN_WIDTH), F32), pltpu.VMEM((PROMPT_TILE, D_MODEL), BF16),
                        pltpu.VMEM((SGU_GROUPS // 2, CHUNK, 2 * CHUNK), BF16)],
        compiler_params=pltpu.CompilerParams(dimension_semantics=("arbitrary", "arbitrary"),
                                             vmem_limit_bytes=VMEM_LIMIT_BYTES),
        name="prompt_layer",
    )(x_prompt, p_prompt[0], cos_p, sin_p, *weights, bias_p, wpair_p, _mix_mask(CHUNK), tabs_p)

    tabs_s, g_seq = _retention_tables(dec_seq)
    cos_s, sin_s = _rotary_tables(PAST_LEN + np.arange(CHUNK) % dec_seq)
    bias_s = jnp.tile(jnp.repeat(sgu_b[0, :, :dec_seq].T, SGU_GROUP_DIM, axis=1), (SAMPLE_SEQS, 1))
    wpair_s = _pair_groups(jnp.tile(sgu_w[0, :, :dec_seq, :dec_seq], (1, SAMPLE_SEQS, SAMPLE_SEQS)))
    n_tok = dec_batch * dec_seq
    tok_spec = lambda width: pl.BlockSpec((CHUNK, width), lambda i: (i, 0))
    state_spec = pl.BlockSpec((SAMPLE_SEQS, RET_HEADS, RET_DK, RET_DV), lambda i: (i, 0, 0, 0))
    y_sample, st_sample, v_sample = pl.pallas_call(
        functools.partial(_sample_kernel, g_seq),
        grid=(n_tok // CHUNK,),
        in_specs=[tok_spec(D_MODEL), tok_spec(PLE_DIM), _const_spec((CHUNK, LANES)), _const_spec((CHUNK, LANES)),
                  state_spec] + weight_specs,
        out_specs=[tok_spec(D_MODEL), state_spec, tok_spec(SGU_WIDTH)],
        out_shape=[jax.ShapeDtypeStruct((n_tok, D_MODEL), F32),
                   jax.ShapeDtypeStruct((dec_batch, RET_HEADS, RET_DK, RET_DV), F32),
                   jax.ShapeDtypeStruct((n_tok, SGU_WIDTH), F32)],
        scratch_shapes=[pltpu.VMEM((CHUNK, IN_WIDTH), F32), pltpu.VMEM((CHUNK, D_MODEL), BF16),
                        pltpu.VMEM((SGU_GROUPS // 2, CHUNK, 2 * CHUNK), BF16)],
        compiler_params=pltpu.CompilerParams(dimension_semantics=("arbitrary",),
                                             vmem_limit_bytes=VMEM_LIMIT_BYTES),
        name="sample_layer",
    )(x_sample.reshape(n_tok, D_MODEL), p_sample[0].reshape(n_tok, PLE_DIM), cos_s, sin_s, state_ret[0],
      *weights, bias_s, wpair_s, _mix_mask(dec_seq), tabs_s)

    return (y_prompt, y_sample.reshape(dec_batch, dec_seq, D_MODEL), st_prompt[None], st_sample[None],
            v_sample.reshape(1, dec_batch, dec_seq, SGU_WIDTH))
```

```python
import functools

import numpy as np
import jax
import jax.numpy as jnp
from jax import lax
from jax.experimental import pallas as pl
from jax.experimental.pallas import tpu as pltpu

F32 = jnp.float32
BF16 = jnp.bfloat16

D_MODEL = 1024
PAST_LEN = 16384
SGU_WIDTH = 512
SGU_GROUPS = 8
SGU_GROUP_DIM = SGU_WIDTH // SGU_GROUPS
RET_HEADS = 4
RET_DK = 128
RET_DV = 128
CHUNK = 128
ROPE_THETA = 10000.0
PLE_DIM = 256
RMS_EPS = 1e-6
LN_EPS = 1e-5
IN_WIDTH = 3 * SGU_WIDTH + RET_HEADS * (2 * RET_DK + 2 * RET_DV)
O_SU, O_SV, O_SG = 0, SGU_WIDTH, 2 * SGU_WIDTH
O_Q = 3 * SGU_WIDTH
O_K = O_Q + RET_HEADS * RET_DK
O_V = O_K + RET_HEADS * RET_DK
O_RG = O_V + RET_HEADS * RET_DV
IN_PIECE_ORDER = (O_SV, O_Q, O_K, O_V, O_SU, O_SG, O_RG)

LANES = 128
VMEM_LIMIT_BYTES = 56 * 1024 * 1024

PROMPT_TILE = 512
PROMPT_BLOCK = 256
PIECE_COLS = 512
SAMPLE_SEQS = CHUNK // 8


def _log_gamma():
    return np.log(1.0 - 2.0 ** (-5.0 - np.arange(RET_HEADS, dtype=np.float64)))


def _retention_tables(seg):
    lg = _log_gamma()[:, None, None]
    r = np.arange(CHUNK)
    i, j = r[:, None] % seg, r[None, :] % seg
    same = (r[:, None] // seg) == (r[None, :] // seg)
    decay = np.where(same & (i >= j), np.exp(lg * np.maximum(i - j, 0)), 0.0)
    wq = np.broadcast_to(np.exp(lg * (i + 1.0)), (RET_HEADS, CHUNK, LANES))
    wkv = np.broadcast_to(np.exp(lg * (seg - 1.0 - i)), (RET_HEADS, CHUNK, LANES))
    tabs = np.stack([decay, wq, wkv]).astype(np.float32)
    g_seg = [float(v) for v in np.exp(_log_gamma() * seg)]
    return tabs, g_seg


def _rotary_tables(pos):
    half = RET_DK // 2
    inv = ROPE_THETA ** (-np.arange(half, dtype=np.float64) / half)
    ang = pos.astype(np.float64)[:, None] * inv[None, :]
    cos, sin = np.cos(ang), np.sin(ang)
    return (np.concatenate([cos, cos], axis=1).astype(np.float32),
            np.concatenate([-sin, sin], axis=1).astype(np.float32))


def _mix_mask(seg):
    r = np.arange(CHUNK)
    same = (r[:, None] // seg) == (r[None, :] // seg)
    m = same & ((r[None, :] % seg) <= (r[:, None] % seg))
    return np.concatenate([m, m], axis=1).astype(np.float32)


def _gelu(x):
    return 0.5 * x * (1.0 + jnp.tanh(np.sqrt(2.0 / np.pi) * (x + 0.044715 * (x * x * x))))


def _silu(x):
    return x * jax.nn.sigmoid(x)


def _rotate(x, cos, sin_signed):
    return x * cos + pltpu.roll(x, RET_DK // 2, 1) * sin_signed


def _normed_bf16(x, norm_ref):
    ms = jnp.mean(x * x, axis=-1, keepdims=True)
    return (x * lax.rsqrt(ms + RMS_EPS) * norm_ref[...]).astype(BF16)


def _in_pieces(x, norm_pre_ref, w_in_ref, z_ref):
    box = {}

    def piece(c0):
        def run():
            if 'h' not in box:
                box['h'] = _normed_bf16(x, norm_pre_ref)
            z_ref[:, c0:c0 + PIECE_COLS] = jnp.dot(box['h'], w_in_ref[:, c0:c0 + PIECE_COLS],
                                                   preferred_element_type=F32)
        return run
    assert PIECE_COLS == SGU_WIDTH == RET_HEADS * RET_DK == RET_HEADS * RET_DV
    return [piece(c0) for c0 in IN_PIECE_ORDER]


def _out_pieces(x, p, mixin_ref, w_out_ref, norm_post_ref, w_gate_ref, w_ple_ref, y_ref):
    box = {}
    halves = [slice(c0, c0 + PIECE_COLS) for c0 in range(0, D_MODEL, PIECE_COLS)]

    def ple():
        box['ple'] = jnp.dot(p.astype(BF16), w_ple_ref[...], preferred_element_type=F32)

    def mix(i):
        def run():
            box['mix%d' % i] = jnp.dot(mixin_ref[...], w_out_ref[:, halves[i]], preferred_element_type=F32)
        return run

    def gate(i):
        def run():
            if 'x1' not in box:
                mixes = [box['mix%d' % j] for j in range(len(halves))]
                ms = sum(jnp.sum(m * m, axis=-1, keepdims=True) for m in mixes) * (1.0 / D_MODEL)
                scale = lax.rsqrt(ms + RMS_EPS)
                box['x1'] = [x[:, halves[j]] + mixes[j] * scale * norm_post_ref[:, halves[j]]
                             for j in range(len(halves))]
                box['x1b'] = jnp.concatenate(box['x1'], axis=1).astype(BF16)
            g = jnp.dot(box['x1b'], w_gate_ref[:, halves[i]], preferred_element_type=F32)
            y_ref[:, halves[i]] = box['x1'][i] + jax.nn.sigmoid(g) * box['ple'][:, halves[i]]
        return run
    return [ple] + [mix(i) for i in range(len(halves))] + [gate(i) for i in range(len(halves))]


def _chunk_stages(z_ref, rows, cos_ref, sin_ref, ln_g_ref, gn_ref, bias_ref, wpair_ref, tabs_ref,
                  state_update, state_output, mixin_ref, vrows_ref):
    box = {}
    zs = lambda off, i, width: z_ref[rows, off + i * width:off + (i + 1) * width]

    def gating_in():
        sv = _gelu(z_ref[rows, O_SV:O_SV + SGU_WIDTH])
        mu = jnp.mean(sv, axis=-1, keepdims=True)
        cen = sv - mu
        var = jnp.mean(cen * cen, axis=-1, keepdims=True)
        vn = cen * lax.rsqrt(var + LN_EPS) * ln_g_ref[...]
        if vrows_ref is not None:
            vrows_ref[rows, :] = vn
        low_lanes = lax.broadcasted_iota(jnp.int32, (CHUNK, LANES), 1) < SGU_GROUP_DIM
        for m in range(SGU_GROUPS // 2):
            blk = vn[:, m * LANES:(m + 1) * LANES]
            rhs = jnp.concatenate([jnp.where(low_lanes, blk, 0.0), jnp.where(low_lanes, 0.0, blk)],
                                  axis=0).astype(BF16)
            box['mixed%d' % m] = jnp.dot(wpair_ref[m], rhs, preferred_element_type=F32)

    def retention_in():
        cos, sin_signed = cos_ref[rows, :], sin_ref[rows, :]
        for h in range(RET_HEADS):
            q = _rotate(zs(O_Q, h, RET_DK), cos, sin_signed)
            k = _rotate(zs(O_K, h, RET_DK), cos, sin_signed) * (RET_DK ** -0.5)
            v = zs(O_V, h, RET_DV)
            k_b = k.astype(BF16)
            box['scores%d' % h] = lax.dot_general(q.astype(BF16), k_b, (((1,), (1,)), ((), ())),
                                                  preferred_element_type=F32)
            box['qw%d' % h] = (q * tabs_ref[1, h]).astype(BF16)
            box['v%d' % h] = v.astype(BF16)
            box['u%d' % h] = state_update(h, k_b, (v * tabs_ref[2, h]).astype(BF16))

    def gating_out_retention_mid():
        for m in range(SGU_GROUPS // 2):
            cols = slice(m * LANES, (m + 1) * LANES)
            mixed = box['mixed%d' % m] + bias_ref[:, cols]
            su = _gelu(zs(O_SU, m, LANES))
            mixin_ref[rows, cols] = (_silu(zs(O_SG, m, LANES)) * (su * mixed)).astype(BF16)
        for h in range(RET_HEADS):
            scores_b = (box['scores%d' % h] * tabs_ref[0, h]).astype(BF16)
            box['o%d' % h] = state_output(h, scores_b, box['qw%d' % h], box['v%d' % h], box['u%d' % h])

    def retention_out():
        for h in range(RET_HEADS):
            o = box['o%d' % h]
            mu = jnp.mean(o, axis=-1, keepdims=True)
            cen = o - mu
            var = jnp.mean(cen * cen, axis=-1, keepdims=True)
            on = cen * lax.rsqrt(var + LN_EPS) * gn_ref[:, h * RET_DV:(h + 1) * RET_DV]
            mixin_ref[rows, SGU_WIDTH + h * RET_DV:SGU_WIDTH + (h + 1) * RET_DV] = (
                _silu(zs(O_RG, h, RET_DV)) * on).astype(BF16)

    return [gating_in, retention_in, gating_out_retention_mid, retention_out]


def _interleave(pieces, stages):
    pieces, stages = list(pieces), list(stages)
    while pieces or stages:
        if pieces:
            pieces.pop(0)()
        if stages:
            stages.pop(0)()


def _init_masked_pairs(wpair_ref, mask_ref, wpm_ref):
    for m in range(SGU_GROUPS // 2):
        wpm_ref[m] = jnp.where(mask_ref[...] > 0.0, wpair_ref[m], jnp.zeros((), BF16))


def _prompt_kernel(g_chunk, x_ref, p_ref, cos_ref, sin_ref, w_in_ref, w_out_ref, w_gate_ref, w_ple_ref,
                   norm_pre_ref, norm_post_ref, ln_g_ref, gn_ref, bias_ref, wpair_ref, mask_ref, tabs_ref,
                   y_ref, state_ref, z_ref, mixin_ref, wpm_ref):
    b, t = pl.program_id(0), pl.program_id(1)

    @pl.when((b == 0) & (t == 0))
    def _():
        _init_masked_pairs(wpair_ref, mask_ref, wpm_ref)

    @pl.when(t == 0)
    def _():
        state_ref[...] = jnp.zeros_like(state_ref)

    def state_update(h, k_b, vw_b):
        return lax.dot_general(k_b, vw_b, (((0,), (0,)), ((), ())), preferred_element_type=F32)

    def state_output(h, scores_b, qw_b, v_b, u):
        s = state_ref[0, h]
        o = jnp.dot(jnp.concatenate([scores_b, qw_b], axis=1),
                    jnp.concatenate([v_b, s.astype(BF16)], axis=0), preferred_element_type=F32)
        state_ref[0, h] = g_chunk[h] * s + u
        return o

    def in_pieces(blk):
        return _in_pieces(x_ref[0, blk, :], norm_pre_ref, w_in_ref, z_ref.at[blk, :])

    def out_pieces(blk):
        return _out_pieces(x_ref[0, blk, :], p_ref[0, blk, :], mixin_ref.at[blk, :], w_out_ref, norm_post_ref,
                           w_gate_ref, w_ple_ref, y_ref.at[0, blk, :])

    def stages(blk):
        per_chunk = [_chunk_stages(z_ref, slice(r, r + CHUNK), cos_ref, sin_ref, ln_g_ref, gn_ref, bias_ref,
                                   wpm_ref, tabs_ref, state_update, state_output, mixin_ref, None)
                     for r in range(blk.start, blk.stop, CHUNK)]
        return [stage for group in zip(*per_chunk) for stage in group]

    blk_a, blk_b = [slice(r, r + PROMPT_BLOCK) for r in range(0, PROMPT_TILE, PROMPT_BLOCK)]
    in_a, in_b = in_pieces(blk_a), in_pieces(blk_b)
    out_a, out_b = out_pieces(blk_a), out_pieces(blk_b)
    st_a, st_b = stages(blk_a), stages(blk_b)
    in_a.pop(0)()
    _interleave(in_a + in_b[:2], st_a)
    _interleave(in_b[2:] + out_a[:3], st_b)
    _interleave(out_a[3:] + out_b, [])


def _sample_kernel(g_seq, x_ref, p_ref, cos_ref, sin_ref, st_in_ref, w_in_ref, w_out_ref, w_gate_ref,
                   w_ple_ref, norm_pre_ref, norm_post_ref, ln_g_ref, gn_ref, bias_ref, wpair_ref, mask_ref,
                   tabs_ref, y_ref, st_out_ref, vrows_ref, z_ref, mixin_ref, wpm_ref):
    @pl.when(pl.program_id(0) == 0)
    def _():
        _init_masked_pairs(wpair_ref, mask_ref, wpm_ref)

    seq_len = CHUNK // SAMPLE_SEQS
    per_seq = lambda a: a.astype(F32).reshape(SAMPLE_SEQS, seq_len, a.shape[-1]).astype(BF16)

    def state_update(h, k_b, vw_b):
        u = jnp.einsum('sjd,sje->sde', per_seq(k_b), per_seq(vw_b), preferred_element_type=F32)
        st_out_ref[:, h] = g_seq[h] * st_in_ref[:, h] + u
        return None

    def state_output(h, scores_b, qw_b, v_b, _):
        o = jnp.dot(scores_b, v_b, preferred_element_type=F32)
        o_inter = jnp.einsum('sid,sde->sie', per_seq(qw_b), st_in_ref[:, h].astype(BF16),
                             preferred_element_type=F32)
        return o + o_inter.reshape(CHUNK, RET_DV)

    _interleave(_in_pieces(x_ref[...], norm_pre_ref, w_in_ref, z_ref), [])
    _interleave([], _chunk_stages(z_ref, slice(0, CHUNK), cos_ref, sin_ref, ln_g_ref, gn_ref, bias_ref, wpm_ref,
                                  tabs_ref, state_update, state_output, mixin_ref, vrows_ref))
    _interleave(_out_pieces(x_ref[...], p_ref[...], mixin_ref, w_out_ref, norm_post_ref, w_gate_ref, w_ple_ref,
                            y_ref), [])


def _const_spec(shape):
    return pl.BlockSpec(shape, lambda *_: (0,) * len(shape))


def _pair_groups(w):
    return jnp.concatenate([w[0::2], w[1::2]], axis=2).astype(BF16)


def kernel(x_prompt, x_sample, state_ret, p_prompt, p_sample, w_in, w_out, norm_pre, norm_post, sgu_w, sgu_b,
           sgu_ln, ret_gn, w_ple_proj, w_ple_gate):
    batch, seq, _ = x_prompt.shape
    dec_batch, dec_seq, _ = x_sample.shape
    assert seq % PROMPT_TILE == 0 and CHUNK % dec_seq == 0 and CHUNK // dec_seq == SAMPLE_SEQS
    assert dec_batch % SAMPLE_SEQS == 0 and w_in.shape[0] == 1

    w_in_b, w_out_b = w_in[0].astype(BF16), w_out[0].astype(BF16)
    w_gate_b, w_ple_b = w_ple_gate[0].astype(BF16), w_ple_proj[0].astype(BF16)
    norm_pre2, norm_post2 = norm_pre[0][None, :], norm_post[0][None, :]
    ln_g2, gn2 = sgu_ln[0][None, :], ret_gn[0][None, :]

    weight_specs = [
        _const_spec((D_MODEL, IN_WIDTH)), _const_spec((D_MODEL, D_MODEL)), _const_spec((D_MODEL, D_MODEL)),
        _const_spec((PLE_DIM, D_MODEL)), _const_spec((1, D_MODEL)), _const_spec((1, D_MODEL)),
        _const_spec((1, SGU_WIDTH)), _const_spec((1, RET_HEADS * RET_DV)), _const_spec((CHUNK, SGU_WIDTH)),
        _const_spec((SGU_GROUPS // 2, CHUNK, 2 * CHUNK)), _const_spec((CHUNK, 2 * CHUNK)),
        _const_spec((3, RET_HEADS, CHUNK, LANES)),
    ]
    weights = (w_in_b, w_out_b, w_gate_b, w_ple_b, norm_pre2, norm_post2, ln_g2, gn2)

    tabs_p, g_chunk = _retention_tables(CHUNK)
    cos_p, sin_p = _rotary_tables(np.arange(seq))
    bias_p = jnp.repeat(sgu_b[0].T, SGU_GROUP_DIM, axis=1)
    wpair_p = _pair_groups(sgu_w[0])
    n_tiles = seq // PROMPT_TILE
    tile_spec = lambda width: pl.BlockSpec((1, PROMPT_TILE, width), lambda b, t: (b, t, 0))
    pos_spec = pl.BlockSpec((PROMPT_TILE, LANES), lambda b, t: (t, 0))
    y_prompt, st_prompt = pl.pallas_call(
        functools.partial(_prompt_kernel, g_chunk),
        grid=(batch, n_tiles),
        in_specs=[tile_spec(D_MODEL), tile_spec(PLE_DIM), pos_spec, pos_spec] + weight_specs,
        out_specs=[tile_spec(D_MODEL),
                   pl.BlockSpec((1, RET_HEADS, RET_DK, RET_DV), lambda b, t: (b, 0, 0, 0))],
        out_shape=[jax.ShapeDtypeStruct((batch, seq, D_MODEL), F32),
                   jax.ShapeDtypeStruct((batch, RET_HEADS, RET_DK, RET_DV), F32)],
        scratch_shapes=[pltpu.VMEM((PROMPT_TILE, IN_WIDTH), F32), pltpu.VMEM((PROMPT_TILE, D_MODEL), BF16),
                        pltpu.VMEM((SGU_GROUPS // 2, CHUNK, 2 * CHUNK), BF16)],
        compiler_params=pltpu.CompilerParams(dimension_semantics=("arbitrary", "arbitrary"),
                                             vmem_limit_bytes=VMEM_LIMIT_BYTES),
        name="prompt_layer",
    )(x_prompt, p_prompt[0], cos_p, sin_p, *weights, bias_p, wpair_p, _mix_mask(CHUNK), tabs_p)

    tabs_s, g_seq = _retention_tables(dec_seq)
    cos_s, sin_s = _rotary_tables(PAST_LEN + np.arange(CHUNK) % dec_seq)
    bias_s = jnp.tile(jnp.repeat(sgu_b[0, :, :dec_seq].T, SGU_GROUP_DIM, axis=1), (SAMPLE_SEQS, 1))
    wpair_s = _pair_groups(jnp.tile(sgu_w[0, :, :dec_seq, :dec_seq], (1, SAMPLE_SEQS, SAMPLE_SEQS)))
    n_tok = dec_batch * dec_seq
    tok_spec = lambda width: pl.BlockSpec((CHUNK, width), lambda i: (i, 0))
    state_spec = pl.BlockSpec((SAMPLE_SEQS, RET_HEADS, RET_DK, RET_DV), lambda i: (i, 0, 0, 0))
    y_sample, st_sample, v_sample = pl.pallas_call(
        functools.partial(_sample_kernel, g_seq),
        grid=(n_tok // CHUNK,),
        in_specs=[tok_spec(D_MODEL), tok_spec(PLE_DIM), _const_spec((CHUNK, LANES)), _const_spec((CHUNK, LANES)),
                  state_spec] + weight_specs,
        out_specs=[tok_spec(D_MODEL), state_spec, tok_spec(SGU_WIDTH)],
        out_shape=[jax.ShapeDtypeStruct((n_tok, D_MODEL), F32),
                   jax.ShapeDtypeStruct((dec_batch, RET_HEADS, RET_DK, RET_DV), F32),
                   jax.ShapeDtypeStruct((n_tok, SGU_WIDTH), F32)],
        scratch_shapes=[pltpu.VMEM((CHUNK, IN_WIDTH), F32), pltpu.VMEM((CHUNK, D_MODEL), BF16),
                        pltpu.VMEM((SGU_GROUPS // 2, CHUNK, 2 * CHUNK), BF16)],
        compiler_params=pltpu.CompilerParams(dimension_semantics=("arbitrary",),
                                             vmem_limit_bytes=VMEM_LIMIT_BYTES),
        name="sample_layer",
    )(x_sample.reshape(n_tok, D_MODEL), p_sample[0].reshape(n_tok, PLE_DIM), cos_s, sin_s, state_ret[0],
      *weights, bias_s, wpair_s, _mix_mask(dec_seq), tabs_s)

    return (y_prompt, y_sample.reshape(dec_batch, dec_seq, D_MODEL), st_prompt[None], st_sample[None],
            v_sample.reshape(1, dec_batch, dec_seq, SGU_WIDTH))
```

```python
import functools

import numpy as np
import jax
import jax.numpy as jnp
from jax import lax
from jax.experimental import pallas as pl
from jax.experimental.pallas import tpu as pltpu

F32 = jnp.float32
BF16 = jnp.bfloat16

D_MODEL = 1024
PAST_LEN = 16384
SGU_WIDTH = 512
SGU_GROUPS = 8
SGU_GROUP_DIM = SGU_WIDTH // SGU_GROUPS
RET_HEADS = 4
RET_DK = 128
RET_DV = 128
CHUNK = 128
ROPE_THETA = 10000.0
PLE_DIM = 256
RMS_EPS = 1e-6
LN_EPS = 1e-5
IN_WIDTH = 3 * SGU_WIDTH + RET_HEADS * (2 * RET_DK + 2 * RET_DV)
O_SU, O_SV, O_SG = 0, SGU_WIDTH, 2 * SGU_WIDTH
O_Q = 3 * SGU_WIDTH
O_K = O_Q + RET_HEADS * RET_DK
O_V = O_K + RET_HEADS * RET_DK
O_RG = O_V + RET_HEADS * RET_DV

LANES = 128
VMEM_LIMIT_BYTES = 60 * 1024 * 1024

PROMPT_TILE = 512
BLOCK_ROWS = {"prompt": 256, "sample": CHUNK}
PIECE_COLS = 512
CAST_STEPS = 8
SAMPLE_TILE = 2 * CHUNK


def _log_gamma():
    return np.log(1.0 - 2.0 ** (-5.0 - np.arange(RET_HEADS, dtype=np.float64)))


def _retention_tables(seg):
    lg = _log_gamma()[:, None, None]
    r = np.arange(CHUNK)
    i, j = r[:, None] % seg, r[None, :] % seg
    same = (r[:, None] // seg) == (r[None, :] // seg)
    decay = np.where(same & (i >= j), np.exp(lg * np.maximum(i - j, 0)), 0.0)
    wq = np.broadcast_to(np.exp(lg * (i + 1.0)), (RET_HEADS, CHUNK, LANES))
    wkv = np.broadcast_to(np.exp(lg * (seg - 1.0 - i)), (RET_HEADS, CHUNK, LANES))
    tabs = np.stack([decay, wq, wkv]).astype(np.float32)
    g_seg = [float(v) for v in np.exp(_log_gamma() * seg)]
    return tabs, g_seg


def _rotary_tables(pos):
    half = RET_DK // 2
    inv = ROPE_THETA ** (-np.arange(half, dtype=np.float64) / half)
    ang = pos.astype(np.float64)[:, None] * inv[None, :]
    cos, sin = np.cos(ang), np.sin(ang)
    return (np.concatenate([cos, cos], axis=1).astype(np.float32),
            np.concatenate([-sin, sin], axis=1).astype(np.float32))


def _mix_mask(seg):
    r = np.arange(CHUNK)
    same = (r[:, None] // seg) == (r[None, :] // seg)
    m = same & ((r[None, :] % seg) <= (r[:, None] % seg))
    return np.concatenate([m, m], axis=1).astype(np.float32)


def _gelu(x):
    return 0.5 * x * (1.0 + jnp.tanh(np.sqrt(2.0 / np.pi) * (x + 0.044715 * (x * x * x))))


def _silu(x):
    return x * jax.nn.sigmoid(x)


def _rotate(x, cos, sin_signed):
    return x * cos + pltpu.roll(x, RET_DK // 2, 1) * sin_signed


def _normed_bf16(x, norm_ref):
    ms = jnp.mean(x * x, axis=-1, keepdims=True)
    return (x * lax.rsqrt(ms + RMS_EPS) * norm_ref[...]).astype(BF16)


def _in_pieces(x, norm_pre_ref, w_in_ref, z_ref):
    box = {}

    def piece(c0):
        def run():
            if 'h' not in box:
                box['h'] = _normed_bf16(x, norm_pre_ref)
            z_ref[:, c0:c0 + PIECE_COLS] = jnp.dot(box['h'], w_in_ref[:, c0:c0 + PIECE_COLS],
                                                   preferred_element_type=F32)
        return run
    return [piece(c0) for c0 in range(0, IN_WIDTH, PIECE_COLS)]


def _out_pieces(x, p, mixin_ref, w_out_ref, norm_post_ref, w_gate_ref, w_ple_ref, y_ref):
    box = {}
    halves = [slice(c0, c0 + PIECE_COLS) for c0 in range(0, D_MODEL, PIECE_COLS)]

    def ple():
        box['ple'] = jnp.dot(p.astype(BF16), w_ple_ref[...], preferred_element_type=F32)

    def mix(i):
        def run():
            box['mix%d' % i] = jnp.dot(mixin_ref[...], w_out_ref[:, halves[i]], preferred_element_type=F32)
        return run

    def gate(i):
        def run():
            if 'x1' not in box:
                mixes = [box['mix%d' % j] for j in range(len(halves))]
                ms = sum(jnp.sum(m * m, axis=-1, keepdims=True) for m in mixes) * (1.0 / D_MODEL)
                scale = lax.rsqrt(ms + RMS_EPS)
                box['x1'] = [x[:, halves[j]] + mixes[j] * scale * norm_post_ref[:, halves[j]]
                             for j in range(len(halves))]
                box['x1b'] = jnp.concatenate(box['x1'], axis=1).astype(BF16)
            g = jnp.dot(box['x1b'], w_gate_ref[:, halves[i]], preferred_element_type=F32)
            y_ref[:, halves[i]] = box['x1'][i] + jax.nn.sigmoid(g) * box['ple'][:, halves[i]]
        return run
    return [ple] + [mix(i) for i in range(len(halves))] + [gate(i) for i in range(len(halves))]


def _chunk_stages(z_ref, rows, cos, sin_signed, ln_g_ref, gn_ref, bias_ref, wpair_ref, tabs_ref,
                  state_update, state_output, mixin_ref, vrows_ref):
    box = {}
    zs = lambda off, i, width: z_ref[rows, off + i * width:off + (i + 1) * width]

    def gating_in():
        sv = _gelu(z_ref[rows, O_SV:O_SV + SGU_WIDTH])
        mu = jnp.mean(sv, axis=-1, keepdims=True)
        cen = sv - mu
        var = jnp.mean(cen * cen, axis=-1, keepdims=True)
        vn = cen * lax.rsqrt(var + LN_EPS) * ln_g_ref[...]
        if vrows_ref is not None:
            vrows_ref[rows, :] = vn
        low_lanes = lax.broadcasted_iota(jnp.int32, (CHUNK, LANES), 1) < SGU_GROUP_DIM
        for m in range(SGU_GROUPS // 2):
            blk = vn[:, m * LANES:(m + 1) * LANES]
            rhs = jnp.concatenate([jnp.where(low_lanes, blk, 0.0), jnp.where(low_lanes, 0.0, blk)],
                                  axis=0).astype(BF16)
            box['mixed%d' % m] = jnp.dot(wpair_ref[m], rhs, preferred_element_type=F32)

    def retention_in():
        c, s = cos(), sin_signed()
        for h in range(RET_HEADS):
            q = _rotate(zs(O_Q, h, RET_DK), c, s)
            k = _rotate(zs(O_K, h, RET_DK), c, s) * (RET_DK ** -0.5)
            v = zs(O_V, h, RET_DV)
            k_b = k.astype(BF16)
            box['scores%d' % h] = lax.dot_general(q.astype(BF16), k_b, (((1,), (1,)), ((), ())),
                                                  preferred_element_type=F32)
            box['qw%d' % h] = (q * tabs_ref[1, h]).astype(BF16)
            box['v%d' % h] = v.astype(BF16)
            box['u%d' % h] = state_update(h, k_b, (v * tabs_ref[2, h]).astype(BF16))

    def gating_out_retention_mid():
        for m in range(SGU_GROUPS // 2):
            cols = slice(m * LANES, (m + 1) * LANES)
            mixed = box['mixed%d' % m] + bias_ref[:, cols]
            su = _gelu(zs(O_SU, m, LANES))
            mixin_ref[rows, cols] = (_silu(zs(O_SG, m, LANES)) * (su * mixed)).astype(BF16)
        for h in range(RET_HEADS):
            scores_b = (box['scores%d' % h] * tabs_ref[0, h]).astype(BF16)
            box['o%d' % h] = state_output(h, scores_b, box['qw%d' % h], box['v%d' % h], box['u%d' % h])

    def retention_out():
        for h in range(RET_HEADS):
            o = box['o%d' % h]
            mu = jnp.mean(o, axis=-1, keepdims=True)
            cen = o - mu
            var = jnp.mean(cen * cen, axis=-1, keepdims=True)
            on = cen * lax.rsqrt(var + LN_EPS) * gn_ref[:, h * RET_DV:(h + 1) * RET_DV]
            mixin_ref[rows, SGU_WIDTH + h * RET_DV:SGU_WIDTH + (h + 1) * RET_DV] = (
                _silu(zs(O_RG, h, RET_DV)) * on).astype(BF16)

    return [gating_in, retention_in, gating_out_retention_mid, retention_out]


def _interleave(pieces, stages):
    pieces, stages = list(pieces), list(stages)
    while pieces or stages:
        if pieces:
            pieces.pop(0)()
        if stages:
            stages.pop(0)()


def _emit_two_blocks(in_pieces, out_pieces, stages, blk_a, blk_b):
    _interleave(in_pieces(blk_a), [])
    _interleave(in_pieces(blk_b), stages(blk_a))
    _interleave(out_pieces(blk_a), stages(blk_b))
    _interleave(out_pieces(blk_b), [])


def _init_tables(seg, sgu_w_ref, sgu_b_ref, mask_ref, wpm_ref, bias_ref):
    rows = lax.broadcasted_iota(jnp.int32, (CHUNK, LANES), 0)
    lanes = lax.broadcasted_iota(jnp.int32, (CHUNK, LANES), 1)
    if seg == CHUNK:
        group = lambda g: sgu_w_ref[g]
        bias_rows = sgu_b_ref[...]
    else:
        reps = CHUNK // seg
        select = jnp.where((rows < seg) & (lanes % seg == rows), 1.0, 0.0).astype(BF16)
        first = lambda a: jnp.where(lanes[:seg] < seg, a, 0.0)

        def group(g):
            stacked = jnp.concatenate([first(sgu_w_ref[g, 0:seg, :])] * reps, axis=0)
            return jnp.dot(stacked.astype(BF16), select, preferred_element_type=F32)
        b_first = jnp.where(lax.broadcasted_iota(jnp.int32, (SGU_GROUPS, LANES), 1) < seg, sgu_b_ref[...], 0.0)
        bias_rows = sum(pltpu.roll(b_first, r * seg, 1) for r in range(reps))
    for m in range(SGU_GROUPS // 2):
        pair = jnp.concatenate([group(2 * m), group(2 * m + 1)], axis=1)
        wpm_ref[m] = jnp.where(mask_ref[...] > 0.0, pair, 0.0).astype(BF16)
    group_of_lane = lax.broadcasted_iota(jnp.int32, (SGU_GROUPS, SGU_WIDTH), 1) // SGU_GROUP_DIM
    expand = jnp.where(group_of_lane == lax.broadcasted_iota(jnp.int32, (SGU_GROUPS, SGU_WIDTH), 0),
                       1.0, 0.0).astype(BF16)
    total, rest = jnp.zeros((CHUNK, SGU_WIDTH), F32), bias_rows
    for _ in range(3):
        term = rest.astype(BF16)
        total = total + lax.dot_general(term, expand, (((0,), (0,)), ((), ())), preferred_element_type=F32)
        rest = rest - term.astype(F32)
    bias_ref[...] = total


def _prompt_kernel(g_chunk, n_tiles, x_ref, p_ref, cos_ref, sin_ref, w_in_ref, w_out_ref, w_gate_ref, w_ple_ref,
                   norm_pre_ref, norm_post_ref, ln_g_ref, gn_ref, sgu_w_ref, sgu_b_ref, mask_ref, tabs_ref,
                   y_ref, state_ref, w_in_b_ref, w_out_b_ref, w_gate_b_ref, w_ple_b_ref,
                   z_ref, mixin_ref, wpm_ref, bias_ref):
    step = pl.program_id(0)

    @pl.when(step == 0)
    def _():
        _init_tables(CHUNK, sgu_w_ref, sgu_b_ref, mask_ref, wpm_ref, bias_ref)

    @pl.when(step < CAST_STEPS)
    def _():
        for src, dst in ((w_in_ref, w_in_b_ref), (w_out_ref, w_out_b_ref), (w_gate_ref, w_gate_b_ref),
                         (w_ple_ref, w_ple_b_ref)):
            n = src.shape[0]
            dst[pl.ds(pl.multiple_of(step * n, n), n), :] = src[...].astype(BF16)

    @pl.when(step >= CAST_STEPS)
    def _():
        @pl.when((step - CAST_STEPS) % n_tiles == 0)
        def _():
            state_ref[...] = jnp.zeros_like(state_ref)

        def state_update(h, k_b, vw_b):
            return lax.dot_general(k_b, vw_b, (((0,), (0,)), ((), ())), preferred_element_type=F32)

        def state_output(h, scores_b, qw_b, v_b, u):
            s = state_ref[0, h]
            o = jnp.dot(jnp.concatenate([scores_b, qw_b], axis=1),
                        jnp.concatenate([v_b, s.astype(BF16)], axis=0), preferred_element_type=F32)
            state_ref[0, h] = g_chunk[h] * s + u
            return o

        def in_pieces(blk):
            return _in_pieces(x_ref[0, blk, :], norm_pre_ref, w_in_b_ref, z_ref.at[blk, :])

        def out_pieces(blk):
            return _out_pieces(x_ref[0, blk, :], p_ref[0, blk, :], mixin_ref.at[blk, :], w_out_b_ref,
                               norm_post_ref, w_gate_b_ref, w_ple_b_ref, y_ref.at[0, blk, :])

        def stages(blk):
            out = []
            for r in range(blk.start, blk.stop, CHUNK):
                rows = slice(r, r + CHUNK)
                out += _chunk_stages(z_ref, rows, lambda rows=rows: cos_ref[rows, :],
                                     lambda rows=rows: sin_ref[rows, :], ln_g_ref, gn_ref, bias_ref, wpm_ref,
                                     tabs_ref, state_update, state_output, mixin_ref, None)
            return out

        rows = BLOCK_ROWS["prompt"]
        _emit_two_blocks(in_pieces, out_pieces, stages, slice(0, rows), slice(rows, 2 * rows))


def _sample_kernel(g_seq, x_ref, p_ref, cos_ref, sin_ref, st_in_ref, w_in_ref, w_out_ref, w_gate_ref,
                   w_ple_ref, norm_pre_ref, norm_post_ref, ln_g_ref, gn_ref, sgu_w_ref, sgu_b_ref, mask_ref,
                   tabs_ref, y_ref, st_out_ref, vrows_ref, z_ref, mixin_ref, wpm_ref, bias_ref):
    seq_len = cos_ref.shape[0]
    seqs = CHUNK // seq_len

    @pl.when(pl.program_id(0) == 0)
    def _():
        _init_tables(seq_len, sgu_w_ref, sgu_b_ref, mask_ref, wpm_ref, bias_ref)

    per_seq = lambda a: a.astype(F32).reshape(seqs, seq_len, a.shape[-1]).astype(BF16)
    tile_rows = lambda ref: jnp.concatenate([ref[...]] * seqs, axis=0)

    def stages(blk):
        sq = slice(blk.start // seq_len, blk.stop // seq_len)

        def state_update(h, k_b, vw_b):
            u = jnp.einsum('sjd,sje->sde', per_seq(k_b), per_seq(vw_b), preferred_element_type=F32)
            st_out_ref[sq, h] = g_seq[h] * st_in_ref[sq, h] + u
            return None

        def state_output(h, scores_b, qw_b, v_b, _):
            o = jnp.dot(scores_b, v_b, preferred_element_type=F32)
            o_inter = jnp.einsum('sid,sde->sie', per_seq(qw_b), st_in_ref[sq, h].astype(BF16),
                                 preferred_element_type=F32)
            return o + o_inter.reshape(CHUNK, RET_DV)

        return _chunk_stages(z_ref, blk, lambda: tile_rows(cos_ref), lambda: tile_rows(sin_ref), ln_g_ref,
                             gn_ref, bias_ref, wpm_ref, tabs_ref, state_update, state_output, mixin_ref,
                             vrows_ref)

    def in_pieces(blk):
        return _in_pieces(x_ref[blk, :], norm_pre_ref, w_in_ref, z_ref.at[blk, :])

    def out_pieces(blk):
        return _out_pieces(x_ref[blk, :], p_ref[blk, :], mixin_ref.at[blk, :], w_out_ref, norm_post_ref,
                           w_gate_ref, w_ple_ref, y_ref.at[blk, :])

    rows = BLOCK_ROWS["sample"]
    _emit_two_blocks(in_pieces, out_pieces, stages, slice(0, rows), slice(rows, 2 * rows))


def _const_spec(shape):
    return pl.BlockSpec(shape, lambda *_: (0,) * len(shape), pipeline_mode=pl.Buffered(1))


def kernel(x_prompt, x_sample, state_ret, p_prompt, p_sample, w_in, w_out, norm_pre, norm_post, sgu_w, sgu_b,
           sgu_ln, ret_gn, w_ple_proj, w_ple_gate):
    batch, seq, _ = x_prompt.shape
    dec_batch, dec_seq, _ = x_sample.shape
    n_tiles = seq // PROMPT_TILE
    n_tok = dec_batch * dec_seq
    assert seq % PROMPT_TILE == 0 and PROMPT_TILE == 2 * BLOCK_ROWS["prompt"] and w_in.shape[0] == 1
    assert CHUNK % dec_seq == 0 and n_tok % SAMPLE_TILE == 0 and SAMPLE_TILE == 2 * BLOCK_ROWS["sample"]
    assert D_MODEL % CAST_STEPS == 0 and PLE_DIM % (16 * CAST_STEPS) == 0

    small = (norm_pre[0][None, :], norm_post[0][None, :], sgu_ln[0][None, :], ret_gn[0][None, :],
             sgu_w[0], sgu_b[0])
    small_specs = [_const_spec((1, D_MODEL)), _const_spec((1, D_MODEL)), _const_spec((1, SGU_WIDTH)),
                   _const_spec((1, RET_HEADS * RET_DV)), _const_spec((SGU_GROUPS, CHUNK, CHUNK)),
                   _const_spec((SGU_GROUPS, CHUNK)), _const_spec((CHUNK, 2 * CHUNK)),
                   _const_spec((3, RET_HEADS, CHUNK, LANES))]
    table_scratch = [pltpu.VMEM((SGU_GROUPS // 2, CHUNK, 2 * CHUNK), BF16), pltpu.VMEM((CHUNK, SGU_WIDTH), F32)]
    weight_shapes = [(D_MODEL, IN_WIDTH), (D_MODEL, D_MODEL), (D_MODEL, D_MODEL), (PLE_DIM, D_MODEL)]

    tabs_p, g_chunk = _retention_tables(CHUNK)
    cos_p, sin_p = _rotary_tables(np.arange(seq))
    tile_of = lambda s: jnp.maximum(s - CAST_STEPS, 0)
    tile_spec = lambda width: pl.BlockSpec((1, PROMPT_TILE, width),
                                           lambda s: (tile_of(s) // n_tiles, tile_of(s) % n_tiles, 0))
    pos_spec = pl.BlockSpec((PROMPT_TILE, LANES), lambda s: (tile_of(s) % n_tiles, 0))
    cast_spec = lambda shape: pl.BlockSpec((shape[0] // CAST_STEPS, shape[1]),
                                           lambda s: (jnp.minimum(s, CAST_STEPS - 1), 0))
    y_prompt, st_prompt, w_in_b, w_out_b, w_gate_b, w_ple_b = pl.pallas_call(
        functools.partial(_prompt_kernel, g_chunk, n_tiles),
        grid=(CAST_STEPS + batch * n_tiles,),
        in_specs=[tile_spec(D_MODEL), tile_spec(PLE_DIM), pos_spec, pos_spec]
        + [cast_spec(shape) for shape in weight_shapes] + small_specs,
        out_specs=[tile_spec(D_MODEL),
                   pl.BlockSpec((1, RET_HEADS, RET_DK, RET_DV), lambda s: (tile_of(s) // n_tiles, 0, 0, 0))]
        + [pl.BlockSpec(shape, lambda s: (0, 0)) for shape in weight_shapes],
        out_shape=[jax.ShapeDtypeStruct((batch, seq, D_MODEL), F32),
                   jax.ShapeDtypeStruct((batch, RET_HEADS, RET_DK, RET_DV), F32)]
        + [jax.ShapeDtypeStruct(shape, BF16) for shape in weight_shapes],
        scratch_shapes=[pltpu.VMEM((PROMPT_TILE, IN_WIDTH), F32), pltpu.VMEM((PROMPT_TILE, D_MODEL), BF16)]
        + table_scratch,
        compiler_params=pltpu.CompilerParams(dimension_semantics=("arbitrary",),
                                             vmem_limit_bytes=VMEM_LIMIT_BYTES),
        name="prompt_layer",
    )(x_prompt, p_prompt[0], cos_p, sin_p, w_in[0], w_out[0], w_ple_gate[0], w_ple_proj[0], *small,
      _mix_mask(CHUNK), tabs_p)

    tabs_s, g_seq = _retention_tables(dec_seq)
    cos_s, sin_s = _rotary_tables(PAST_LEN + np.arange(dec_seq))
    tile_seqs = SAMPLE_TILE // dec_seq
    tok_spec = lambda width: pl.BlockSpec((SAMPLE_TILE, width), lambda i: (i, 0))
    state_spec = pl.BlockSpec((tile_seqs, RET_HEADS, RET_DK, RET_DV), lambda i: (i, 0, 0, 0))
    y_sample, st_sample, v_sample = pl.pallas_call(
        functools.partial(_sample_kernel, g_seq),
        grid=(n_tok // SAMPLE_TILE,),
        in_specs=[tok_spec(D_MODEL), tok_spec(PLE_DIM), _const_spec((dec_seq, LANES)),
                  _const_spec((dec_seq, LANES)), state_spec]
        + [_const_spec(shape) for shape in weight_shapes] + small_specs,
        out_specs=[tok_spec(D_MODEL), state_spec, tok_spec(SGU_WIDTH)],
        out_shape=[jax.ShapeDtypeStruct((n_tok, D_MODEL), F32),
                   jax.ShapeDtypeStruct((dec_batch, RET_HEADS, RET_DK, RET_DV), F32),
                   jax.ShapeDtypeStruct((n_tok, SGU_WIDTH), F32)],
        scratch_shapes=[pltpu.VMEM((SAMPLE_TILE, IN_WIDTH), F32), pltpu.VMEM((SAMPLE_TILE, D_MODEL), BF16)]
        + table_scratch,
        compiler_params=pltpu.CompilerParams(dimension_semantics=("arbitrary",),
                                             vmem_limit_bytes=VMEM_LIMIT_BYTES),
        name="sample_layer",
    )(x_sample.reshape(n_tok, D_MODEL), p_sample[0].reshape(n_tok, PLE_DIM), cos_s, sin_s, state_ret[0],
      w_in_b, w_out_b, w_gate_b, w_ple_b, *small, _mix_mask(dec_seq), tabs_s)

    return (y_prompt, y_sample.reshape(dec_batch, dec_seq, D_MODEL), st_prompt[None], st_sample[None],
            v_sample.reshape(1, dec_batch, dec_seq, SGU_WIDTH))
```

```python
import functools

import numpy as np
import jax
import jax.numpy as jnp
from jax import lax
from jax.experimental import pallas as pl
from jax.experimental.pallas import tpu as pltpu

F32 = jnp.float32
BF16 = jnp.bfloat16

D_MODEL = 1024
PAST_LEN = 16384
SGU_WIDTH = 512
SGU_GROUPS = 8
SGU_GROUP_DIM = SGU_WIDTH // SGU_GROUPS
RET_HEADS = 4
RET_DK = 128
RET_DV = 128
CHUNK = 128
ROPE_THETA = 10000.0
PLE_DIM = 256
RMS_EPS = 1e-6
LN_EPS = 1e-5
IN_WIDTH = 3 * SGU_WIDTH + RET_HEADS * (2 * RET_DK + 2 * RET_DV)
O_SU, O_SV, O_SG = 0, SGU_WIDTH, 2 * SGU_WIDTH
O_Q = 3 * SGU_WIDTH
O_K = O_Q + RET_HEADS * RET_DK
O_V = O_K + RET_HEADS * RET_DK
O_RG = O_V + RET_HEADS * RET_DV

LANES = 128
VMEM_LIMIT_BYTES = 60 * 1024 * 1024

PROMPT_TILE = 1024
BLOCK_ROWS = {"prompt": 512, "sample": CHUNK}
PIECE_COLS = 512
CAST_STEPS = 8
SAMPLE_TILE = 2 * CHUNK


def _log_gamma():
    return np.log(1.0 - 2.0 ** (-5.0 - np.arange(RET_HEADS, dtype=np.float64)))


def _retention_tables(seg):
    lg = _log_gamma()[:, None, None]
    r = np.arange(CHUNK)
    i, j = r[:, None] % seg, r[None, :] % seg
    same = (r[:, None] // seg) == (r[None, :] // seg)
    decay = np.where(same & (i >= j), np.exp(lg * np.maximum(i - j, 0)), 0.0)
    wq = np.broadcast_to(np.exp(lg * (i + 1.0)), (RET_HEADS, CHUNK, LANES))
    wkv = np.broadcast_to(np.exp(lg * (seg - 1.0 - i)), (RET_HEADS, CHUNK, LANES))
    scale = RET_DK ** -0.5
    tabs = np.stack([decay * scale, wq, wkv * scale]).astype(np.float32)
    g_seg = [float(v) for v in np.exp(_log_gamma() * seg)]
    return tabs, g_seg


def _rotary_tables(pos):
    half = RET_DK // 2
    inv = ROPE_THETA ** (-np.arange(half, dtype=np.float64) / half)
    ang = pos.astype(np.float64)[:, None] * inv[None, :]
    cos, sin = np.cos(ang), np.sin(ang)
    return (np.concatenate([cos, cos], axis=1).astype(np.float32),
            np.concatenate([-sin, sin], axis=1).astype(np.float32))


def _mix_mask(seg):
    r = np.arange(CHUNK)
    same = (r[:, None] // seg) == (r[None, :] // seg)
    m = same & ((r[None, :] % seg) <= (r[:, None] % seg))
    return np.concatenate([m, m], axis=1).astype(np.float32)


def _gelu(x):
    c = float(np.sqrt(2.0 / np.pi))
    half = 0.5 * x
    return half + half * jnp.tanh(x * (c + (c * 0.044715) * (x * x)))


def _silu(x):
    return x * jax.nn.sigmoid(x)


def _rotate(x, cos, sin_signed):
    return x * cos + pltpu.roll(x, RET_DK // 2, 1) * sin_signed


def _normed_bf16(x, norm_ref):
    ms = jnp.mean(x * x, axis=-1, keepdims=True)
    return (x * lax.rsqrt(ms + RMS_EPS) * norm_ref[...]).astype(BF16)


def _in_pieces(x, norm_pre_ref, w_in_ref, z_ref):
    box = {}

    def piece(c0):
        def run():
            if 'h' not in box:
                box['h'] = _normed_bf16(x, norm_pre_ref)
            z_ref[:, c0:c0 + PIECE_COLS] = jnp.dot(box['h'], w_in_ref[:, c0:c0 + PIECE_COLS],
                                                   preferred_element_type=F32)
        return run
    return [piece(c0) for c0 in range(0, IN_WIDTH, PIECE_COLS)]


def _out_pieces(x, p, mixin_ref, w_out_ref, norm_post_ref, w_gate_ref, w_ple_ref, y_ref):
    box = {}
    halves = [slice(c0, c0 + PIECE_COLS) for c0 in range(0, D_MODEL, PIECE_COLS)]

    def ple():
        box['ple'] = jnp.dot(p.astype(BF16), w_ple_ref[...], preferred_element_type=F32)

    def mix(i):
        def run():
            box['mix%d' % i] = jnp.dot(mixin_ref[...], w_out_ref[:, halves[i]], preferred_element_type=F32)
        return run

    def gate(i):
        def run():
            if 'x1' not in box:
                mixes = [box['mix%d' % j] for j in range(len(halves))]
                ms = sum(jnp.sum(m * m, axis=-1, keepdims=True) for m in mixes) * (1.0 / D_MODEL)
                scale = lax.rsqrt(ms + RMS_EPS)
                box['x1'] = [x[:, halves[j]] + mixes[j] * scale * norm_post_ref[:, halves[j]]
                             for j in range(len(halves))]
                box['x1b'] = jnp.concatenate(box['x1'], axis=1).astype(BF16)
            g = jnp.dot(box['x1b'], w_gate_ref[:, halves[i]], preferred_element_type=F32)
            y_ref[:, halves[i]] = box['x1'][i] + jax.nn.sigmoid(g) * box['ple'][:, halves[i]]
        return run
    return [ple] + [mix(i) for i in range(len(halves))] + [gate(i) for i in range(len(halves))]


def _chunk_stages(z_ref, rows, cos, sin_signed, ln_g_ref, gn_ref, bias_ref, wpair_ref, tabs_ref,
                  state_update, state_output, mixin_ref, vrows_ref):
    box = {}
    zs = lambda off, i, width: z_ref[rows, off + i * width:off + (i + 1) * width]

    def gating_in():
        sv = _gelu(z_ref[rows, O_SV:O_SV + SGU_WIDTH])
        mu = jnp.mean(sv, axis=-1, keepdims=True)
        cen = sv - mu
        var = jnp.mean(cen * cen, axis=-1, keepdims=True)
        vn = cen * lax.rsqrt(var + LN_EPS) * ln_g_ref[...]
        if vrows_ref is not None:
            vrows_ref[rows, :] = vn
        low_lanes = lax.broadcasted_iota(jnp.int32, (CHUNK, LANES), 1) < SGU_GROUP_DIM
        for m in range(SGU_GROUPS // 2):
            blk = vn[:, m * LANES:(m + 1) * LANES]
            rhs = jnp.concatenate([jnp.where(low_lanes, blk, 0.0), jnp.where(low_lanes, 0.0, blk)],
                                  axis=0).astype(BF16)
            box['mixed%d' % m] = jnp.dot(wpair_ref[m], rhs, preferred_element_type=F32)

    def retention_in():
        c, s = cos(), sin_signed()
        for h in range(RET_HEADS):
            q = _rotate(zs(O_Q, h, RET_DK), c, s)
            k = _rotate(zs(O_K, h, RET_DK), c, s)
            v = zs(O_V, h, RET_DV)
            k_b = k.astype(BF16)
            box['scores%d' % h] = lax.dot_general(q.astype(BF16), k_b, (((1,), (1,)), ((), ())),
                                                  preferred_element_type=F32)
            box['qw%d' % h] = (q * tabs_ref[1, h]).astype(BF16)
            box['v%d' % h] = v.astype(BF16)
            box['u%d' % h] = state_update(h, k_b, (v * tabs_ref[2, h]).astype(BF16))

    def gating_out_retention_mid():
        for m in range(SGU_GROUPS // 2):
            cols = slice(m * LANES, (m + 1) * LANES)
            mixed = box['mixed%d' % m] + bias_ref[:, cols]
            su = _gelu(zs(O_SU, m, LANES))
            mixin_ref[rows, cols] = (_silu(zs(O_SG, m, LANES)) * (su * mixed)).astype(BF16)
        for h in range(RET_HEADS):
            scores_b = (box['scores%d' % h] * tabs_ref[0, h]).astype(BF16)
            box['o%d' % h] = state_output(h, scores_b, box['qw%d' % h], box['v%d' % h], box['u%d' % h])

    def retention_out():
        for h in range(RET_HEADS):
            o = box['o%d' % h]
            mu = jnp.mean(o, axis=-1, keepdims=True)
            cen = o - mu
            var = jnp.mean(cen * cen, axis=-1, keepdims=True)
            on = cen * lax.rsqrt(var + LN_EPS) * gn_ref[:, h * RET_DV:(h + 1) * RET_DV]
            mixin_ref[rows, SGU_WIDTH + h * RET_DV:SGU_WIDTH + (h + 1) * RET_DV] = (
                _silu(zs(O_RG, h, RET_DV)) * on).astype(BF16)

    return [gating_in, retention_in, gating_out_retention_mid, retention_out]


def _interleave(pieces, stages):
    pieces, stages = list(pieces), list(stages)
    while pieces or stages:
        if pieces:
            pieces.pop(0)()
        if stages:
            stages.pop(0)()


def _emit_two_blocks(in_pieces, out_pieces, stages, blk_a, blk_b):
    _interleave(in_pieces(blk_a), [])
    _interleave(in_pieces(blk_b), stages(blk_a))
    _interleave(out_pieces(blk_a), stages(blk_b))
    _interleave(out_pieces(blk_b), [])


def _init_tables(seg, sgu_w_ref, sgu_b_ref, mask_ref, wpm_ref, bias_ref):
    rows = lax.broadcasted_iota(jnp.int32, (CHUNK, LANES), 0)
    lanes = lax.broadcasted_iota(jnp.int32, (CHUNK, LANES), 1)
    if seg == CHUNK:
        group = lambda g: sgu_w_ref[g]
        bias_rows = sgu_b_ref[...]
    else:
        reps = CHUNK // seg
        select = jnp.where((rows < seg) & (lanes % seg == rows), 1.0, 0.0).astype(BF16)
        first = lambda a: jnp.where(lanes[:seg] < seg, a, 0.0)

        def group(g):
            stacked = jnp.concatenate([first(sgu_w_ref[g, 0:seg, :])] * reps, axis=0)
            return jnp.dot(stacked.astype(BF16), select, preferred_element_type=F32)
        b_first = jnp.where(lax.broadcasted_iota(jnp.int32, (SGU_GROUPS, LANES), 1) < seg, sgu_b_ref[...], 0.0)
        bias_rows = sum(pltpu.roll(b_first, r * seg, 1) for r in range(reps))
    for m in range(SGU_GROUPS // 2):
        pair = jnp.concatenate([group(2 * m), group(2 * m + 1)], axis=1)
        wpm_ref[m] = jnp.where(mask_ref[...] > 0.0, pair, 0.0).astype(BF16)
    group_of_lane = lax.broadcasted_iota(jnp.int32, (SGU_GROUPS, SGU_WIDTH), 1) // SGU_GROUP_DIM
    expand = jnp.where(group_of_lane == lax.broadcasted_iota(jnp.int32, (SGU_GROUPS, SGU_WIDTH), 0),
                       1.0, 0.0).astype(BF16)
    total, rest = jnp.zeros((CHUNK, SGU_WIDTH), F32), bias_rows
    for _ in range(3):
        term = rest.astype(BF16)
        total = total + lax.dot_general(term, expand, (((0,), (0,)), ((), ())), preferred_element_type=F32)
        rest = rest - term.astype(F32)
    bias_ref[...] = total


def _prompt_kernel(g_chunk, n_tiles, x_ref, p_ref, cos_ref, sin_ref, w_in_ref, w_out_ref, w_gate_ref, w_ple_ref,
                   norm_pre_ref, norm_post_ref, ln_g_ref, gn_ref, sgu_w_ref, sgu_b_ref, mask_ref, tabs_ref,
                   y_ref, state_ref, w_in_b_ref, w_out_b_ref, w_gate_b_ref, w_ple_b_ref,
                   z_ref, mixin_ref, wpm_ref, bias_ref):
    step = pl.program_id(0)

    @pl.when(step == 0)
    def _():
        _init_tables(CHUNK, sgu_w_ref, sgu_b_ref, mask_ref, wpm_ref, bias_ref)

    @pl.when(step < CAST_STEPS)
    def _():
        for src, dst in ((w_in_ref, w_in_b_ref), (w_out_ref, w_out_b_ref), (w_gate_ref, w_gate_b_ref),
                         (w_ple_ref, w_ple_b_ref)):
            n = src.shape[0]
            dst[pl.ds(pl.multiple_of(step * n, n), n), :] = src[...].astype(BF16)

    @pl.when(step >= CAST_STEPS)
    def _():
        @pl.when((step - CAST_STEPS) % n_tiles == 0)
        def _():
            state_ref[...] = jnp.zeros_like(state_ref)

        def state_update(h, k_b, vw_b):
            return lax.dot_general(k_b, vw_b, (((0,), (0,)), ((), ())), preferred_element_type=F32)

        def state_output(h, scores_b, qw_b, v_b, u):
            s = state_ref[0, h]
            o = jnp.dot(jnp.concatenate([scores_b, qw_b], axis=1),
                        jnp.concatenate([v_b, s.astype(BF16)], axis=0), preferred_element_type=F32)
            state_ref[0, h] = g_chunk[h] * s + u
            return o

        def in_pieces(blk):
            return _in_pieces(x_ref[0, blk, :], norm_pre_ref, w_in_b_ref, z_ref.at[blk, :])

        def out_pieces(blk):
            return _out_pieces(x_ref[0, blk, :], p_ref[0, blk, :], mixin_ref.at[blk, :], w_out_b_ref,
                               norm_post_ref, w_gate_b_ref, w_ple_b_ref, y_ref.at[0, blk, :])

        def stages(blk):
            out = []
            for r in range(blk.start, blk.stop, CHUNK):
                rows = slice(r, r + CHUNK)
                out += _chunk_stages(z_ref, rows, lambda rows=rows: cos_ref[rows, :],
                                     lambda rows=rows: sin_ref[rows, :], ln_g_ref, gn_ref, bias_ref, wpm_ref,
                                     tabs_ref, state_update, state_output, mixin_ref, None)
            return out

        rows = BLOCK_ROWS["prompt"]
        _emit_two_blocks(in_pieces, out_pieces, stages, slice(0, rows), slice(rows, 2 * rows))


def _sample_kernel(g_seq, x_ref, p_ref, cos_ref, sin_ref, st_in_ref, w_in_ref, w_out_ref, w_gate_ref,
                   w_ple_ref, norm_pre_ref, norm_post_ref, ln_g_ref, gn_ref, sgu_w_ref, sgu_b_ref, mask_ref,
                   tabs_ref, y_ref, st_out_ref, vrows_ref, z_ref, mixin_ref, wpm_ref, bias_ref):
    seq_len = cos_ref.shape[0]
    seqs = CHUNK // seq_len

    @pl.when(pl.program_id(0) == 0)
    def _():
        _init_tables(seq_len, sgu_w_ref, sgu_b_ref, mask_ref, wpm_ref, bias_ref)

    per_seq = lambda a: a.astype(F32).reshape(seqs, seq_len, a.shape[-1]).astype(BF16)
    tile_rows = lambda ref: jnp.concatenate([ref[...]] * seqs, axis=0)

    def stages(blk):
        sq = slice(blk.start // seq_len, blk.stop // seq_len)

        def state_update(h, k_b, vw_b):
            u = jnp.einsum('sjd,sje->sde', per_seq(k_b), per_seq(vw_b), preferred_element_type=F32)
            st_out_ref[sq, h] = g_seq[h] * st_in_ref[sq, h] + u
            return None

        def state_output(h, scores_b, qw_b, v_b, _):
            o = jnp.dot(scores_b, v_b, preferred_element_type=F32)
            o_inter = jnp.einsum('sid,sde->sie', per_seq(qw_b), st_in_ref[sq, h].astype(BF16),
                                 preferred_element_type=F32)
            return o + o_inter.reshape(CHUNK, RET_DV)

        return _chunk_stages(z_ref, blk, lambda: tile_rows(cos_ref), lambda: tile_rows(sin_ref), ln_g_ref,
                             gn_ref, bias_ref, wpm_ref, tabs_ref, state_update, state_output, mixin_ref,
                             vrows_ref)

    def in_pieces(blk):
        return _in_pieces(x_ref[blk, :], norm_pre_ref, w_in_ref, z_ref.at[blk, :])

    def out_pieces(blk):
        return _out_pieces(x_ref[blk, :], p_ref[blk, :], mixin_ref.at[blk, :], w_out_ref, norm_post_ref,
                           w_gate_ref, w_ple_ref, y_ref.at[blk, :])

    rows = BLOCK_ROWS["sample"]
    _emit_two_blocks(in_pieces, out_pieces, stages, slice(0, rows), slice(rows, 2 * rows))


def _const_spec(shape):
    return pl.BlockSpec(shape, lambda *_: (0,) * len(shape), pipeline_mode=pl.Buffered(1))


def kernel(x_prompt, x_sample, state_ret, p_prompt, p_sample, w_in, w_out, norm_pre, norm_post, sgu_w, sgu_b,
           sgu_ln, ret_gn, w_ple_proj, w_ple_gate):
    batch, seq, _ = x_prompt.shape
    dec_batch, dec_seq, _ = x_sample.shape
    n_tiles = seq // PROMPT_TILE
    n_tok = dec_batch * dec_seq
    assert seq % PROMPT_TILE == 0 and PROMPT_TILE == 2 * BLOCK_ROWS["prompt"] and w_in.shape[0] == 1
    assert CHUNK % dec_seq == 0 and n_tok % SAMPLE_TILE == 0 and SAMPLE_TILE == 2 * BLOCK_ROWS["sample"]
    assert D_MODEL % CAST_STEPS == 0 and PLE_DIM % (16 * CAST_STEPS) == 0

    small = (norm_pre[0][None, :], norm_post[0][None, :], sgu_ln[0][None, :], ret_gn[0][None, :],
             sgu_w[0], sgu_b[0])
    small_specs = [_const_spec((1, D_MODEL)), _const_spec((1, D_MODEL)), _const_spec((1, SGU_WIDTH)),
                   _const_spec((1, RET_HEADS * RET_DV)), _const_spec((SGU_GROUPS, CHUNK, CHUNK)),
                   _const_spec((SGU_GROUPS, CHUNK)), _const_spec((CHUNK, 2 * CHUNK)),
                   _const_spec((3, RET_HEADS, CHUNK, LANES))]
    table_scratch = [pltpu.VMEM((SGU_GROUPS // 2, CHUNK, 2 * CHUNK), BF16), pltpu.VMEM((CHUNK, SGU_WIDTH), F32)]
    weight_shapes = [(D_MODEL, IN_WIDTH), (D_MODEL, D_MODEL), (D_MODEL, D_MODEL), (PLE_DIM, D_MODEL)]

    tabs_p, g_chunk = _retention_tables(CHUNK)
    cos_p, sin_p = _rotary_tables(np.arange(seq))
    tile_of = lambda s: jnp.maximum(s - CAST_STEPS, 0)
    tile_spec = lambda width: pl.BlockSpec((1, PROMPT_TILE, width),
                                           lambda s: (tile_of(s) // n_tiles, tile_of(s) % n_tiles, 0))
    pos_spec = pl.BlockSpec((PROMPT_TILE, LANES), lambda s: (tile_of(s) % n_tiles, 0))
    cast_spec = lambda shape: pl.BlockSpec((shape[0] // CAST_STEPS, shape[1]),
                                           lambda s: (jnp.minimum(s, CAST_STEPS - 1), 0))
    y_prompt, st_prompt, w_in_b, w_out_b, w_gate_b, w_ple_b = pl.pallas_call(
        functools.partial(_prompt_kernel, g_chunk, n_tiles),
        grid=(CAST_STEPS + batch * n_tiles,),
        in_specs=[tile_spec(D_MODEL), tile_spec(PLE_DIM), pos_spec, pos_spec]
        + [cast_spec(shape) for shape in weight_shapes] + small_specs,
        out_specs=[tile_spec(D_MODEL),
                   pl.BlockSpec((1, RET_HEADS, RET_DK, RET_DV), lambda s: (tile_of(s) // n_tiles, 0, 0, 0))]
        + [pl.BlockSpec(shape, lambda s: (0, 0)) for shape in weight_shapes],
        out_shape=[jax.ShapeDtypeStruct((batch, seq, D_MODEL), F32),
                   jax.ShapeDtypeStruct((batch, RET_HEADS, RET_DK, RET_DV), F32)]
        + [jax.ShapeDtypeStruct(shape, BF16) for shape in weight_shapes],
        scratch_shapes=[pltpu.VMEM((PROMPT_TILE, IN_WIDTH), F32), pltpu.VMEM((PROMPT_TILE, D_MODEL), BF16)]
        + table_scratch,
        compiler_params=pltpu.CompilerParams(dimension_semantics=("arbitrary",),
                                             vmem_limit_bytes=VMEM_LIMIT_BYTES),
        name="prompt_layer",
    )(x_prompt, p_prompt[0], cos_p, sin_p, w_in[0], w_out[0], w_ple_gate[0], w_ple_proj[0], *small,
      _mix_mask(CHUNK), tabs_p)

    tabs_s, g_seq = _retention_tables(dec_seq)
    cos_s, sin_s = _rotary_tables(PAST_LEN + np.arange(dec_seq))
    tile_seqs = SAMPLE_TILE // dec_seq
    tok_spec = lambda width: pl.BlockSpec((SAMPLE_TILE, width), lambda i: (i, 0))
    state_spec = pl.BlockSpec((tile_seqs, RET_HEADS, RET_DK, RET_DV), lambda i: (i, 0, 0, 0))
    y_sample, st_sample, v_sample = pl.pallas_call(
        functools.partial(_sample_kernel, g_seq),
        grid=(n_tok // SAMPLE_TILE,),
        in_specs=[tok_spec(D_MODEL), tok_spec(PLE_DIM), _const_spec((dec_seq, LANES)),
                  _const_spec((dec_seq, LANES)), state_spec]
        + [_const_spec(shape) for shape in weight_shapes] + small_specs,
        out_specs=[tok_spec(D_MODEL), state_spec, tok_spec(SGU_WIDTH)],
        out_shape=[jax.ShapeDtypeStruct((n_tok, D_MODEL), F32),
                   jax.ShapeDtypeStruct((dec_batch, RET_HEADS, RET_DK, RET_DV), F32),
                   jax.ShapeDtypeStruct((n_tok, SGU_WIDTH), F32)],
        scratch_shapes=[pltpu.VMEM((SAMPLE_TILE, IN_WIDTH), F32), pltpu.VMEM((SAMPLE_TILE, D_MODEL), BF16)]
        + table_scratch,
        compiler_params=pltpu.CompilerParams(dimension_semantics=("arbitrary",),
                                             vmem_limit_bytes=VMEM_LIMIT_BYTES),
        name="sample_layer",
    )(x_sample.reshape(n_tok, D_MODEL), p_sample[0].reshape(n_tok, PLE_DIM), cos_s, sin_s, state_ret[0],
      w_in_b, w_out_b, w_gate_b, w_ple_b, *small, _mix_mask(dec_seq), tabs_s)

    return (y_prompt, y_sample.reshape(dec_batch, dec_seq, D_MODEL), st_prompt[None], st_sample[None],
            v_sample.reshape(1, dec_batch, dec_seq, SGU_WIDTH))
```

```python
import functools

import numpy as np
import jax
import jax.numpy as jnp
from jax import lax
from jax.experimental import pallas as pl
from jax.experimental.pallas import tpu as pltpu

F32 = jnp.float32
BF16 = jnp.bfloat16

D_MODEL = 1024
PAST_LEN = 16384
SGU_WIDTH = 512
SGU_GROUPS = 8
SGU_GROUP_DIM = SGU_WIDTH // SGU_GROUPS
RET_HEADS = 4
RET_DK = 128
RET_DV = 128
CHUNK = 128
ROPE_THETA = 10000.0
PLE_DIM = 256
RMS_EPS = 1e-6
LN_EPS = 1e-5
IN_WIDTH = 3 * SGU_WIDTH + RET_HEADS * (2 * RET_DK + 2 * RET_DV)
O_SU, O_SV, O_SG = 0, SGU_WIDTH, 2 * SGU_WIDTH
O_Q = 3 * SGU_WIDTH
O_K = O_Q + RET_HEADS * RET_DK
O_V = O_K + RET_HEADS * RET_DK
O_RG = O_V + RET_HEADS * RET_DV

LANES = 128
VMEM_LIMIT_BYTES = 60 * 1024 * 1024

PROMPT_TILE = 1024
BLOCK_ROWS = {"prompt": 256, "sample": CHUNK}
PIECE_COLS = 512
CAST_STEPS = 8
SAMPLE_TILE = 2 * CHUNK


def _log_gamma():
    return np.log(1.0 - 2.0 ** (-5.0 - np.arange(RET_HEADS, dtype=np.float64)))


def _retention_tables(seg):
    lg = _log_gamma()[:, None, None]
    r = np.arange(CHUNK)
    i, j = r[:, None] % seg, r[None, :] % seg
    same = (r[:, None] // seg) == (r[None, :] // seg)
    decay = np.where(same & (i >= j), np.exp(lg * np.maximum(i - j, 0)), 0.0)
    wq = np.broadcast_to(np.exp(lg * (i + 1.0)), (RET_HEADS, CHUNK, LANES))
    wkv = np.broadcast_to(np.exp(lg * (seg - 1.0 - i)), (RET_HEADS, CHUNK, LANES))
    scale = RET_DK ** -0.5
    tabs = np.stack([decay * scale, wq, wkv * scale]).astype(np.float32)
    g_seg = [float(v) for v in np.exp(_log_gamma() * seg)]
    return tabs, g_seg


def _rotary_tables(pos):
    half = RET_DK // 2
    inv = ROPE_THETA ** (-np.arange(half, dtype=np.float64) / half)
    ang = pos.astype(np.float64)[:, None] * inv[None, :]
    cos, sin = np.cos(ang), np.sin(ang)
    return (np.concatenate([cos, cos], axis=1).astype(np.float32),
            np.concatenate([-sin, sin], axis=1).astype(np.float32))


def _mix_mask(seg):
    r = np.arange(CHUNK)
    same = (r[:, None] // seg) == (r[None, :] // seg)
    m = same & ((r[None, :] % seg) <= (r[:, None] % seg))
    return np.concatenate([m, m], axis=1).astype(np.float32)


def _gelu(x):
    c = float(np.sqrt(2.0 / np.pi))
    half = 0.5 * x
    return half + half * jnp.tanh(x * (c + (c * 0.044715) * (x * x)))


def _silu(x):
    return x * jax.nn.sigmoid(x)


def _rotate(x, cos, sin_signed):
    return x * cos + pltpu.roll(x, RET_DK // 2, 1) * sin_signed


def _normed_bf16(x, norm_ref):
    ms = jnp.mean(x * x, axis=-1, keepdims=True)
    return (x * lax.rsqrt(ms + RMS_EPS) * norm_ref[...]).astype(BF16)


def _in_pieces(x, norm_pre_ref, w_in_ref, z_ref):
    box = {}

    def piece(c0):
        def run():
            if 'h' not in box:
                box['h'] = _normed_bf16(x, norm_pre_ref)
            z_ref[:, c0:c0 + PIECE_COLS] = jnp.dot(box['h'], w_in_ref[:, c0:c0 + PIECE_COLS],
                                                   preferred_element_type=F32)
        return run
    return [piece(c0) for c0 in range(0, IN_WIDTH, PIECE_COLS)]


def _out_pieces(x, p, mixin_ref, w_out_ref, norm_post_ref, w_gate_ref, w_ple_ref, y_ref):
    box = {}
    halves = [slice(c0, c0 + PIECE_COLS) for c0 in range(0, D_MODEL, PIECE_COLS)]

    def ple():
        box['ple'] = jnp.dot(p.astype(BF16), w_ple_ref[...], preferred_element_type=F32)

    def mix(i):
        def run():
            box['mix%d' % i] = jnp.dot(mixin_ref[...], w_out_ref[:, halves[i]], preferred_element_type=F32)
        return run

    def gate(i):
        def run():
            if 'x1' not in box:
                mixes = [box['mix%d' % j] for j in range(len(halves))]
                ms = sum(jnp.sum(m * m, axis=-1, keepdims=True) for m in mixes) * (1.0 / D_MODEL)
                scale = lax.rsqrt(ms + RMS_EPS)
                box['x1'] = [x[:, halves[j]] + mixes[j] * scale * norm_post_ref[:, halves[j]]
                             for j in range(len(halves))]
                box['x1b'] = jnp.concatenate(box['x1'], axis=1).astype(BF16)
            g = jnp.dot(box['x1b'], w_gate_ref[:, halves[i]], preferred_element_type=F32)
            y_ref[:, halves[i]] = box['x1'][i] + jax.nn.sigmoid(g) * box['ple'][:, halves[i]]
        return run
    return [ple] + [mix(i) for i in range(len(halves))] + [gate(i) for i in range(len(halves))]


def _chunk_stages(z_ref, rows, cos, sin_signed, ln_g_ref, gn_ref, bias_ref, wpair_ref, tabs_ref,
                  state_update, state_output, mixin_ref, vrows_ref):
    box = {}
    zs = lambda off, i, width: z_ref[rows, off + i * width:off + (i + 1) * width]

    def gating_in():
        sv = _gelu(z_ref[rows, O_SV:O_SV + SGU_WIDTH])
        mu = jnp.mean(sv, axis=-1, keepdims=True)
        cen = sv - mu
        var = jnp.mean(cen * cen, axis=-1, keepdims=True)
        vn = cen * lax.rsqrt(var + LN_EPS) * ln_g_ref[...]
        if vrows_ref is not None:
            vrows_ref[rows, :] = vn
        low_lanes = lax.broadcasted_iota(jnp.int32, (CHUNK, LANES), 1) < SGU_GROUP_DIM
        for m in range(SGU_GROUPS // 2):
            blk = vn[:, m * LANES:(m + 1) * LANES]
            rhs = jnp.concatenate([jnp.where(low_lanes, blk, 0.0), jnp.where(low_lanes, 0.0, blk)],
                                  axis=0).astype(BF16)
            box['mixed%d' % m] = jnp.dot(wpair_ref[m], rhs, preferred_element_type=F32)

    def retention_in():
        c, s = cos(), sin_signed()
        for h in range(RET_HEADS):
            q = _rotate(zs(O_Q, h, RET_DK), c, s)
            k = _rotate(zs(O_K, h, RET_DK), c, s)
            v = zs(O_V, h, RET_DV)
            k_b = k.astype(BF16)
            box['scores%d' % h] = lax.dot_general(q.astype(BF16), k_b, (((1,), (1,)), ((), ())),
                                                  preferred_element_type=F32)
            box['qw%d' % h] = (q * tabs_ref[1, h]).astype(BF16)
            box['v%d' % h] = v.astype(BF16)
            box['u%d' % h] = state_update(h, k_b, (v * tabs_ref[2, h]).astype(BF16))

    def gating_out_retention_mid():
        for m in range(SGU_GROUPS // 2):
            cols = slice(m * LANES, (m + 1) * LANES)
            mixed = box['mixed%d' % m] + bias_ref[:, cols]
            su = _gelu(zs(O_SU, m, LANES))
            mixin_ref[rows, cols] = (_silu(zs(O_SG, m, LANES)) * (su * mixed)).astype(BF16)
        for h in range(RET_HEADS):
            scores_b = (box['scores%d' % h] * tabs_ref[0, h]).astype(BF16)
            box['o%d' % h] = state_output(h, scores_b, box['qw%d' % h], box['v%d' % h], box['u%d' % h])

    def retention_out():
        for h in range(RET_HEADS):
            o = box['o%d' % h]
            mu = jnp.mean(o, axis=-1, keepdims=True)
            cen = o - mu
            var = jnp.mean(cen * cen, axis=-1, keepdims=True)
            on = cen * lax.rsqrt(var + LN_EPS) * gn_ref[:, h * RET_DV:(h + 1) * RET_DV]
            mixin_ref[rows, SGU_WIDTH + h * RET_DV:SGU_WIDTH + (h + 1) * RET_DV] = (
                _silu(zs(O_RG, h, RET_DV)) * on).astype(BF16)

    return [gating_in, retention_in, gating_out_retention_mid, retention_out]


def _interleave(pieces, stages):
    slots = max(len(stages), 1)
    done = 0
    for i in range(slots):
        upto = -(-(i + 1) * len(pieces) // slots)
        for piece in pieces[done:upto]:
            piece()
        done = upto
        if i < len(stages):
            stages[i]()


def _emit_blocks(in_pieces, out_pieces, stages, blocks):
    _interleave(in_pieces(blocks[0]), [])
    for i, blk in enumerate(blocks):
        nxt = in_pieces(blocks[i + 1]) if i + 1 < len(blocks) else []
        if i > 0:
            ple, mix0, mix1, gate0, gate1 = out_pieces(blocks[i - 1])
            pieces = [ple, mix0, mix1] + nxt[:3] + [gate0] + nxt[3:4] + [gate1] + nxt[4:]
        else:
            pieces = nxt
        _interleave(pieces, stages(blk))
    _interleave(out_pieces(blocks[-1]), [])


def _init_tables(seg, sgu_w_ref, sgu_b_ref, mask_ref, wpm_ref, bias_ref):
    rows = lax.broadcasted_iota(jnp.int32, (CHUNK, LANES), 0)
    lanes = lax.broadcasted_iota(jnp.int32, (CHUNK, LANES), 1)
    if seg == CHUNK:
        group = lambda g: sgu_w_ref[g]
        bias_rows = sgu_b_ref[...]
    else:
        reps = CHUNK // seg
        select = jnp.where((rows < seg) & (lanes % seg == rows), 1.0, 0.0).astype(BF16)
        first = lambda a: jnp.where(lanes[:seg] < seg, a, 0.0)

        def group(g):
            stacked = jnp.concatenate([first(sgu_w_ref[g, 0:seg, :])] * reps, axis=0)
            return jnp.dot(stacked.astype(BF16), select, preferred_element_type=F32)
        b_first = jnp.where(lax.broadcasted_iota(jnp.int32, (SGU_GROUPS, LANES), 1) < seg, sgu_b_ref[...], 0.0)
        bias_rows = sum(pltpu.roll(b_first, r * seg, 1) for r in range(reps))
    for m in range(SGU_GROUPS // 2):
        pair = jnp.concatenate([group(2 * m), group(2 * m + 1)], axis=1)
        wpm_ref[m] = jnp.where(mask_ref[...] > 0.0, pair, 0.0).astype(BF16)
    group_of_lane = lax.broadcasted_iota(jnp.int32, (SGU_GROUPS, SGU_WIDTH), 1) // SGU_GROUP_DIM
    expand = jnp.where(group_of_lane == lax.broadcasted_iota(jnp.int32, (SGU_GROUPS, SGU_WIDTH), 0),
                       1.0, 0.0).astype(BF16)
    total, rest = jnp.zeros((CHUNK, SGU_WIDTH), F32), bias_rows
    for _ in range(3):
        term = rest.astype(BF16)
        total = total + lax.dot_general(term, expand, (((0,), (0,)), ((), ())), preferred_element_type=F32)
        rest = rest - term.astype(F32)
    bias_ref[...] = total


def _prompt_kernel(g_chunk, n_tiles, x_ref, p_ref, cos_ref, sin_ref, w_in_ref, w_out_ref, w_gate_ref, w_ple_ref,
                   norm_pre_ref, norm_post_ref, ln_g_ref, gn_ref, sgu_w_ref, sgu_b_ref, mask_ref, tabs_ref,
                   y_ref, state_ref, w_in_b_ref, w_out_b_ref, w_gate_b_ref, w_ple_b_ref,
                   z_ref, mixin_ref, wpm_ref, bias_ref):
    step = pl.program_id(0)

    @pl.when(step == 0)
    def _():
        _init_tables(CHUNK, sgu_w_ref, sgu_b_ref, mask_ref, wpm_ref, bias_ref)

    @pl.when(step < CAST_STEPS)
    def _():
        for src, dst in ((w_in_ref, w_in_b_ref), (w_out_ref, w_out_b_ref), (w_gate_ref, w_gate_b_ref),
                         (w_ple_ref, w_ple_b_ref)):
            n = src.shape[0]
            dst[pl.ds(pl.multiple_of(step * n, n), n), :] = src[...].astype(BF16)

    @pl.when(step >= CAST_STEPS)
    def _():
        @pl.when((step - CAST_STEPS) % n_tiles == 0)
        def _():
            state_ref[...] = jnp.zeros_like(state_ref)

        def state_update(h, k_b, vw_b):
            return lax.dot_general(k_b, vw_b, (((0,), (0,)), ((), ())), preferred_element_type=F32)

        def state_output(h, scores_b, qw_b, v_b, u):
            s = state_ref[0, h]
            o = jnp.dot(jnp.concatenate([scores_b, qw_b], axis=1),
                        jnp.concatenate([v_b, s.astype(BF16)], axis=0), preferred_element_type=F32)
            state_ref[0, h] = g_chunk[h] * s + u
            return o

        def in_pieces(blk):
            return _in_pieces(x_ref[0, blk, :], norm_pre_ref, w_in_b_ref, z_ref.at[blk, :])

        def out_pieces(blk):
            return _out_pieces(x_ref[0, blk, :], p_ref[0, blk, :], mixin_ref.at[blk, :], w_out_b_ref,
                               norm_post_ref, w_gate_b_ref, w_ple_b_ref, y_ref.at[0, blk, :])

        def stages(blk):
            out = []
            for r in range(blk.start, blk.stop, CHUNK):
                rows = slice(r, r + CHUNK)
                out += _chunk_stages(z_ref, rows, lambda rows=rows: cos_ref[rows, :],
                                     lambda rows=rows: sin_ref[rows, :], ln_g_ref, gn_ref, bias_ref, wpm_ref,
                                     tabs_ref, state_update, state_output, mixin_ref, None)
            return out

        rows = BLOCK_ROWS["prompt"]
        _emit_blocks(in_pieces, out_pieces, stages, [slice(r, r + rows) for r in range(0, PROMPT_TILE, rows)])


def _sample_kernel(g_seq, x_ref, p_ref, cos_ref, sin_ref, st_in_ref, w_in_ref, w_out_ref, w_gate_ref,
                   w_ple_ref, norm_pre_ref, norm_post_ref, ln_g_ref, gn_ref, sgu_w_ref, sgu_b_ref, mask_ref,
                   tabs_ref, y_ref, st_out_ref, vrows_ref, z_ref, mixin_ref, wpm_ref, bias_ref):
    seq_len = cos_ref.shape[0]
    seqs = CHUNK // seq_len

    @pl.when(pl.program_id(0) == 0)
    def _():
        _init_tables(seq_len, sgu_w_ref, sgu_b_ref, mask_ref, wpm_ref, bias_ref)

    per_seq = lambda a: a.astype(F32).reshape(seqs, seq_len, a.shape[-1]).astype(BF16)
    tile_rows = lambda ref: jnp.concatenate([ref[...]] * seqs, axis=0)

    def stages(blk):
        sq = slice(blk.start // seq_len, blk.stop // seq_len)

        def state_update(h, k_b, vw_b):
            u = jnp.einsum('sjd,sje->sde', per_seq(k_b), per_seq(vw_b), preferred_element_type=F32)
            st_out_ref[sq, h] = g_seq[h] * st_in_ref[sq, h] + u
            return None

        def state_output(h, scores_b, qw_b, v_b, _):
            o = jnp.dot(scores_b, v_b, preferred_element_type=F32)
            o_inter = jnp.einsum('sid,sde->sie', per_seq(qw_b), st_in_ref[sq, h].astype(BF16),
                                 preferred_element_type=F32)
            return o + o_inter.reshape(CHUNK, RET_DV)

        return _chunk_stages(z_ref, blk, lambda: tile_rows(cos_ref), lambda: tile_rows(sin_ref), ln_g_ref,
                             gn_ref, bias_ref, wpm_ref, tabs_ref, state_update, state_output, mixin_ref,
                             vrows_ref)

    def in_pieces(blk):
        return _in_pieces(x_ref[blk, :], norm_pre_ref, w_in_ref, z_ref.at[blk, :])

    def out_pieces(blk):
        return _out_pieces(x_ref[blk, :], p_ref[blk, :], mixin_ref.at[blk, :], w_out_ref, norm_post_ref,
                           w_gate_ref, w_ple_ref, y_ref.at[blk, :])

    rows = BLOCK_ROWS["sample"]
    _emit_blocks(in_pieces, out_pieces, stages, [slice(r, r + rows) for r in range(0, SAMPLE_TILE, rows)])


def _const_spec(shape):
    return pl.BlockSpec(shape, lambda *_: (0,) * len(shape), pipeline_mode=pl.Buffered(1))


def kernel(x_prompt, x_sample, state_ret, p_prompt, p_sample, w_in, w_out, norm_pre, norm_post, sgu_w, sgu_b,
           sgu_ln, ret_gn, w_ple_proj, w_ple_gate):
    batch, seq, _ = x_prompt.shape
    dec_batch, dec_seq, _ = x_sample.shape
    n_tiles = seq // PROMPT_TILE
    n_tok = dec_batch * dec_seq
    assert seq % PROMPT_TILE == 0 and PROMPT_TILE % BLOCK_ROWS["prompt"] == 0 and w_in.shape[0] == 1
    assert CHUNK % dec_seq == 0 and n_tok % SAMPLE_TILE == 0 and SAMPLE_TILE % BLOCK_ROWS["sample"] == 0
    assert D_MODEL % CAST_STEPS == 0 and PLE_DIM % (16 * CAST_STEPS) == 0

    small = (norm_pre[0][None, :], norm_post[0][None, :], sgu_ln[0][None, :], ret_gn[0][None, :],
             sgu_w[0], sgu_b[0])
    small_specs = [_const_spec((1, D_MODEL)), _const_spec((1, D_MODEL)), _const_spec((1, SGU_WIDTH)),
                   _const_spec((1, RET_HEADS * RET_DV)), _const_spec((SGU_GROUPS, CHUNK, CHUNK)),
                   _const_spec((SGU_GROUPS, CHUNK)), _const_spec((CHUNK, 2 * CHUNK)),
                   _const_spec((3, RET_HEADS, CHUNK, LANES))]
    table_scratch = [pltpu.VMEM((SGU_GROUPS // 2, CHUNK, 2 * CHUNK), BF16), pltpu.VMEM((CHUNK, SGU_WIDTH), F32)]
    weight_shapes = [(D_MODEL, IN_WIDTH), (D_MODEL, D_MODEL), (D_MODEL, D_MODEL), (PLE_DIM, D_MODEL)]

    tabs_p, g_chunk = _retention_tables(CHUNK)
    cos_p, sin_p = _rotary_tables(np.arange(seq))
    tile_of = lambda s: jnp.maximum(s - CAST_STEPS, 0)
    tile_spec = lambda width: pl.BlockSpec((1, PROMPT_TILE, width),
                                           lambda s: (tile_of(s) // n_tiles, tile_of(s) % n_tiles, 0))
    pos_spec = pl.BlockSpec((PROMPT_TILE, LANES), lambda s: (tile_of(s) % n_tiles, 0))
    cast_spec = lambda shape: pl.BlockSpec((shape[0] // CAST_STEPS, shape[1]),
                                           lambda s: (jnp.minimum(s, CAST_STEPS - 1), 0))
    y_prompt, st_prompt, w_in_b, w_out_b, w_gate_b, w_ple_b = pl.pallas_call(
        functools.partial(_prompt_kernel, g_chunk, n_tiles),
        grid=(CAST_STEPS + batch * n_tiles,),
        in_specs=[tile_spec(D_MODEL), tile_spec(PLE_DIM), pos_spec, pos_spec]
        + [cast_spec(shape) for shape in weight_shapes] + small_specs,
        out_specs=[tile_spec(D_MODEL),
                   pl.BlockSpec((1, RET_HEADS, RET_DK, RET_DV), lambda s: (tile_of(s) // n_tiles, 0, 0, 0))]
        + [pl.BlockSpec(shape, lambda s: (0, 0)) for shape in weight_shapes],
        out_shape=[jax.ShapeDtypeStruct((batch, seq, D_MODEL), F32),
                   jax.ShapeDtypeStruct((batch, RET_HEADS, RET_DK, RET_DV), F32)]
        + [jax.ShapeDtypeStruct(shape, BF16) for shape in weight_shapes],
        scratch_shapes=[pltpu.VMEM((PROMPT_TILE, IN_WIDTH), F32), pltpu.VMEM((PROMPT_TILE, D_MODEL), BF16)]
        + table_scratch,
        compiler_params=pltpu.CompilerParams(dimension_semantics=("arbitrary",),
                                             vmem_limit_bytes=VMEM_LIMIT_BYTES),
        name="prompt_layer",
    )(x_prompt, p_prompt[0], cos_p, sin_p, w_in[0], w_out[0], w_ple_gate[0], w_ple_proj[0], *small,
      _mix_mask(CHUNK), tabs_p)

    tabs_s, g_seq = _retention_tables(dec_seq)
    cos_s, sin_s = _rotary_tables(PAST_LEN + np.arange(dec_seq))
    tile_seqs = SAMPLE_TILE // dec_seq
    tok_spec = lambda width: pl.BlockSpec((SAMPLE_TILE, width), lambda i: (i, 0))
    state_spec = pl.BlockSpec((tile_seqs, RET_HEADS, RET_DK, RET_DV), lambda i: (i, 0, 0, 0))
    y_sample, st_sample, v_sample = pl.pallas_call(
        functools.partial(_sample_kernel, g_seq),
        grid=(n_tok // SAMPLE_TILE,),
        in_specs=[tok_spec(D_MODEL), tok_spec(PLE_DIM), _const_spec((dec_seq, LANES)),
                  _const_spec((dec_seq, LANES)), state_spec]
        + [_const_spec(shape) for shape in weight_shapes] + small_specs,
        out_specs=[tok_spec(D_MODEL), state_spec, tok_spec(SGU_WIDTH)],
        out_shape=[jax.ShapeDtypeStruct((n_tok, D_MODEL), F32),
                   jax.ShapeDtypeStruct((dec_batch, RET_HEADS, RET_DK, RET_DV), F32),
                   jax.ShapeDtypeStruct((n_tok, SGU_WIDTH), F32)],
        scratch_shapes=[pltpu.VMEM((SAMPLE_TILE, IN_WIDTH), F32), pltpu.VMEM((SAMPLE_TILE, D_MODEL), BF16)]
        + table_scratch,
        compiler_params=pltpu.CompilerParams(dimension_semantics=("arbitrary",),
                                             vmem_limit_bytes=VMEM_LIMIT_BYTES),
        name="sample_layer",
    )(x_sample.reshape(n_tok, D_MODEL), p_sample[0].reshape(n_tok, PLE_DIM), cos_s, sin_s, state_ret[0],
      w_in_b, w_out_b, w_gate_b, w_ple_b, *small, _mix_mask(dec_seq), tabs_s)

    return (y_prompt, y_sample.reshape(dec_batch, dec_seq, D_MODEL), st_prompt[None], st_sample[None],
            v_sample.reshape(1, dec_batch, dec_seq, SGU_WIDTH))
```

```python
import functools

import numpy as np
import jax
import jax.numpy as jnp
from jax import lax
from jax.experimental import pallas as pl
from jax.experimental.pallas import tpu as pltpu

F32 = jnp.float32
BF16 = jnp.bfloat16

D_MODEL = 1024
PAST_LEN = 16384
SGU_WIDTH = 512
SGU_GROUPS = 8
SGU_GROUP_DIM = SGU_WIDTH // SGU_GROUPS
RET_HEADS = 4
RET_DK = 128
RET_DV = 128
CHUNK = 128
ROPE_THETA = 10000.0
PLE_DIM = 256
RMS_EPS = 1e-6
LN_EPS = 1e-5
IN_WIDTH = 3 * SGU_WIDTH + RET_HEADS * (2 * RET_DK + 2 * RET_DV)
O_SU, O_SV, O_SG = 0, SGU_WIDTH, 2 * SGU_WIDTH
O_Q = 3 * SGU_WIDTH
O_K = O_Q + RET_HEADS * RET_DK
O_V = O_K + RET_HEADS * RET_DK
O_RG = O_V + RET_HEADS * RET_DV

LANES = 128
VMEM_LIMIT_BYTES = 60 * 1024 * 1024

PROMPT_TILE = 1024
BLOCK_ROWS = {"prompt": 256, "sample": CHUNK}
PIECE_COLS = 512
CAST_STEPS = 8
SAMPLE_TILE = 2 * CHUNK


def _log_gamma():
    return np.log(1.0 - 2.0 ** (-5.0 - np.arange(RET_HEADS, dtype=np.float64)))


def _retention_tables(seg):
    lg = _log_gamma()[:, None, None]
    r = np.arange(CHUNK)
    i, j = r[:, None] % seg, r[None, :] % seg
    same = (r[:, None] // seg) == (r[None, :] // seg)
    decay = np.where(same & (i >= j), np.exp(lg * np.maximum(i - j, 0)), 0.0)
    wq = np.broadcast_to(np.exp(lg * (i + 1.0)), (RET_HEADS, CHUNK, LANES))
    wkv = np.broadcast_to(np.exp(lg * (seg - 1.0 - i)), (RET_HEADS, CHUNK, LANES))
    scale = RET_DK ** -0.5
    tabs = np.stack([decay * scale, wq, wkv * scale]).astype(np.float32)
    g_seg = [float(v) for v in np.exp(_log_gamma() * seg)]
    return tabs, g_seg


def _rotary_tables(pos):
    half = RET_DK // 2
    inv = ROPE_THETA ** (-np.arange(half, dtype=np.float64) / half)
    ang = pos.astype(np.float64)[:, None] * inv[None, :]
    cos, sin = np.cos(ang), np.sin(ang)
    return (np.concatenate([cos, cos], axis=1).astype(np.float32),
            np.concatenate([-sin, sin], axis=1).astype(np.float32))


def _mix_mask(seg):
    r = np.arange(CHUNK)
    same = (r[:, None] // seg) == (r[None, :] // seg)
    m = same & ((r[None, :] % seg) <= (r[:, None] % seg))
    return np.concatenate([m, m], axis=1).astype(np.float32)


def _gelu(x):
    c = float(np.sqrt(2.0 / np.pi))
    half = 0.5 * x
    return half + half * jnp.tanh(x * (c + (c * 0.044715) * (x * x)))


def _silu(x):
    return x * jax.nn.sigmoid(x)


def _rotate(x, cos, sin_signed):
    return x * cos + pltpu.roll(x, RET_DK // 2, 1) * sin_signed


def _normed_bf16(x, norm_ref):
    ms = jnp.mean(x * x, axis=-1, keepdims=True)
    return (x * lax.rsqrt(ms + RMS_EPS) * norm_ref[...]).astype(BF16)


def _in_pieces(x, norm_pre_ref, w_in_ref, z_ref):
    box = {}

    def piece(c0):
        def run():
            if 'h' not in box:
                box['h'] = _normed_bf16(x, norm_pre_ref)
            z_ref[:, c0:c0 + PIECE_COLS] = jnp.dot(box['h'], w_in_ref[:, c0:c0 + PIECE_COLS],
                                                   preferred_element_type=F32)
        return run
    return [piece(c0) for c0 in range(0, IN_WIDTH, PIECE_COLS)]


def _out_pieces(x, p, mixin_ref, w_out_ref, norm_post_ref, w_gate_ref, w_ple_ref, y_ref):
    box = {}
    halves = [slice(c0, c0 + PIECE_COLS) for c0 in range(0, D_MODEL, PIECE_COLS)]

    def ple():
        box['ple'] = jnp.dot(p.astype(BF16), w_ple_ref[...], preferred_element_type=F32)

    def mix(i):
        def run():
            box['mix%d' % i] = jnp.dot(mixin_ref[...], w_out_ref[:, halves[i]], preferred_element_type=F32)
        return run

    def gate(i):
        def run():
            if 'x1' not in box:
                mixes = [box['mix%d' % j] for j in range(len(halves))]
                ms = sum(jnp.sum(m * m, axis=-1, keepdims=True) for m in mixes) * (1.0 / D_MODEL)
                scale = lax.rsqrt(ms + RMS_EPS)
                box['x1'] = [x[:, halves[j]] + mixes[j] * scale * norm_post_ref[:, halves[j]]
                             for j in range(len(halves))]
                box['x1b'] = jnp.concatenate(box['x1'], axis=1).astype(BF16)
            g = jnp.dot(box['x1b'], w_gate_ref[:, halves[i]], preferred_element_type=F32)
            y_ref[:, halves[i]] = box['x1'][i] + jax.nn.sigmoid(g) * box['ple'][:, halves[i]]
        return run
    return [ple] + [mix(i) for i in range(len(halves))] + [gate(i) for i in range(len(halves))]


def _chunk_stages(z_ref, rows, cos, sin_signed, ln_g_ref, gn_ref, bias_ref, wpair_ref, tabs_ref,
                  state_update, state_output, mixin_ref, vrows_ref):
    box = {}
    zs = lambda off, i, width: z_ref[rows, off + i * width:off + (i + 1) * width]

    def gating_in():
        sv = _gelu(z_ref[rows, O_SV:O_SV + SGU_WIDTH])
        mu = jnp.mean(sv, axis=-1, keepdims=True)
        cen = sv - mu
        var = jnp.mean(cen * cen, axis=-1, keepdims=True)
        vn = cen * lax.rsqrt(var + LN_EPS) * ln_g_ref[...]
        if vrows_ref is not None:
            vrows_ref[rows, :] = vn
        low_lanes = lax.broadcasted_iota(jnp.int32, (CHUNK, LANES), 1) < SGU_GROUP_DIM
        for m in range(SGU_GROUPS // 2):
            blk = vn[:, m * LANES:(m + 1) * LANES]
            rhs = jnp.concatenate([jnp.where(low_lanes, blk, 0.0), jnp.where(low_lanes, 0.0, blk)],
                                  axis=0).astype(BF16)
            box['mixed%d' % m] = jnp.dot(wpair_ref[m], rhs, preferred_element_type=F32)

    def retention_in():
        c, s = cos(), sin_signed()
        for h in range(RET_HEADS):
            q = _rotate(zs(O_Q, h, RET_DK), c, s)
            k = _rotate(zs(O_K, h, RET_DK), c, s)
            v = zs(O_V, h, RET_DV)
            k_b = k.astype(BF16)
            box['scores%d' % h] = lax.dot_general(q.astype(BF16), k_b, (((1,), (1,)), ((), ())),
                                                  preferred_element_type=F32)
            box['qw%d' % h] = (q * tabs_ref[1, h]).astype(BF16)
            box['v%d' % h] = v.astype(BF16)
            box['u%d' % h] = state_update(h, k_b, (v * tabs_ref[2, h]).astype(BF16))

    def gating_out_retention_mid():
        for m in range(SGU_GROUPS // 2):
            cols = slice(m * LANES, (m + 1) * LANES)
            mixed = box['mixed%d' % m] + bias_ref[:, cols]
            su = _gelu(zs(O_SU, m, LANES))
            mixin_ref[rows, cols] = (_silu(zs(O_SG, m, LANES)) * (su * mixed)).astype(BF16)
        for h in range(RET_HEADS):
            scores_b = (box['scores%d' % h] * tabs_ref[0, h]).astype(BF16)
            box['o%d' % h] = state_output(h, scores_b, box['qw%d' % h], box['v%d' % h], box['u%d' % h])

    def retention_out():
        for h in range(RET_HEADS):
            o = box['o%d' % h]
            mu = jnp.mean(o, axis=-1, keepdims=True)
            cen = o - mu
            var = jnp.mean(cen * cen, axis=-1, keepdims=True)
            on = cen * lax.rsqrt(var + LN_EPS) * gn_ref[:, h * RET_DV:(h + 1) * RET_DV]
            mixin_ref[rows, SGU_WIDTH + h * RET_DV:SGU_WIDTH + (h + 1) * RET_DV] = (
                _silu(zs(O_RG, h, RET_DV)) * on).astype(BF16)

    return [gating_in, retention_in, gating_out_retention_mid, retention_out]


def _interleave(pieces, stages):
    slots = max(len(stages), 1)
    done = 0
    for i in range(slots):
        upto = -(-(i + 1) * len(pieces) // slots)
        for piece in pieces[done:upto]:
            piece()
        done = upto
        if i < len(stages):
            stages[i]()


def _emit_blocks(in_pieces, out_pieces, stages, blocks, first_in=None, next_in=None):
    _interleave(first_in or [], [])
    for i, blk in enumerate(blocks):
        nxt = in_pieces(blocks[i + 1]) if i + 1 < len(blocks) else (next_in or [])
        if i > 0:
            ple, mix0, mix1, gate0, gate1 = out_pieces(blocks[i - 1])
            pieces = [ple, mix0, mix1] + nxt[:3] + [gate0] + nxt[3:4] + [gate1] + nxt[4:]
        else:
            pieces = nxt
        _interleave(pieces, stages(blk))
    _interleave(out_pieces(blocks[-1]), [])


def _init_tables(seg, sgu_w_ref, sgu_b_ref, mask_ref, wpm_ref, bias_ref):
    rows = lax.broadcasted_iota(jnp.int32, (CHUNK, LANES), 0)
    lanes = lax.broadcasted_iota(jnp.int32, (CHUNK, LANES), 1)
    if seg == CHUNK:
        group = lambda g: sgu_w_ref[g]
        bias_rows = sgu_b_ref[...]
    else:
        reps = CHUNK // seg
        select = jnp.where((rows < seg) & (lanes % seg == rows), 1.0, 0.0).astype(BF16)
        first = lambda a: jnp.where(lanes[:seg] < seg, a, 0.0)

        def group(g):
            stacked = jnp.concatenate([first(sgu_w_ref[g, 0:seg, :])] * reps, axis=0)
            return jnp.dot(stacked.astype(BF16), select, preferred_element_type=F32)
        b_first = jnp.where(lax.broadcasted_iota(jnp.int32, (SGU_GROUPS, LANES), 1) < seg, sgu_b_ref[...], 0.0)
        bias_rows = sum(pltpu.roll(b_first, r * seg, 1) for r in range(reps))
    for m in range(SGU_GROUPS // 2):
        pair = jnp.concatenate([group(2 * m), group(2 * m + 1)], axis=1)
        wpm_ref[m] = jnp.where(mask_ref[...] > 0.0, pair, 0.0).astype(BF16)
    group_of_lane = lax.broadcasted_iota(jnp.int32, (SGU_GROUPS, SGU_WIDTH), 1) // SGU_GROUP_DIM
    expand = jnp.where(group_of_lane == lax.broadcasted_iota(jnp.int32, (SGU_GROUPS, SGU_WIDTH), 0),
                       1.0, 0.0).astype(BF16)
    total, rest = jnp.zeros((CHUNK, SGU_WIDTH), F32), bias_rows
    for _ in range(3):
        term = rest.astype(BF16)
        total = total + lax.dot_general(term, expand, (((0,), (0,)), ((), ())), preferred_element_type=F32)
        rest = rest - term.astype(F32)
    bias_ref[...] = total


def _prompt_kernel(g_chunk, n_tiles, x_ref, x_next_ref, p_ref, cos_ref, sin_ref, w_in_ref, w_out_ref, w_gate_ref,
                   w_ple_ref, norm_pre_ref, norm_post_ref, ln_g_ref, gn_ref, sgu_w_ref, sgu_b_ref, mask_ref,
                   tabs_ref, y_ref, state_ref, w_in_b_ref, w_out_b_ref, w_gate_b_ref, w_ple_b_ref,
                   z_ref, z_first_ref, mixin_ref, wpm_ref, bias_ref):
    step = pl.program_id(0)
    first = slice(0, BLOCK_ROWS["prompt"])

    def next_first_in():
        return _in_pieces(x_next_ref[0], norm_pre_ref, w_in_b_ref, z_first_ref)

    @pl.when(step == 0)
    def _():
        _init_tables(CHUNK, sgu_w_ref, sgu_b_ref, mask_ref, wpm_ref, bias_ref)

    @pl.when(step < CAST_STEPS)
    def _():
        for src, dst in ((w_in_ref, w_in_b_ref), (w_out_ref, w_out_b_ref), (w_gate_ref, w_gate_b_ref),
                         (w_ple_ref, w_ple_b_ref)):
            n = src.shape[0]
            dst[pl.ds(pl.multiple_of(step * n, n), n), :] = src[...].astype(BF16)

    @pl.when(step == CAST_STEPS - 1)
    def _():
        _interleave(next_first_in(), [])

    @pl.when(step >= CAST_STEPS)
    def _():
        @pl.when((step - CAST_STEPS) % n_tiles == 0)
        def _():
            state_ref[...] = jnp.zeros_like(state_ref)

        def state_update(h, k_b, vw_b):
            return lax.dot_general(k_b, vw_b, (((0,), (0,)), ((), ())), preferred_element_type=F32)

        def state_output(h, scores_b, qw_b, v_b, u):
            s = state_ref[0, h]
            o = jnp.dot(jnp.concatenate([scores_b, qw_b], axis=1),
                        jnp.concatenate([v_b, s.astype(BF16)], axis=0), preferred_element_type=F32)
            state_ref[0, h] = g_chunk[h] * s + u
            return o

        def in_pieces(blk):
            return _in_pieces(x_ref[0, blk, :], norm_pre_ref, w_in_b_ref, z_ref.at[blk, :])

        def out_pieces(blk):
            return _out_pieces(x_ref[0, blk, :], p_ref[0, blk, :], mixin_ref.at[blk, :], w_out_b_ref,
                               norm_post_ref, w_gate_b_ref, w_ple_b_ref, y_ref.at[0, blk, :])

        def stages(blk):
            out = []
            for r in range(blk.start, blk.stop, CHUNK):
                rows = slice(r, r + CHUNK)
                out += _chunk_stages(z_first_ref if blk == first else z_ref, rows,
                                     lambda rows=rows: cos_ref[rows, :],
                                     lambda rows=rows: sin_ref[rows, :], ln_g_ref, gn_ref, bias_ref, wpm_ref,
                                     tabs_ref, state_update, state_output, mixin_ref, None)
            return out

        rows = BLOCK_ROWS["prompt"]
        _emit_blocks(in_pieces, out_pieces, stages, [slice(r, r + rows) for r in range(0, PROMPT_TILE, rows)],
                     next_in=next_first_in())


def _sample_kernel(g_seq, x_ref, p_ref, cos_ref, sin_ref, st_in_ref, w_in_ref, w_out_ref, w_gate_ref,
                   w_ple_ref, norm_pre_ref, norm_post_ref, ln_g_ref, gn_ref, sgu_w_ref, sgu_b_ref, mask_ref,
                   tabs_ref, y_ref, st_out_ref, vrows_ref, z_ref, mixin_ref, wpm_ref, bias_ref):
    seq_len = cos_ref.shape[0]
    seqs = CHUNK // seq_len

    @pl.when(pl.program_id(0) == 0)
    def _():
        _init_tables(seq_len, sgu_w_ref, sgu_b_ref, mask_ref, wpm_ref, bias_ref)

    per_seq = lambda a: a.astype(F32).reshape(seqs, seq_len, a.shape[-1]).astype(BF16)
    tile_rows = lambda ref: jnp.concatenate([ref[...]] * seqs, axis=0)

    def stages(blk):
        sq = slice(blk.start // seq_len, blk.stop // seq_len)

        def state_update(h, k_b, vw_b):
            u = jnp.einsum('sjd,sje->sde', per_seq(k_b), per_seq(vw_b), preferred_element_type=F32)
            st_out_ref[sq, h] = g_seq[h] * st_in_ref[sq, h] + u
            return None

        def state_output(h, scores_b, qw_b, v_b, _):
            o = jnp.dot(scores_b, v_b, preferred_element_type=F32)
            o_inter = jnp.einsum('sid,sde->sie', per_seq(qw_b), st_in_ref[sq, h].astype(BF16),
                                 preferred_element_type=F32)
            return o + o_inter.reshape(CHUNK, RET_DV)

        return _chunk_stages(z_ref, blk, lambda: tile_rows(cos_ref), lambda: tile_rows(sin_ref), ln_g_ref,
                             gn_ref, bias_ref, wpm_ref, tabs_ref, state_update, state_output, mixin_ref,
                             vrows_ref)

    def in_pieces(blk):
        return _in_pieces(x_ref[blk, :], norm_pre_ref, w_in_ref, z_ref.at[blk, :])

    def out_pieces(blk):
        return _out_pieces(x_ref[blk, :], p_ref[blk, :], mixin_ref.at[blk, :], w_out_ref, norm_post_ref,
                           w_gate_ref, w_ple_ref, y_ref.at[blk, :])

    rows = BLOCK_ROWS["sample"]
    _emit_blocks(in_pieces, out_pieces, stages, [slice(r, r + rows) for r in range(0, SAMPLE_TILE, rows)],
                 first_in=in_pieces(slice(0, rows)))


def _const_spec(shape):
    return pl.BlockSpec(shape, lambda *_: (0,) * len(shape), pipeline_mode=pl.Buffered(1))


def kernel(x_prompt, x_sample, state_ret, p_prompt, p_sample, w_in, w_out, norm_pre, norm_post, sgu_w, sgu_b,
           sgu_ln, ret_gn, w_ple_proj, w_ple_gate):
    batch, seq, _ = x_prompt.shape
    dec_batch, dec_seq, _ = x_sample.shape
    n_tiles = seq // PROMPT_TILE
    n_tok = dec_batch * dec_seq
    assert seq % PROMPT_TILE == 0 and PROMPT_TILE % BLOCK_ROWS["prompt"] == 0 and w_in.shape[0] == 1
    assert CHUNK % dec_seq == 0 and n_tok % SAMPLE_TILE == 0 and SAMPLE_TILE % BLOCK_ROWS["sample"] == 0
    assert D_MODEL % CAST_STEPS == 0 and PLE_DIM % (16 * CAST_STEPS) == 0

    small = (norm_pre[0][None, :], norm_post[0][None, :], sgu_ln[0][None, :], ret_gn[0][None, :],
             sgu_w[0], sgu_b[0])
    small_specs = [_const_spec((1, D_MODEL)), _const_spec((1, D_MODEL)), _const_spec((1, SGU_WIDTH)),
                   _const_spec((1, RET_HEADS * RET_DV)), _const_spec((SGU_GROUPS, CHUNK, CHUNK)),
                   _const_spec((SGU_GROUPS, CHUNK)), _const_spec((CHUNK, 2 * CHUNK)),
                   _const_spec((3, RET_HEADS, CHUNK, LANES))]
    table_scratch = [pltpu.VMEM((SGU_GROUPS // 2, CHUNK, 2 * CHUNK), BF16), pltpu.VMEM((CHUNK, SGU_WIDTH), F32)]
    weight_shapes = [(D_MODEL, IN_WIDTH), (D_MODEL, D_MODEL), (D_MODEL, D_MODEL), (PLE_DIM, D_MODEL)]

    tabs_p, g_chunk = _retention_tables(CHUNK)
    cos_p, sin_p = _rotary_tables(np.arange(seq))
    tile_of = lambda s: jnp.maximum(s - CAST_STEPS, 0)
    tile_spec = lambda width: pl.BlockSpec((1, PROMPT_TILE, width),
                                           lambda s: (tile_of(s) // n_tiles, tile_of(s) % n_tiles, 0))
    pos_spec = pl.BlockSpec((PROMPT_TILE, LANES), lambda s: (tile_of(s) % n_tiles, 0))
    blocks_per_tile = PROMPT_TILE // BLOCK_ROWS["prompt"]
    next_tile = lambda s: jnp.minimum(jnp.maximum(s - CAST_STEPS + 1, 0), batch * n_tiles - 1)
    next_spec = pl.BlockSpec((1, BLOCK_ROWS["prompt"], D_MODEL),
                             lambda s: (next_tile(s) // n_tiles, (next_tile(s) % n_tiles) * blocks_per_tile, 0))
    cast_spec = lambda shape: pl.BlockSpec((shape[0] // CAST_STEPS, shape[1]),
                                           lambda s: (jnp.minimum(s, CAST_STEPS - 1), 0))
    y_prompt, st_prompt, w_in_b, w_out_b, w_gate_b, w_ple_b = pl.pallas_call(
        functools.partial(_prompt_kernel, g_chunk, n_tiles),
        grid=(CAST_STEPS + batch * n_tiles,),
        in_specs=[tile_spec(D_MODEL), next_spec, tile_spec(PLE_DIM), pos_spec, pos_spec]
        + [cast_spec(shape) for shape in weight_shapes] + small_specs,
        out_specs=[tile_spec(D_MODEL),
                   pl.BlockSpec((1, RET_HEADS, RET_DK, RET_DV), lambda s: (tile_of(s) // n_tiles, 0, 0, 0))]
        + [pl.BlockSpec(shape, lambda s: (0, 0)) for shape in weight_shapes],
        out_shape=[jax.ShapeDtypeStruct((batch, seq, D_MODEL), F32),
                   jax.ShapeDtypeStruct((batch, RET_HEADS, RET_DK, RET_DV), F32)]
        + [jax.ShapeDtypeStruct(shape, BF16) for shape in weight_shapes],
        scratch_shapes=[pltpu.VMEM((PROMPT_TILE, IN_WIDTH), F32), pltpu.VMEM((BLOCK_ROWS["prompt"], IN_WIDTH), F32),
                        pltpu.VMEM((PROMPT_TILE, D_MODEL), BF16)] + table_scratch,
        compiler_params=pltpu.CompilerParams(dimension_semantics=("arbitrary",),
                                             vmem_limit_bytes=VMEM_LIMIT_BYTES),
        name="prompt_layer",
    )(x_prompt, x_prompt, p_prompt[0], cos_p, sin_p, w_in[0], w_out[0], w_ple_gate[0], w_ple_proj[0], *small,
      _mix_mask(CHUNK), tabs_p)

    tabs_s, g_seq = _retention_tables(dec_seq)
    cos_s, sin_s = _rotary_tables(PAST_LEN + np.arange(dec_seq))
    tile_seqs = SAMPLE_TILE // dec_seq
    tok_spec = lambda width: pl.BlockSpec((SAMPLE_TILE, width), lambda i: (i, 0))
    state_spec = pl.BlockSpec((tile_seqs, RET_HEADS, RET_DK, RET_DV), lambda i: (i, 0, 0, 0))
    y_sample, st_sample, v_sample = pl.pallas_call(
        functools.partial(_sample_kernel, g_seq),
        grid=(n_tok // SAMPLE_TILE,),
        in_specs=[tok_spec(D_MODEL), tok_spec(PLE_DIM), _const_spec((dec_seq, LANES)),
                  _const_spec((dec_seq, LANES)), state_spec]
        + [_const_spec(shape) for shape in weight_shapes] + small_specs,
        out_specs=[tok_spec(D_MODEL), state_spec, tok_spec(SGU_WIDTH)],
        out_shape=[jax.ShapeDtypeStruct((n_tok, D_MODEL), F32),
                   jax.ShapeDtypeStruct((dec_batch, RET_HEADS, RET_DK, RET_DV), F32),
                   jax.ShapeDtypeStruct((n_tok, SGU_WIDTH), F32)],
        scratch_shapes=[pltpu.VMEM((SAMPLE_TILE, IN_WIDTH), F32), pltpu.VMEM((SAMPLE_TILE, D_MODEL), BF16)]
        + table_scratch,
        compiler_params=pltpu.CompilerParams(dimension_semantics=("arbitrary",),
                                             vmem_limit_bytes=VMEM_LIMIT_BYTES),
        name="sample_layer",
    )(x_sample.reshape(n_tok, D_MODEL), p_sample[0].reshape(n_tok, PLE_DIM), cos_s, sin_s, state_ret[0],
      w_in_b, w_out_b, w_gate_b, w_ple_b, *small, _mix_mask(dec_seq), tabs_s)

    return (y_prompt, y_sample.reshape(dec_batch, dec_seq, D_MODEL), st_prompt[None], st_sample[None],
            v_sample.reshape(1, dec_batch, dec_seq, SGU_WIDTH))
```

```python
import functools

import numpy as np
import jax
import jax.numpy as jnp
from jax import lax
from jax.experimental import pallas as pl
from jax.experimental.pallas import tpu as pltpu

F32 = jnp.float32
BF16 = jnp.bfloat16

D_MODEL = 1024
PAST_LEN = 16384
SGU_WIDTH = 512
SGU_GROUPS = 8
SGU_GROUP_DIM = SGU_WIDTH // SGU_GROUPS
RET_HEADS = 4
RET_DK = 128
RET_DV = 128
CHUNK = 128
ROPE_THETA = 10000.0
PLE_DIM = 256
RMS_EPS = 1e-6
LN_EPS = 1e-5
IN_WIDTH = 3 * SGU_WIDTH + RET_HEADS * (2 * RET_DK + 2 * RET_DV)
O_SU, O_SV, O_SG = 0, SGU_WIDTH, 2 * SGU_WIDTH
O_Q = 3 * SGU_WIDTH
O_K = O_Q + RET_HEADS * RET_DK
O_V = O_K + RET_HEADS * RET_DK
O_RG = O_V + RET_HEADS * RET_DV

LANES = 128
VMEM_LIMIT_BYTES = 60 * 1024 * 1024

PROMPT_TILE = 1024
BLOCK_ROWS = {"prompt": 256, "sample": CHUNK}
PIECE_COLS = 512
CAST_STEPS = 8
SAMPLE_TILE = 2 * CHUNK


def _log_gamma():
    return np.log(1.0 - 2.0 ** (-5.0 - np.arange(RET_HEADS, dtype=np.float64)))


def _retention_tables(seg):
    lg = _log_gamma()[:, None, None]
    r = np.arange(CHUNK)
    i, j = r[:, None] % seg, r[None, :] % seg
    same = (r[:, None] // seg) == (r[None, :] // seg)
    decay = np.where(same & (i >= j), np.exp(lg * np.maximum(i - j, 0)), 0.0)
    wq = np.broadcast_to(np.exp(lg * (i + 1.0)), (RET_HEADS, CHUNK, LANES))
    wkv = np.broadcast_to(np.exp(lg * (seg - 1.0 - i)), (RET_HEADS, CHUNK, LANES))
    scale = RET_DK ** -0.5
    tabs = np.stack([decay * scale, wq, wkv * scale]).astype(np.float32)
    g_seg = [float(v) for v in np.exp(_log_gamma() * seg)]
    return tabs, g_seg


def _rotary_tables(pos):
    half = RET_DK // 2
    inv = ROPE_THETA ** (-np.arange(half, dtype=np.float64) / half)
    ang = pos.astype(np.float64)[:, None] * inv[None, :]
    cos, sin = np.cos(ang), np.sin(ang)
    return (np.concatenate([cos, cos], axis=1).astype(np.float32),
            np.concatenate([-sin, sin], axis=1).astype(np.float32))


def _mix_mask(seg):
    r = np.arange(CHUNK)
    same = (r[:, None] // seg) == (r[None, :] // seg)
    m = same & ((r[None, :] % seg) <= (r[:, None] % seg))
    return np.concatenate([m, m], axis=1).astype(np.float32)


def _gelu(x):
    c = float(np.sqrt(2.0 / np.pi))
    half = 0.5 * x
    return half + half * jnp.tanh(x * (c + (c * 0.044715) * (x * x)))


def _silu(x):
    return x * jax.nn.sigmoid(x)


def _rotate(x, cos, sin_signed):
    return x * cos + pltpu.roll(x, RET_DK // 2, 1) * sin_signed


def _normed_bf16(x, norm_ref):
    ms = jnp.mean(x * x, axis=-1, keepdims=True)
    return (x * lax.rsqrt(ms + RMS_EPS) * norm_ref[...]).astype(BF16)


def _in_pieces(x, norm_pre_ref, w_in_ref, z_ref):
    box = {}

    def piece(c0):
        def run():
            if 'h' not in box:
                box['h'] = _normed_bf16(x, norm_pre_ref)
            z_ref[:, c0:c0 + PIECE_COLS] = jnp.dot(box['h'], w_in_ref[:, c0:c0 + PIECE_COLS],
                                                   preferred_element_type=F32)
        return run
    return [piece(c0) for c0 in range(0, IN_WIDTH, PIECE_COLS)]


def _out_pieces(x, p, mixin_ref, w_out_ref, norm_post_ref, w_gate_ref, w_ple_ref, y_ref):
    box = {}
    halves = [slice(c0, c0 + PIECE_COLS) for c0 in range(0, D_MODEL, PIECE_COLS)]

    def ple():
        box['ple'] = jnp.dot(p.astype(BF16), w_ple_ref[...], preferred_element_type=F32)

    def mix(i):
        def run():
            box['mix%d' % i] = jnp.dot(mixin_ref[...], w_out_ref[:, halves[i]], preferred_element_type=F32)
        return run

    def gate(i):
        def run():
            if 'x1' not in box:
                mixes = [box['mix%d' % j] for j in range(len(halves))]
                ms = sum(jnp.sum(m * m, axis=-1, keepdims=True) for m in mixes) * (1.0 / D_MODEL)
                scale = lax.rsqrt(ms + RMS_EPS)
                box['x1'] = [x[:, halves[j]] + mixes[j] * scale * norm_post_ref[:, halves[j]]
                             for j in range(len(halves))]
                box['x1b'] = jnp.concatenate(box['x1'], axis=1).astype(BF16)
            g = jnp.dot(box['x1b'], w_gate_ref[:, halves[i]], preferred_element_type=F32)
            y_ref[:, halves[i]] = box['x1'][i] + jax.nn.sigmoid(g) * box['ple'][:, halves[i]]
        return run
    return [ple] + [mix(i) for i in range(len(halves))] + [gate(i) for i in range(len(halves))]


def _chunk_stages(z_ref, rows, cos, sin_signed, ln_g_ref, gn_ref, bias_ref, wpair_ref, tabs_ref,
                  state_update, state_output, mixin_ref, vrows_ref):
    box = {}
    zs = lambda off, i, width: z_ref[rows, off + i * width:off + (i + 1) * width]

    def gating_in():
        sv = _gelu(z_ref[rows, O_SV:O_SV + SGU_WIDTH])
        mu = jnp.mean(sv, axis=-1, keepdims=True)
        cen = sv - mu
        var = jnp.mean(cen * cen, axis=-1, keepdims=True)
        vn = cen * lax.rsqrt(var + LN_EPS) * ln_g_ref[...]
        if vrows_ref is not None:
            vrows_ref[rows, :] = vn
        low_lanes = lax.broadcasted_iota(jnp.int32, (CHUNK, LANES), 1) < SGU_GROUP_DIM
        for m in range(SGU_GROUPS // 2):
            blk = vn[:, m * LANES:(m + 1) * LANES]
            rhs = jnp.concatenate([jnp.where(low_lanes, blk, 0.0), jnp.where(low_lanes, 0.0, blk)],
                                  axis=0).astype(BF16)
            box['mixed%d' % m] = jnp.dot(wpair_ref[m], rhs, preferred_element_type=F32)

    def retention_in():
        c, s = cos(), sin_signed()
        for h in range(RET_HEADS):
            q = _rotate(zs(O_Q, h, RET_DK), c, s)
            k = _rotate(zs(O_K, h, RET_DK), c, s)
            v = zs(O_V, h, RET_DV)
            k_b = k.astype(BF16)
            box['scores%d' % h] = lax.dot_general(q.astype(BF16), k_b, (((1,), (1,)), ((), ())),
                                                  preferred_element_type=F32)
            box['qw%d' % h] = (q * tabs_ref[1, h]).astype(BF16)
            box['v%d' % h] = v.astype(BF16)
            box['u%d' % h] = state_update(h, k_b, (v * tabs_ref[2, h]).astype(BF16))

    def gating_out_retention_mid():
        for m in range(SGU_GROUPS // 2):
            cols = slice(m * LANES, (m + 1) * LANES)
            mixed = box['mixed%d' % m] + bias_ref[:, cols]
            su = _gelu(zs(O_SU, m, LANES))
            mixin_ref[rows, cols] = (_silu(zs(O_SG, m, LANES)) * (su * mixed)).astype(BF16)
        for h in range(RET_HEADS):
            scores_b = (box['scores%d' % h] * tabs_ref[0, h]).astype(BF16)
            box['o%d' % h] = state_output(h, scores_b, box['qw%d' % h], box['v%d' % h], box['u%d' % h])

    def retention_out():
        for h in range(RET_HEADS):
            o = box['o%d' % h]
            mu = jnp.mean(o, axis=-1, keepdims=True)
            cen = o - mu
            var = jnp.mean(cen * cen, axis=-1, keepdims=True)
            on = cen * lax.rsqrt(var + LN_EPS) * gn_ref[:, h * RET_DV:(h + 1) * RET_DV]
            mixin_ref[rows, SGU_WIDTH + h * RET_DV:SGU_WIDTH + (h + 1) * RET_DV] = (
                _silu(zs(O_RG, h, RET_DV)) * on).astype(BF16)

    return [gating_in, retention_in, gating_out_retention_mid, retention_out]


def _interleave(pieces, stages):
    slots = max(len(stages), 1)
    done = 0
    for i in range(slots):
        upto = -(-(i + 1) * len(pieces) // slots)
        for piece in pieces[done:upto]:
            piece()
        done = upto
        if i < len(stages):
            stages[i]()


def _emit_blocks(in_pieces, out_pieces, stages, blocks, first_in=None, next_in=None):
    _interleave(first_in or [], [])
    for i, blk in enumerate(blocks):
        nxt = in_pieces(blocks[i + 1]) if i + 1 < len(blocks) else (next_in or [])
        if i > 0:
            ple, mix0, mix1, gate0, gate1 = out_pieces(blocks[i - 1])
            pieces = [ple, mix0, mix1] + nxt[:3] + [gate0] + nxt[3:4] + [gate1] + nxt[4:]
        else:
            pieces = nxt
        _interleave(pieces, stages(blk))
    _interleave(out_pieces(blocks[-1]), [])


def _init_tables(seg, sgu_w_ref, sgu_b_ref, mask_ref, wpm_ref, bias_ref):
    rows = lax.broadcasted_iota(jnp.int32, (CHUNK, LANES), 0)
    lanes = lax.broadcasted_iota(jnp.int32, (CHUNK, LANES), 1)
    if seg == CHUNK:
        group = lambda g: sgu_w_ref[g]
        bias_rows = sgu_b_ref[...]
    else:
        reps = CHUNK // seg
        select = jnp.where((rows < seg) & (lanes % seg == rows), 1.0, 0.0).astype(BF16)
        first = lambda a: jnp.where(lanes[:seg] < seg, a, 0.0)

        def group(g):
            stacked = jnp.concatenate([first(sgu_w_ref[g, 0:seg, :])] * reps, axis=0)
            return jnp.dot(stacked.astype(BF16), select, preferred_element_type=F32)
        b_first = jnp.where(lax.broadcasted_iota(jnp.int32, (SGU_GROUPS, LANES), 1) < seg, sgu_b_ref[...], 0.0)
        bias_rows = sum(pltpu.roll(b_first, r * seg, 1) for r in range(reps))
    for m in range(SGU_GROUPS // 2):
        pair = jnp.concatenate([group(2 * m), group(2 * m + 1)], axis=1)
        wpm_ref[m] = jnp.where(mask_ref[...] > 0.0, pair, 0.0).astype(BF16)
    group_of_lane = lax.broadcasted_iota(jnp.int32, (SGU_GROUPS, SGU_WIDTH), 1) // SGU_GROUP_DIM
    expand = jnp.where(group_of_lane == lax.broadcasted_iota(jnp.int32, (SGU_GROUPS, SGU_WIDTH), 0),
                       1.0, 0.0).astype(BF16)
    total, rest = jnp.zeros((CHUNK, SGU_WIDTH), F32), bias_rows
    for _ in range(3):
        term = rest.astype(BF16)
        total = total + lax.dot_general(term, expand, (((0,), (0,)), ((), ())), preferred_element_type=F32)
        rest = rest - term.astype(F32)
    bias_ref[...] = total


def _prompt_kernel(g_chunk, n_tiles, x_ref, x_next_ref, p_ref, cos_ref, sin_ref, w_in_ref, w_out_ref, w_gate_ref,
                   w_ple_ref, norm_pre_ref, norm_post_ref, ln_g_ref, gn_ref, sgu_w_ref, sgu_b_ref, mask_ref,
                   tabs_ref, y_ref, state_ref, w_in_b_ref, w_out_b_ref, w_gate_b_ref, w_ple_b_ref,
                   z_ref, z_first_ref, mixin_ref, wpm_ref, bias_ref):
    step = pl.program_id(0)
    first = slice(0, BLOCK_ROWS["prompt"])

    def next_first_in():
        return _in_pieces(x_next_ref[0], norm_pre_ref, w_in_b_ref, z_first_ref)

    @pl.when(step == 0)
    def _():
        _init_tables(CHUNK, sgu_w_ref, sgu_b_ref, mask_ref, wpm_ref, bias_ref)

    @pl.when(step < CAST_STEPS)
    def _():
        for src, dst in ((w_in_ref, w_in_b_ref), (w_out_ref, w_out_b_ref), (w_gate_ref, w_gate_b_ref),
                         (w_ple_ref, w_ple_b_ref)):
            n = src.shape[0]
            dst[pl.ds(pl.multiple_of(step * n, n), n), :] = src[...].astype(BF16)

    @pl.when(step == CAST_STEPS - 1)
    def _():
        _interleave(next_first_in(), [])

    @pl.when(step >= CAST_STEPS)
    def _():
        @pl.when((step - CAST_STEPS) % n_tiles == 0)
        def _():
            state_ref[...] = jnp.zeros_like(state_ref)

        def state_update(h, k_b, vw_b):
            return lax.dot_general(k_b, vw_b, (((0,), (0,)), ((), ())), preferred_element_type=F32)

        def state_output(h, scores_b, qw_b, v_b, u):
            s = state_ref[0, h]
            o = jnp.dot(jnp.concatenate([scores_b, qw_b], axis=1),
                        jnp.concatenate([v_b, s.astype(BF16)], axis=0), preferred_element_type=F32)
            state_ref[0, h] = g_chunk[h] * s + u
            return o

        def in_pieces(blk):
            return _in_pieces(x_ref[0, blk, :], norm_pre_ref, w_in_b_ref, z_ref.at[blk, :])

        def out_pieces(blk):
            return _out_pieces(x_ref[0, blk, :], p_ref[0, blk, :], mixin_ref.at[blk, :], w_out_b_ref,
                               norm_post_ref, w_gate_b_ref, w_ple_b_ref, y_ref.at[0, blk, :])

        def stages(blk):
            out = []
            for r in range(blk.start, blk.stop, CHUNK):
                rows = slice(r, r + CHUNK)
                out += _chunk_stages(z_first_ref if blk == first else z_ref, rows,
                                     lambda rows=rows: cos_ref[rows, :],
                                     lambda rows=rows: sin_ref[rows, :], ln_g_ref, gn_ref, bias_ref, wpm_ref,
                                     tabs_ref, state_update, state_output, mixin_ref, None)
            return out

        rows = BLOCK_ROWS["prompt"]
        _emit_blocks(in_pieces, out_pieces, stages, [slice(r, r + rows) for r in range(0, PROMPT_TILE, rows)],
                     next_in=next_first_in())


def _sample_kernel(g_seq, x_ref, p_ref, cos_ref, sin_ref, st_in_ref, w_in_ref, w_out_ref, w_gate_ref,
                   w_ple_ref, norm_pre_ref, norm_post_ref, ln_g_ref, gn_ref, sgu_w_ref, sgu_b_ref, mask_ref,
                   tabs_ref, y_ref, st_out_ref, vrows_ref, z_ref, mixin_ref, wpm_ref, bias_ref):
    seq_len = cos_ref.shape[0]
    seqs = CHUNK // seq_len

    @pl.when(pl.program_id(0) == 0)
    def _():
        _init_tables(seq_len, sgu_w_ref, sgu_b_ref, mask_ref, wpm_ref, bias_ref)

    pairs = seqs // 2
    per_pair = lambda a: a.reshape(pairs, 2 * seq_len, a.shape[-1])
    first_of_pair = lax.broadcasted_iota(jnp.int32, (1, 2 * seq_len, 1), 1) < seq_len
    tile_rows = lambda ref: jnp.concatenate([ref[...]] * seqs, axis=0)

    def stages(blk):
        pq = slice(blk.start // (2 * seq_len), blk.stop // (2 * seq_len))

        def state_update(h, k_b, vw_b):
            vw = per_pair(vw_b)
            zero = jnp.zeros_like(vw)
            both = jnp.concatenate([jnp.where(first_of_pair, vw, zero), jnp.where(first_of_pair, zero, vw)],
                                   axis=-1)
            u = jnp.einsum('pjd,pje->pde', per_pair(k_b), both, preferred_element_type=F32)
            st_out_ref[pq, 0, h] = g_seq[h] * st_in_ref[pq, 0, h] + u[:, :, :RET_DV]
            st_out_ref[pq, 1, h] = g_seq[h] * st_in_ref[pq, 1, h] + u[:, :, RET_DV:]
            return None

        def state_output(h, scores_b, qw_b, v_b, _):
            o = jnp.dot(scores_b, v_b, preferred_element_type=F32)
            s_pair = jnp.concatenate([st_in_ref[pq, 0, h], st_in_ref[pq, 1, h]], axis=-1).astype(BF16)
            both = jnp.einsum('pid,pde->pie', per_pair(qw_b), s_pair, preferred_element_type=F32)
            o_inter = jnp.where(first_of_pair, both[:, :, :RET_DV], both[:, :, RET_DV:])
            return o + o_inter.reshape(CHUNK, RET_DV)

        return _chunk_stages(z_ref, blk, lambda: tile_rows(cos_ref), lambda: tile_rows(sin_ref), ln_g_ref,
                             gn_ref, bias_ref, wpm_ref, tabs_ref, state_update, state_output, mixin_ref,
                             vrows_ref)

    def in_pieces(blk):
        return _in_pieces(x_ref[blk, :], norm_pre_ref, w_in_ref, z_ref.at[blk, :])

    def out_pieces(blk):
        return _out_pieces(x_ref[blk, :], p_ref[blk, :], mixin_ref.at[blk, :], w_out_ref, norm_post_ref,
                           w_gate_ref, w_ple_ref, y_ref.at[blk, :])

    rows = BLOCK_ROWS["sample"]
    _emit_blocks(in_pieces, out_pieces, stages, [slice(r, r + rows) for r in range(0, SAMPLE_TILE, rows)],
                 first_in=in_pieces(slice(0, rows)))


def _const_spec(shape):
    return pl.BlockSpec(shape, lambda *_: (0,) * len(shape), pipeline_mode=pl.Buffered(1))


def kernel(x_prompt, x_sample, state_ret, p_prompt, p_sample, w_in, w_out, norm_pre, norm_post, sgu_w, sgu_b,
           sgu_ln, ret_gn, w_ple_proj, w_ple_gate):
    batch, seq, _ = x_prompt.shape
    dec_batch, dec_seq, _ = x_sample.shape
    n_tiles = seq // PROMPT_TILE
    n_tok = dec_batch * dec_seq
    assert seq % PROMPT_TILE == 0 and PROMPT_TILE % BLOCK_ROWS["prompt"] == 0 and w_in.shape[0] == 1
    assert CHUNK % dec_seq == 0 and n_tok % SAMPLE_TILE == 0 and SAMPLE_TILE % BLOCK_ROWS["sample"] == 0
    assert D_MODEL % CAST_STEPS == 0 and PLE_DIM % (16 * CAST_STEPS) == 0

    small = (norm_pre[0][None, :], norm_post[0][None, :], sgu_ln[0][None, :], ret_gn[0][None, :],
             sgu_w[0], sgu_b[0])
    small_specs = [_const_spec((1, D_MODEL)), _const_spec((1, D_MODEL)), _const_spec((1, SGU_WIDTH)),
                   _const_spec((1, RET_HEADS * RET_DV)), _const_spec((SGU_GROUPS, CHUNK, CHUNK)),
                   _const_spec((SGU_GROUPS, CHUNK)), _const_spec((CHUNK, 2 * CHUNK)),
                   _const_spec((3, RET_HEADS, CHUNK, LANES))]
    table_scratch = [pltpu.VMEM((SGU_GROUPS // 2, CHUNK, 2 * CHUNK), BF16), pltpu.VMEM((CHUNK, SGU_WIDTH), F32)]
    weight_shapes = [(D_MODEL, IN_WIDTH), (D_MODEL, D_MODEL), (D_MODEL, D_MODEL), (PLE_DIM, D_MODEL)]

    tabs_p, g_chunk = _retention_tables(CHUNK)
    cos_p, sin_p = _rotary_tables(np.arange(seq))
    tile_of = lambda s: jnp.maximum(s - CAST_STEPS, 0)
    tile_spec = lambda width: pl.BlockSpec((1, PROMPT_TILE, width),
                                           lambda s: (tile_of(s) // n_tiles, tile_of(s) % n_tiles, 0))
    pos_spec = pl.BlockSpec((PROMPT_TILE, LANES), lambda s: (tile_of(s) % n_tiles, 0))
    blocks_per_tile = PROMPT_TILE // BLOCK_ROWS["prompt"]
    next_tile = lambda s: jnp.minimum(jnp.maximum(s - CAST_STEPS + 1, 0), batch * n_tiles - 1)
    next_spec = pl.BlockSpec((1, BLOCK_ROWS["prompt"], D_MODEL),
                             lambda s: (next_tile(s) // n_tiles, (next_tile(s) % n_tiles) * blocks_per_tile, 0))
    cast_spec = lambda shape: pl.BlockSpec((shape[0] // CAST_STEPS, shape[1]),
                                           lambda s: (jnp.minimum(s, CAST_STEPS - 1), 0))
    y_prompt, st_prompt, w_in_b, w_out_b, w_gate_b, w_ple_b = pl.pallas_call(
        functools.partial(_prompt_kernel, g_chunk, n_tiles),
        grid=(CAST_STEPS + batch * n_tiles,),
        in_specs=[tile_spec(D_MODEL), next_spec, tile_spec(PLE_DIM), pos_spec, pos_spec]
        + [cast_spec(shape) for shape in weight_shapes] + small_specs,
        out_specs=[tile_spec(D_MODEL),
                   pl.BlockSpec((1, RET_HEADS, RET_DK, RET_DV), lambda s: (tile_of(s) // n_tiles, 0, 0, 0))]
        + [pl.BlockSpec(shape, lambda s: (0, 0)) for shape in weight_shapes],
        out_shape=[jax.ShapeDtypeStruct((batch, seq, D_MODEL), F32),
                   jax.ShapeDtypeStruct((batch, RET_HEADS, RET_DK, RET_DV), F32)]
        + [jax.ShapeDtypeStruct(shape, BF16) for shape in weight_shapes],
        scratch_shapes=[pltpu.VMEM((PROMPT_TILE, IN_WIDTH), F32), pltpu.VMEM((BLOCK_ROWS["prompt"], IN_WIDTH), F32),
                        pltpu.VMEM((PROMPT_TILE, D_MODEL), BF16)] + table_scratch,
        compiler_params=pltpu.CompilerParams(dimension_semantics=("arbitrary",),
                                             vmem_limit_bytes=VMEM_LIMIT_BYTES),
        name="prompt_layer",
    )(x_prompt, x_prompt, p_prompt[0], cos_p, sin_p, w_in[0], w_out[0], w_ple_gate[0], w_ple_proj[0], *small,
      _mix_mask(CHUNK), tabs_p)

    tabs_s, g_seq = _retention_tables(dec_seq)
    cos_s, sin_s = _rotary_tables(PAST_LEN + np.arange(dec_seq))
    tile_seqs = SAMPLE_TILE // dec_seq
    tok_spec = lambda width: pl.BlockSpec((SAMPLE_TILE, width), lambda i: (i, 0))
    state_spec = pl.BlockSpec((tile_seqs // 2, 2, RET_HEADS, RET_DK, RET_DV), lambda i: (i, 0, 0, 0, 0))
    paired = (dec_batch // 2, 2, RET_HEADS, RET_DK, RET_DV)
    y_sample, st_sample, v_sample = pl.pallas_call(
        functools.partial(_sample_kernel, g_seq),
        grid=(n_tok // SAMPLE_TILE,),
        in_specs=[tok_spec(D_MODEL), tok_spec(PLE_DIM), _const_spec((dec_seq, LANES)),
                  _const_spec((dec_seq, LANES)), state_spec]
        + [_const_spec(shape) for shape in weight_shapes] + small_specs,
        out_specs=[tok_spec(D_MODEL), state_spec, tok_spec(SGU_WIDTH)],
        out_shape=[jax.ShapeDtypeStruct((n_tok, D_MODEL), F32),
                   jax.ShapeDtypeStruct(paired, F32), jax.ShapeDtypeStruct((n_tok, SGU_WIDTH), F32)],
        scratch_shapes=[pltpu.VMEM((SAMPLE_TILE, IN_WIDTH), F32), pltpu.VMEM((SAMPLE_TILE, D_MODEL), BF16)]
        + table_scratch,
        compiler_params=pltpu.CompilerParams(dimension_semantics=("arbitrary",),
                                             vmem_limit_bytes=VMEM_LIMIT_BYTES),
        name="sample_layer",
    )(x_sample.reshape(n_tok, D_MODEL), p_sample[0].reshape(n_tok, PLE_DIM), cos_s, sin_s,
      state_ret[0].reshape(paired),
      w_in_b, w_out_b, w_gate_b, w_ple_b, *small, _mix_mask(dec_seq), tabs_s)

    return (y_prompt, y_sample.reshape(dec_batch, dec_seq, D_MODEL), st_prompt[None],
            st_sample.reshape(state_ret.shape),
            v_sample.reshape(1, dec_batch, dec_seq, SGU_WIDTH))
```

```python
import functools

import numpy as np
import jax
import jax.numpy as jnp
from jax import lax
from jax.experimental import pallas as pl
from jax.experimental.pallas import tpu as pltpu

F32 = jnp.float32
BF16 = jnp.bfloat16

D_MODEL = 1024
PAST_LEN = 16384
SGU_WIDTH = 512
SGU_GROUPS = 8
SGU_GROUP_DIM = SGU_WIDTH // SGU_GROUPS
RET_HEADS = 4
RET_DK = 128
RET_DV = 128
CHUNK = 128
ROPE_THETA = 10000.0
PLE_DIM = 256
RMS_EPS = 1e-6
LN_EPS = 1e-5
IN_WIDTH = 3 * SGU_WIDTH + RET_HEADS * (2 * RET_DK + 2 * RET_DV)
O_SU, O_SV, O_SG = 0, SGU_WIDTH, 2 * SGU_WIDTH
O_Q = 3 * SGU_WIDTH
O_K = O_Q + RET_HEADS * RET_DK
O_V = O_K + RET_HEADS * RET_DK
O_RG = O_V + RET_HEADS * RET_DV

LANES = 128
VMEM_LIMIT_BYTES = 60 * 1024 * 1024

PROMPT_TILE = 1024
BLOCK_ROWS = {"prompt": 256, "sample": CHUNK}
PIECE_COLS = 512
CAST_STEPS = 8
SAMPLE_TILE = 2 * CHUNK


def _log_gamma():
    return np.log(1.0 - 2.0 ** (-5.0 - np.arange(RET_HEADS, dtype=np.float64)))


def _retention_tables(seg):
    lg = _log_gamma()[:, None, None]
    r = np.arange(CHUNK)
    i, j = r[:, None] % seg, r[None, :] % seg
    same = (r[:, None] // seg) == (r[None, :] // seg)
    decay = np.where(same & (i >= j), np.exp(lg * np.maximum(i - j, 0)), 0.0)
    wq = np.broadcast_to(np.exp(lg * (i + 1.0)), (RET_HEADS, CHUNK, LANES))
    wkv = np.broadcast_to(np.exp(lg * (seg - 1.0 - i)), (RET_HEADS, CHUNK, LANES))
    scale = RET_DK ** -0.5
    tabs = np.stack([decay * scale, wq, wkv * scale]).astype(np.float32)
    g_seg = [float(v) for v in np.exp(_log_gamma() * seg)]
    return tabs, g_seg


def _rotary_tables(pos):
    half = RET_DK // 2
    inv = ROPE_THETA ** (-np.arange(half, dtype=np.float64) / half)
    ang = pos.astype(np.float64)[:, None] * inv[None, :]
    cos, sin = np.cos(ang), np.sin(ang)
    return (np.concatenate([cos, cos], axis=1).astype(np.float32),
            np.concatenate([-sin, sin], axis=1).astype(np.float32))


def _mix_mask(seg):
    r = np.arange(CHUNK)
    same = (r[:, None] // seg) == (r[None, :] // seg)
    m = same & ((r[None, :] % seg) <= (r[:, None] % seg))
    return np.concatenate([m, m], axis=1).astype(np.float32)


def _gelu(x):
    c = float(np.sqrt(2.0 / np.pi))
    half = 0.5 * x
    return half + half * jnp.tanh(x * (c + (c * 0.044715) * (x * x)))


def _silu(x):
    return x * jax.nn.sigmoid(x)


def _rotate(x, cos, sin_signed):
    return x * cos + pltpu.roll(x, RET_DK // 2, 1) * sin_signed


def _normed_bf16(x, norm_ref):
    ms = jnp.mean(x * x, axis=-1, keepdims=True)
    return (x * lax.rsqrt(ms + RMS_EPS) * norm_ref[...]).astype(BF16)


def _in_pieces(x, norm_pre_ref, w_in_ref, z_ref):
    box = {}

    def piece(c0):
        def run():
            if 'h' not in box:
                box['h'] = _normed_bf16(x, norm_pre_ref)
            z_ref[:, c0:c0 + PIECE_COLS] = jnp.dot(box['h'], w_in_ref[:, c0:c0 + PIECE_COLS],
                                                   preferred_element_type=F32)
        return run
    return [piece(c0) for c0 in range(0, IN_WIDTH, PIECE_COLS)]


def _out_pieces(x, p, mixin_ref, w_out_ref, norm_post_ref, w_gate_ref, w_ple_ref, y_ref):
    box = {}
    halves = [slice(c0, c0 + PIECE_COLS) for c0 in range(0, D_MODEL, PIECE_COLS)]

    def ple():
        box['ple'] = jnp.dot(p.astype(BF16), w_ple_ref[...], preferred_element_type=F32)

    def mix(i):
        def run():
            box['mix%d' % i] = jnp.dot(mixin_ref[...], w_out_ref[:, halves[i]], preferred_element_type=F32)
        return run

    def gate(i):
        def run():
            if 'x1' not in box:
                mixes = [box['mix%d' % j] for j in range(len(halves))]
                ms = sum(jnp.sum(m * m, axis=-1, keepdims=True) for m in mixes) * (1.0 / D_MODEL)
                scale = lax.rsqrt(ms + RMS_EPS)
                box['x1'] = [x[:, halves[j]] + mixes[j] * scale * norm_post_ref[:, halves[j]]
                             for j in range(len(halves))]
                box['x1b'] = jnp.concatenate(box['x1'], axis=1).astype(BF16)
            g = jnp.dot(box['x1b'], w_gate_ref[:, halves[i]], preferred_element_type=F32)
            y_ref[:, halves[i]] = box['x1'][i] + jax.nn.sigmoid(g) * box['ple'][:, halves[i]]
        return run
    return [ple] + [mix(i) for i in range(len(halves))] + [gate(i) for i in range(len(halves))]


def _chunk_stages(z_ref, rows, cos, sin_signed, ln_g_ref, gn_ref, bias_ref, wpair_ref, tabs_ref,
                  state_update, state_output, mixin_ref, vrows_ref):
    box = {}
    zs = lambda off, i, width: z_ref[rows, off + i * width:off + (i + 1) * width]

    def gating_in():
        sv = _gelu(z_ref[rows, O_SV:O_SV + SGU_WIDTH])
        mu = jnp.mean(sv, axis=-1, keepdims=True)
        cen = sv - mu
        var = jnp.mean(cen * cen, axis=-1, keepdims=True)
        vn = cen * lax.rsqrt(var + LN_EPS) * ln_g_ref[...]
        if vrows_ref is not None:
            vrows_ref[rows, :] = vn
        low_lanes = lax.broadcasted_iota(jnp.int32, (CHUNK, LANES), 1) < SGU_GROUP_DIM
        for m in range(SGU_GROUPS // 2):
            blk = vn[:, m * LANES:(m + 1) * LANES]
            rhs = jnp.concatenate([jnp.where(low_lanes, blk, 0.0), jnp.where(low_lanes, 0.0, blk)],
                                  axis=0).astype(BF16)
            box['mixed%d' % m] = jnp.dot(wpair_ref[m], rhs, preferred_element_type=F32)

    def retention_in():
        c, s = cos(), sin_signed()
        for h in range(RET_HEADS):
            q = _rotate(zs(O_Q, h, RET_DK), c, s)
            k = _rotate(zs(O_K, h, RET_DK), c, s)
            v = zs(O_V, h, RET_DV)
            k_b = k.astype(BF16)
            box['scores%d' % h] = lax.dot_general(q.astype(BF16), k_b, (((1,), (1,)), ((), ())),
                                                  preferred_element_type=F32)
            box['qw%d' % h] = (q * tabs_ref[1, h]).astype(BF16)
            box['v%d' % h] = v.astype(BF16)
            box['u%d' % h] = state_update(h, k_b, (v * tabs_ref[2, h]).astype(BF16))

    def gating_out_retention_mid():
        for m in range(SGU_GROUPS // 2):
            cols = slice(m * LANES, (m + 1) * LANES)
            mixed = box['mixed%d' % m] + bias_ref[:, cols]
            su = _gelu(zs(O_SU, m, LANES))
            mixin_ref[rows, cols] = (_silu(zs(O_SG, m, LANES)) * (su * mixed)).astype(BF16)
        for h in range(RET_HEADS):
            scores_b = (box['scores%d' % h] * tabs_ref[0, h]).astype(BF16)
            box['o%d' % h] = state_output(h, scores_b, box['qw%d' % h], box['v%d' % h], box['u%d' % h])

    def retention_out():
        for h in range(RET_HEADS):
            o = box['o%d' % h]
            mu = jnp.mean(o, axis=-1, keepdims=True)
            cen = o - mu
            var = jnp.mean(cen * cen, axis=-1, keepdims=True)
            on = cen * lax.rsqrt(var + LN_EPS) * gn_ref[:, h * RET_DV:(h + 1) * RET_DV]
            mixin_ref[rows, SGU_WIDTH + h * RET_DV:SGU_WIDTH + (h + 1) * RET_DV] = (
                _silu(zs(O_RG, h, RET_DV)) * on).astype(BF16)

    return [gating_in, retention_in, gating_out_retention_mid, retention_out]


def _interleave(pieces, stages):
    slots = max(len(stages), 1)
    done = 0
    for i in range(slots):
        upto = -(-(i + 1) * len(pieces) // slots)
        for piece in pieces[done:upto]:
            piece()
        done = upto
        if i < len(stages):
            stages[i]()


def _emit_blocks(in_pieces, out_pieces, stages, blocks, first_in=None, next_in=None):
    _interleave(first_in or [], [])
    for i, blk in enumerate(blocks):
        nxt = in_pieces(blocks[i + 1]) if i + 1 < len(blocks) else (next_in or [])
        if i > 0:
            ple, mix0, mix1, gate0, gate1 = out_pieces(blocks[i - 1])
            pieces = [ple, mix0, mix1] + nxt[:3] + [gate0] + nxt[3:4] + [gate1] + nxt[4:]
        else:
            pieces = nxt
        _interleave(pieces, stages(blk))
    _interleave(out_pieces(blocks[-1]), [])


def _init_tables(seg, sgu_w_ref, sgu_b_ref, mask_ref, wpm_ref, bias_ref):
    rows = lax.broadcasted_iota(jnp.int32, (CHUNK, LANES), 0)
    lanes = lax.broadcasted_iota(jnp.int32, (CHUNK, LANES), 1)
    if seg == CHUNK:
        group = lambda g: sgu_w_ref[g]
        bias_rows = sgu_b_ref[...]
    else:
        reps = CHUNK // seg
        select = jnp.where((rows < seg) & (lanes % seg == rows), 1.0, 0.0).astype(BF16)
        first = lambda a: jnp.where(lanes[:seg] < seg, a, 0.0)

        def group(g):
            stacked = jnp.concatenate([first(sgu_w_ref[g, 0:seg, :])] * reps, axis=0)
            return jnp.dot(stacked.astype(BF16), select, preferred_element_type=F32)
        b_first = jnp.where(lax.broadcasted_iota(jnp.int32, (SGU_GROUPS, LANES), 1) < seg, sgu_b_ref[...], 0.0)
        bias_rows = sum(pltpu.roll(b_first, r * seg, 1) for r in range(reps))
    for m in range(SGU_GROUPS // 2):
        pair = jnp.concatenate([group(2 * m), group(2 * m + 1)], axis=1)
        wpm_ref[m] = jnp.where(mask_ref[...] > 0.0, pair, 0.0).astype(BF16)
    group_of_lane = lax.broadcasted_iota(jnp.int32, (SGU_GROUPS, SGU_WIDTH), 1) // SGU_GROUP_DIM
    expand = jnp.where(group_of_lane == lax.broadcasted_iota(jnp.int32, (SGU_GROUPS, SGU_WIDTH), 0),
                       1.0, 0.0).astype(BF16)
    total, rest = jnp.zeros((CHUNK, SGU_WIDTH), F32), bias_rows
    for _ in range(3):
        term = rest.astype(BF16)
        total = total + lax.dot_general(term, expand, (((0,), (0,)), ((), ())), preferred_element_type=F32)
        rest = rest - term.astype(F32)
    bias_ref[...] = total


def _prompt_kernel(g_chunk, n_tiles, x_ref, x_next_ref, p_ref, cos_ref, sin_ref, w_in_ref, w_out_ref, w_gate_ref,
                   w_ple_ref, norm_pre_ref, norm_post_ref, ln_g_ref, gn_ref, sgu_w_ref, sgu_b_ref, mask_ref,
                   tabs_ref, y_ref, state_ref, w_in_o_ref, w_out_o_ref, w_gate_o_ref, w_ple_o_ref,
                   z_ref, z_first_ref, mixin_ref, wpm_ref, bias_ref, w_in_b_ref, w_out_b_ref, w_gate_b_ref,
                   w_ple_b_ref):
    step = pl.program_id(0)
    first = slice(0, BLOCK_ROWS["prompt"])

    def next_first_in():
        return _in_pieces(x_next_ref[0], norm_pre_ref, w_in_b_ref, z_first_ref)

    @pl.when(step == 0)
    def _():
        _init_tables(CHUNK, sgu_w_ref, sgu_b_ref, mask_ref, wpm_ref, bias_ref)

    @pl.when(step < CAST_STEPS)
    def _():
        for src, dst, out in ((w_in_ref, w_in_b_ref, w_in_o_ref), (w_out_ref, w_out_b_ref, w_out_o_ref),
                              (w_gate_ref, w_gate_b_ref, w_gate_o_ref), (w_ple_ref, w_ple_b_ref, w_ple_o_ref)):
            n = src.shape[0]
            chunk = src[...].astype(BF16)
            dst[pl.ds(pl.multiple_of(step * n, n), n), :] = chunk
            out[...] = chunk

    @pl.when(step == CAST_STEPS - 1)
    def _():
        _interleave(next_first_in(), [])

    @pl.when(step >= CAST_STEPS)
    def _():
        @pl.when((step - CAST_STEPS) % n_tiles == 0)
        def _():
            state_ref[...] = jnp.zeros_like(state_ref)

        def state_update(h, k_b, vw_b):
            return lax.dot_general(k_b, vw_b, (((0,), (0,)), ((), ())), preferred_element_type=F32)

        def state_output(h, scores_b, qw_b, v_b, u):
            s = state_ref[0, h]
            o = jnp.dot(jnp.concatenate([scores_b, qw_b], axis=1),
                        jnp.concatenate([v_b, s.astype(BF16)], axis=0), preferred_element_type=F32)
            state_ref[0, h] = g_chunk[h] * s + u
            return o

        def in_pieces(blk):
            return _in_pieces(x_ref[0, blk, :], norm_pre_ref, w_in_b_ref, z_ref.at[blk, :])

        def out_pieces(blk):
            return _out_pieces(x_ref[0, blk, :], p_ref[0, blk, :], mixin_ref.at[blk, :], w_out_b_ref,
                               norm_post_ref, w_gate_b_ref, w_ple_b_ref, y_ref.at[0, blk, :])

        def stages(blk):
            out = []
            for r in range(blk.start, blk.stop, CHUNK):
                rows = slice(r, r + CHUNK)
                out += _chunk_stages(z_first_ref if blk == first else z_ref, rows,
                                     lambda rows=rows: cos_ref[rows, :],
                                     lambda rows=rows: sin_ref[rows, :], ln_g_ref, gn_ref, bias_ref, wpm_ref,
                                     tabs_ref, state_update, state_output, mixin_ref, None)
            return out

        rows = BLOCK_ROWS["prompt"]
        _emit_blocks(in_pieces, out_pieces, stages, [slice(r, r + rows) for r in range(0, PROMPT_TILE, rows)],
                     next_in=next_first_in())


def _sample_kernel(g_seq, x_ref, p_ref, cos_ref, sin_ref, st_in_ref, w_in_ref, w_out_ref, w_gate_ref,
                   w_ple_ref, norm_pre_ref, norm_post_ref, ln_g_ref, gn_ref, sgu_w_ref, sgu_b_ref, mask_ref,
                   tabs_ref, y_ref, st_out_ref, vrows_ref, z_ref, mixin_ref, wpm_ref, bias_ref):
    seq_len = cos_ref.shape[0]
    seqs = CHUNK // seq_len

    @pl.when(pl.program_id(0) == 0)
    def _():
        _init_tables(seq_len, sgu_w_ref, sgu_b_ref, mask_ref, wpm_ref, bias_ref)

    pairs = seqs // 2
    per_pair = lambda a: a.reshape(pairs, 2 * seq_len, a.shape[-1])
    first_of_pair = lax.broadcasted_iota(jnp.int32, (1, 2 * seq_len, 1), 1) < seq_len
    tile_rows = lambda ref: jnp.concatenate([ref[...]] * seqs, axis=0)

    def stages(blk):
        pq = slice(blk.start // (2 * seq_len), blk.stop // (2 * seq_len))

        def state_update(h, k_b, vw_b):
            vw = per_pair(vw_b)
            zero = jnp.zeros_like(vw)
            both = jnp.concatenate([jnp.where(first_of_pair, vw, zero), jnp.where(first_of_pair, zero, vw)],
                                   axis=-1)
            u = jnp.einsum('pjd,pje->pde', per_pair(k_b), both, preferred_element_type=F32)
            st_out_ref[pq, 0, h] = g_seq[h] * st_in_ref[pq, 0, h] + u[:, :, :RET_DV]
            st_out_ref[pq, 1, h] = g_seq[h] * st_in_ref[pq, 1, h] + u[:, :, RET_DV:]
            return None

        def state_output(h, scores_b, qw_b, v_b, _):
            o = jnp.dot(scores_b, v_b, preferred_element_type=F32)
            s_pair = jnp.concatenate([st_in_ref[pq, 0, h], st_in_ref[pq, 1, h]], axis=-1).astype(BF16)
            both = jnp.einsum('pid,pde->pie', per_pair(qw_b), s_pair, preferred_element_type=F32)
            o_inter = jnp.where(first_of_pair, both[:, :, :RET_DV], both[:, :, RET_DV:])
            return o + o_inter.reshape(CHUNK, RET_DV)

        return _chunk_stages(z_ref, blk, lambda: tile_rows(cos_ref), lambda: tile_rows(sin_ref), ln_g_ref,
                             gn_ref, bias_ref, wpm_ref, tabs_ref, state_update, state_output, mixin_ref,
                             vrows_ref)

    def in_pieces(blk):
        return _in_pieces(x_ref[blk, :], norm_pre_ref, w_in_ref, z_ref.at[blk, :])

    def out_pieces(blk):
        return _out_pieces(x_ref[blk, :], p_ref[blk, :], mixin_ref.at[blk, :], w_out_ref, norm_post_ref,
                           w_gate_ref, w_ple_ref, y_ref.at[blk, :])

    rows = BLOCK_ROWS["sample"]
    _emit_blocks(in_pieces, out_pieces, stages, [slice(r, r + rows) for r in range(0, SAMPLE_TILE, rows)],
                 first_in=in_pieces(slice(0, rows)))


def _const_spec(shape):
    return pl.BlockSpec(shape, lambda *_: (0,) * len(shape), pipeline_mode=pl.Buffered(1))


def kernel(x_prompt, x_sample, state_ret, p_prompt, p_sample, w_in, w_out, norm_pre, norm_post, sgu_w, sgu_b,
           sgu_ln, ret_gn, w_ple_proj, w_ple_gate):
    batch, seq, _ = x_prompt.shape
    dec_batch, dec_seq, _ = x_sample.shape
    n_tiles = seq // PROMPT_TILE
    n_tok = dec_batch * dec_seq
    assert seq % PROMPT_TILE == 0 and PROMPT_TILE % BLOCK_ROWS["prompt"] == 0 and w_in.shape[0] == 1
    assert CHUNK % dec_seq == 0 and n_tok % SAMPLE_TILE == 0 and SAMPLE_TILE % BLOCK_ROWS["sample"] == 0
    assert D_MODEL % CAST_STEPS == 0 and PLE_DIM % (16 * CAST_STEPS) == 0

    small = (norm_pre[0][None, :], norm_post[0][None, :], sgu_ln[0][None, :], ret_gn[0][None, :],
             sgu_w[0], sgu_b[0])
    small_specs = [_const_spec((1, D_MODEL)), _const_spec((1, D_MODEL)), _const_spec((1, SGU_WIDTH)),
                   _const_spec((1, RET_HEADS * RET_DV)), _const_spec((SGU_GROUPS, CHUNK, CHUNK)),
                   _const_spec((SGU_GROUPS, CHUNK)), _const_spec((CHUNK, 2 * CHUNK)),
                   _const_spec((3, RET_HEADS, CHUNK, LANES))]
    table_scratch = [pltpu.VMEM((SGU_GROUPS // 2, CHUNK, 2 * CHUNK), BF16), pltpu.VMEM((CHUNK, SGU_WIDTH), F32)]
    weight_shapes = [(D_MODEL, IN_WIDTH), (D_MODEL, D_MODEL), (D_MODEL, D_MODEL), (PLE_DIM, D_MODEL)]

    tabs_p, g_chunk = _retention_tables(CHUNK)
    cos_p, sin_p = _rotary_tables(np.arange(seq))
    tile_of = lambda s: jnp.maximum(s - CAST_STEPS, 0)
    tile_spec = lambda width: pl.BlockSpec((1, PROMPT_TILE, width),
                                           lambda s: (tile_of(s) // n_tiles, tile_of(s) % n_tiles, 0))
    pos_spec = pl.BlockSpec((PROMPT_TILE, LANES), lambda s: (tile_of(s) % n_tiles, 0))
    blocks_per_tile = PROMPT_TILE // BLOCK_ROWS["prompt"]
    next_tile = lambda s: jnp.minimum(jnp.maximum(s - CAST_STEPS + 1, 0), batch * n_tiles - 1)
    next_spec = pl.BlockSpec((1, BLOCK_ROWS["prompt"], D_MODEL),
                             lambda s: (next_tile(s) // n_tiles, (next_tile(s) % n_tiles) * blocks_per_tile, 0))
    cast_spec = lambda shape: pl.BlockSpec((shape[0] // CAST_STEPS, shape[1]),
                                           lambda s: (jnp.minimum(s, CAST_STEPS - 1), 0))
    y_prompt, st_prompt, w_in_b, w_out_b, w_gate_b, w_ple_b = pl.pallas_call(
        functools.partial(_prompt_kernel, g_chunk, n_tiles),
        grid=(CAST_STEPS + batch * n_tiles,),
        in_specs=[tile_spec(D_MODEL), next_spec, tile_spec(PLE_DIM), pos_spec, pos_spec]
        + [cast_spec(shape) for shape in weight_shapes] + small_specs,
        out_specs=[tile_spec(D_MODEL),
                   pl.BlockSpec((1, RET_HEADS, RET_DK, RET_DV), lambda s: (tile_of(s) // n_tiles, 0, 0, 0))]
        + [cast_spec(shape) for shape in weight_shapes],
        out_shape=[jax.ShapeDtypeStruct((batch, seq, D_MODEL), F32),
                   jax.ShapeDtypeStruct((batch, RET_HEADS, RET_DK, RET_DV), F32)]
        + [jax.ShapeDtypeStruct(shape, BF16) for shape in weight_shapes],
        scratch_shapes=[pltpu.VMEM((PROMPT_TILE, IN_WIDTH), F32), pltpu.VMEM((BLOCK_ROWS["prompt"], IN_WIDTH), F32),
                        pltpu.VMEM((PROMPT_TILE, D_MODEL), BF16)] + table_scratch
        + [pltpu.VMEM(shape, BF16) for shape in weight_shapes],
        compiler_params=pltpu.CompilerParams(dimension_semantics=("arbitrary",),
                                             vmem_limit_bytes=VMEM_LIMIT_BYTES),
        name="prompt_layer",
    )(x_prompt, x_prompt, p_prompt[0], cos_p, sin_p, w_in[0], w_out[0], w_ple_gate[0], w_ple_proj[0], *small,
      _mix_mask(CHUNK), tabs_p)

    tabs_s, g_seq = _retention_tables(dec_seq)
    cos_s, sin_s = _rotary_tables(PAST_LEN + np.arange(dec_seq))
    tile_seqs = SAMPLE_TILE // dec_seq
    tok_spec = lambda width: pl.BlockSpec((SAMPLE_TILE, width), lambda i: (i, 0))
    state_spec = pl.BlockSpec((tile_seqs // 2, 2, RET_HEADS, RET_DK, RET_DV), lambda i: (i, 0, 0, 0, 0))
    paired = (dec_batch // 2, 2, RET_HEADS, RET_DK, RET_DV)
    y_sample, st_sample, v_sample = pl.pallas_call(
        functools.partial(_sample_kernel, g_seq),
        grid=(n_tok // SAMPLE_TILE,),
        in_specs=[tok_spec(D_MODEL), tok_spec(PLE_DIM), _const_spec((dec_seq, LANES)),
                  _const_spec((dec_seq, LANES)), state_spec]
        + [_const_spec(shape) for shape in weight_shapes] + small_specs,
        out_specs=[tok_spec(D_MODEL), state_spec, tok_spec(SGU_WIDTH)],
        out_shape=[jax.ShapeDtypeStruct((n_tok, D_MODEL), F32),
                   jax.ShapeDtypeStruct(paired, F32), jax.ShapeDtypeStruct((n_tok, SGU_WIDTH), F32)],
        scratch_shapes=[pltpu.VMEM((SAMPLE_TILE, IN_WIDTH), F32), pltpu.VMEM((SAMPLE_TILE, D_MODEL), BF16)]
        + table_scratch,
        compiler_params=pltpu.CompilerParams(dimension_semantics=("arbitrary",),
                                             vmem_limit_bytes=VMEM_LIMIT_BYTES),
        name="sample_layer",
    )(x_sample.reshape(n_tok, D_MODEL), p_sample[0].reshape(n_tok, PLE_DIM), cos_s, sin_s,
      state_ret[0].reshape(paired),
      w_in_b, w_out_b, w_gate_b, w_ple_b, *small, _mix_mask(dec_seq), tabs_s)

    return (y_prompt, y_sample.reshape(dec_batch, dec_seq, D_MODEL), st_prompt[None],
            st_sample.reshape(state_ret.shape),
            v_sample.reshape(1, dec_batch, dec_seq, SGU_WIDTH))
```

```python
import functools

import numpy as np
import jax
import jax.numpy as jnp
from jax import lax
from jax.experimental import pallas as pl
from jax.experimental.pallas import tpu as pltpu

F32 = jnp.float32
BF16 = jnp.bfloat16

D_MODEL = 1024
PAST_LEN = 16384
SGU_WIDTH = 512
SGU_GROUPS = 8
SGU_GROUP_DIM = SGU_WIDTH // SGU_GROUPS
RET_HEADS = 4
RET_DK = 128
RET_DV = 128
CHUNK = 128
ROPE_THETA = 10000.0
PLE_DIM = 256
RMS_EPS = 1e-6
LN_EPS = 1e-5
IN_WIDTH = 3 * SGU_WIDTH + RET_HEADS * (2 * RET_DK + 2 * RET_DV)
O_SU, O_SV, O_SG = 0, SGU_WIDTH, 2 * SGU_WIDTH
O_Q = 3 * SGU_WIDTH
O_K = O_Q + RET_HEADS * RET_DK
O_V = O_K + RET_HEADS * RET_DK
O_RG = O_V + RET_HEADS * RET_DV

LANES = 128
VMEM_LIMIT_BYTES = 60 * 1024 * 1024

PROMPT_TILE = 1024
BLOCK_ROWS = {"prompt": 256, "sample": CHUNK}
PIECE_COLS = 512
IN_PIECE_COLS = 1792
CAST_STEPS = 8
SAMPLE_TILE = 2 * CHUNK


def _log_gamma():
    return np.log(1.0 - 2.0 ** (-5.0 - np.arange(RET_HEADS, dtype=np.float64)))


def _retention_tables(seg):
    lg = _log_gamma()[:, None, None]
    r = np.arange(CHUNK)
    i, j = r[:, None] % seg, r[None, :] % seg
    same = (r[:, None] // seg) == (r[None, :] // seg)
    decay = np.where(same & (i >= j), np.exp(lg * np.maximum(i - j, 0)), 0.0)
    wq = np.broadcast_to(np.exp(lg * (i + 1.0)), (RET_HEADS, CHUNK, LANES))
    wkv = np.broadcast_to(np.exp(lg * (seg - 1.0 - i)), (RET_HEADS, CHUNK, LANES))
    scale = RET_DK ** -0.5
    tabs = np.stack([decay * scale, wq, wkv * scale]).astype(np.float32)
    g_seg = [float(v) for v in np.exp(_log_gamma() * seg)]
    return tabs, g_seg


def _rotary_tables(pos):
    half = RET_DK // 2
    inv = ROPE_THETA ** (-np.arange(half, dtype=np.float64) / half)
    ang = pos.astype(np.float64)[:, None] * inv[None, :]
    cos, sin = np.cos(ang), np.sin(ang)
    return (np.concatenate([cos, cos], axis=1).astype(np.float32),
            np.concatenate([-sin, sin], axis=1).astype(np.float32))


def _mix_mask(seg):
    r = np.arange(CHUNK)
    same = (r[:, None] // seg) == (r[None, :] // seg)
    m = same & ((r[None, :] % seg) <= (r[:, None] % seg))
    return np.concatenate([m, m], axis=1).astype(np.float32)


def _gelu(x):
    c = float(np.sqrt(2.0 / np.pi))
    half = 0.5 * x
    return half + half * jnp.tanh(x * (c + (c * 0.044715) * (x * x)))


def _silu(x):
    return x * jax.nn.sigmoid(x)


def _rotate(x, cos, sin_signed):
    return x * cos + pltpu.roll(x, RET_DK // 2, 1) * sin_signed


def _normed_bf16(x, norm_ref):
    ms = jnp.mean(x * x, axis=-1, keepdims=True)
    return (x * lax.rsqrt(ms + RMS_EPS) * norm_ref[...]).astype(BF16)


def _in_pieces(x, norm_pre_ref, w_in_ref, z_ref):
    box = {}

    def piece(c0):
        def run():
            if 'h' not in box:
                box['h'] = _normed_bf16(x, norm_pre_ref)
            z_ref[:, c0:c0 + IN_PIECE_COLS] = jnp.dot(box['h'], w_in_ref[:, c0:c0 + IN_PIECE_COLS],
                                                      preferred_element_type=F32)
        return run
    return [piece(c0) for c0 in range(0, IN_WIDTH, IN_PIECE_COLS)]


def _out_pieces(x, p, mixin_ref, w_out_ref, norm_post_ref, w_gate_ref, w_ple_ref, y_ref):
    box = {}
    halves = [slice(c0, c0 + PIECE_COLS) for c0 in range(0, D_MODEL, PIECE_COLS)]

    def ple():
        box['ple'] = jnp.dot(p.astype(BF16), w_ple_ref[...], preferred_element_type=F32)

    def mix(i):
        def run():
            box['mix%d' % i] = jnp.dot(mixin_ref[...], w_out_ref[:, halves[i]], preferred_element_type=F32)
        return run

    def gate(i):
        def run():
            if 'x1' not in box:
                mixes = [box['mix%d' % j] for j in range(len(halves))]
                ms = sum(jnp.sum(m * m, axis=-1, keepdims=True) for m in mixes) * (1.0 / D_MODEL)
                scale = lax.rsqrt(ms + RMS_EPS)
                box['x1'] = [x[:, halves[j]] + mixes[j] * scale * norm_post_ref[:, halves[j]]
                             for j in range(len(halves))]
                box['x1b'] = jnp.concatenate(box['x1'], axis=1).astype(BF16)
            g = jnp.dot(box['x1b'], w_gate_ref[:, halves[i]], preferred_element_type=F32)
            y_ref[:, halves[i]] = box['x1'][i] + jax.nn.sigmoid(g) * box['ple'][:, halves[i]]
        return run
    return [ple] + [mix(i) for i in range(len(halves))] + [gate(i) for i in range(len(halves))]


def _chunk_stages(z_ref, rows, cos, sin_signed, ln_g_ref, gn_ref, bias_ref, wpair_ref, tabs_ref,
                  state_update, state_output, mixin_ref, vrows_ref):
    box = {}
    zs = lambda off, i, width: z_ref[rows, off + i * width:off + (i + 1) * width]

    def gating_in():
        sv = _gelu(z_ref[rows, O_SV:O_SV + SGU_WIDTH])
        mu = jnp.mean(sv, axis=-1, keepdims=True)
        cen = sv - mu
        var = jnp.mean(cen * cen, axis=-1, keepdims=True)
        vn = cen * lax.rsqrt(var + LN_EPS) * ln_g_ref[...]
        if vrows_ref is not None:
            vrows_ref[rows, :] = vn
        low_lanes = lax.broadcasted_iota(jnp.int32, (CHUNK, LANES), 1) < SGU_GROUP_DIM
        for m in range(SGU_GROUPS // 2):
            blk = vn[:, m * LANES:(m + 1) * LANES]
            rhs = jnp.concatenate([jnp.where(low_lanes, blk, 0.0), jnp.where(low_lanes, 0.0, blk)],
                                  axis=0).astype(BF16)
            box['mixed%d' % m] = jnp.dot(wpair_ref[m], rhs, preferred_element_type=F32)

    def retention_in():
        c, s = cos(), sin_signed()
        for h in range(RET_HEADS):
            q = _rotate(zs(O_Q, h, RET_DK), c, s)
            k = _rotate(zs(O_K, h, RET_DK), c, s)
            v = zs(O_V, h, RET_DV)
            k_b = k.astype(BF16)
            box['scores%d' % h] = lax.dot_general(q.astype(BF16), k_b, (((1,), (1,)), ((), ())),
                                                  preferred_element_type=F32)
            box['qw%d' % h] = (q * tabs_ref[1, h]).astype(BF16)
            box['v%d' % h] = v.astype(BF16)
            box['u%d' % h] = state_update(h, k_b, (v * tabs_ref[2, h]).astype(BF16))

    def gating_out_retention_mid():
        for m in range(SGU_GROUPS // 2):
            cols = slice(m * LANES, (m + 1) * LANES)
            mixed = box['mixed%d' % m] + bias_ref[:, cols]
            su = _gelu(zs(O_SU, m, LANES))
            mixin_ref[rows, cols] = (_silu(zs(O_SG, m, LANES)) * (su * mixed)).astype(BF16)
        for h in range(RET_HEADS):
            scores_b = (box['scores%d' % h] * tabs_ref[0, h]).astype(BF16)
            box['o%d' % h] = state_output(h, scores_b, box['qw%d' % h], box['v%d' % h], box['u%d' % h])

    def retention_out():
        for h in range(RET_HEADS):
            o = box['o%d' % h]
            mu = jnp.mean(o, axis=-1, keepdims=True)
            cen = o - mu
            var = jnp.mean(cen * cen, axis=-1, keepdims=True)
            on = cen * lax.rsqrt(var + LN_EPS) * gn_ref[:, h * RET_DV:(h + 1) * RET_DV]
            mixin_ref[rows, SGU_WIDTH + h * RET_DV:SGU_WIDTH + (h + 1) * RET_DV] = (
                _silu(zs(O_RG, h, RET_DV)) * on).astype(BF16)

    return [gating_in, retention_in, gating_out_retention_mid, retention_out]


def _interleave(pieces, stages):
    slots = max(len(stages), 1)
    done = 0
    for i in range(slots):
        upto = -(-(i + 1) * len(pieces) // slots)
        for piece in pieces[done:upto]:
            piece()
        done = upto
        if i < len(stages):
            stages[i]()


def _emit_blocks(in_pieces, out_pieces, stages, blocks, first_in=None, next_in=None):
    _interleave(first_in or [], [])
    for i, blk in enumerate(blocks):
        nxt = in_pieces(blocks[i + 1]) if i + 1 < len(blocks) else (next_in or [])
        if i > 0:
            ple, mix0, mix1, gate0, gate1 = out_pieces(blocks[i - 1])
            pieces = [ple, mix0, mix1] + nxt[:3] + [gate0] + nxt[3:4] + [gate1] + nxt[4:]
        else:
            pieces = nxt
        _interleave(pieces, stages(blk))
    _interleave(out_pieces(blocks[-1]), [])


def _init_tables(seg, sgu_w_ref, sgu_b_ref, mask_ref, wpm_ref, bias_ref):
    rows = lax.broadcasted_iota(jnp.int32, (CHUNK, LANES), 0)
    lanes = lax.broadcasted_iota(jnp.int32, (CHUNK, LANES), 1)
    if seg == CHUNK:
        group = lambda g: sgu_w_ref[g]
        bias_rows = sgu_b_ref[...]
    else:
        reps = CHUNK // seg
        select = jnp.where((rows < seg) & (lanes % seg == rows), 1.0, 0.0).astype(BF16)
        first = lambda a: jnp.where(lanes[:seg] < seg, a, 0.0)

        def group(g):
            stacked = jnp.concatenate([first(sgu_w_ref[g, 0:seg, :])] * reps, axis=0)
            return jnp.dot(stacked.astype(BF16), select, preferred_element_type=F32)
        b_first = jnp.where(lax.broadcasted_iota(jnp.int32, (SGU_GROUPS, LANES), 1) < seg, sgu_b_ref[...], 0.0)
        bias_rows = sum(pltpu.roll(b_first, r * seg, 1) for r in range(reps))
    for m in range(SGU_GROUPS // 2):
        pair = jnp.concatenate([group(2 * m), group(2 * m + 1)], axis=1)
        wpm_ref[m] = jnp.where(mask_ref[...] > 0.0, pair, 0.0).astype(BF16)
    group_of_lane = lax.broadcasted_iota(jnp.int32, (SGU_GROUPS, SGU_WIDTH), 1) // SGU_GROUP_DIM
    expand = jnp.where(group_of_lane == lax.broadcasted_iota(jnp.int32, (SGU_GROUPS, SGU_WIDTH), 0),
                       1.0, 0.0).astype(BF16)
    total, rest = jnp.zeros((CHUNK, SGU_WIDTH), F32), bias_rows
    for _ in range(3):
        term = rest.astype(BF16)
        total = total + lax.dot_general(term, expand, (((0,), (0,)), ((), ())), preferred_element_type=F32)
        rest = rest - term.astype(F32)
    bias_ref[...] = total


def _prompt_kernel(g_chunk, n_tiles, x_ref, x_next_ref, p_ref, cos_ref, sin_ref, w_in_ref, w_out_ref, w_gate_ref,
                   w_ple_ref, norm_pre_ref, norm_post_ref, ln_g_ref, gn_ref, sgu_w_ref, sgu_b_ref, mask_ref,
                   tabs_ref, y_ref, state_ref, w_in_b_ref, w_out_b_ref, w_gate_b_ref, w_ple_b_ref,
                   z_ref, z_first_ref, mixin_ref, wpm_ref, bias_ref):
    step = pl.program_id(0)
    first = slice(0, BLOCK_ROWS["prompt"])

    def next_first_in():
        return _in_pieces(x_next_ref[0], norm_pre_ref, w_in_b_ref, z_first_ref)

    @pl.when(step == 0)
    def _():
        _init_tables(CHUNK, sgu_w_ref, sgu_b_ref, mask_ref, wpm_ref, bias_ref)

    @pl.when(step < CAST_STEPS)
    def _():
        for src, dst in ((w_in_ref, w_in_b_ref), (w_out_ref, w_out_b_ref), (w_gate_ref, w_gate_b_ref),
                         (w_ple_ref, w_ple_b_ref)):
            n = src.shape[0]
            dst[pl.ds(pl.multiple_of(step * n, n), n), :] = src[...].astype(BF16)

    @pl.when(step == CAST_STEPS - 1)
    def _():
        _interleave(next_first_in(), [])

    @pl.when(step >= CAST_STEPS)
    def _():
        @pl.when((step - CAST_STEPS) % n_tiles == 0)
        def _():
            state_ref[...] = jnp.zeros_like(state_ref)

        def state_update(h, k_b, vw_b):
            return lax.dot_general(k_b, vw_b, (((0,), (0,)), ((), ())), preferred_element_type=F32)

        def state_output(h, scores_b, qw_b, v_b, u):
            s = state_ref[0, h]
            o = jnp.dot(jnp.concatenate([scores_b, qw_b], axis=1),
                        jnp.concatenate([v_b, s.astype(BF16)], axis=0), preferred_element_type=F32)
            state_ref[0, h] = g_chunk[h] * s + u
            return o

        def in_pieces(blk):
            return _in_pieces(x_ref[0, blk, :], norm_pre_ref, w_in_b_ref, z_ref.at[blk, :])

        def out_pieces(blk):
            return _out_pieces(x_ref[0, blk, :], p_ref[0, blk, :], mixin_ref.at[blk, :], w_out_b_ref,
                               norm_post_ref, w_gate_b_ref, w_ple_b_ref, y_ref.at[0, blk, :])

        def stages(blk):
            out = []
            for r in range(blk.start, blk.stop, CHUNK):
                rows = slice(r, r + CHUNK)
                out += _chunk_stages(z_first_ref if blk == first else z_ref, rows,
                                     lambda rows=rows: cos_ref[rows, :],
                                     lambda rows=rows: sin_ref[rows, :], ln_g_ref, gn_ref, bias_ref, wpm_ref,
                                     tabs_ref, state_update, state_output, mixin_ref, None)
            return out

        rows = BLOCK_ROWS["prompt"]
        _emit_blocks(in_pieces, out_pieces, stages, [slice(r, r + rows) for r in range(0, PROMPT_TILE, rows)],
                     next_in=next_first_in())


def _sample_kernel(g_seq, x_ref, p_ref, cos_ref, sin_ref, st_in_ref, w_in_ref, w_out_ref, w_gate_ref,
                   w_ple_ref, norm_pre_ref, norm_post_ref, ln_g_ref, gn_ref, sgu_w_ref, sgu_b_ref, mask_ref,
                   tabs_ref, y_ref, st_out_ref, vrows_ref, z_ref, mixin_ref, wpm_ref, bias_ref):
    seq_len = cos_ref.shape[0]
    seqs = CHUNK // seq_len

    @pl.when(pl.program_id(0) == 0)
    def _():
        _init_tables(seq_len, sgu_w_ref, sgu_b_ref, mask_ref, wpm_ref, bias_ref)

    pairs = seqs // 2
    per_pair = lambda a: a.reshape(pairs, 2 * seq_len, a.shape[-1])
    first_of_pair = lax.broadcasted_iota(jnp.int32, (1, 2 * seq_len, 1), 1) < seq_len
    tile_rows = lambda ref: jnp.concatenate([ref[...]] * seqs, axis=0)

    def stages(blk):
        pq = slice(blk.start // (2 * seq_len), blk.stop // (2 * seq_len))

        def state_update(h, k_b, vw_b):
            vw = per_pair(vw_b)
            zero = jnp.zeros_like(vw)
            both = jnp.concatenate([jnp.where(first_of_pair, vw, zero), jnp.where(first_of_pair, zero, vw)],
                                   axis=-1)
            u = jnp.einsum('pjd,pje->pde', per_pair(k_b), both, preferred_element_type=F32)
            st_out_ref[pq, 0, h] = g_seq[h] * st_in_ref[pq, 0, h] + u[:, :, :RET_DV]
            st_out_ref[pq, 1, h] = g_seq[h] * st_in_ref[pq, 1, h] + u[:, :, RET_DV:]
            return None

        def state_output(h, scores_b, qw_b, v_b, _):
            o = jnp.dot(scores_b, v_b, preferred_element_type=F32)
            s_pair = jnp.concatenate([st_in_ref[pq, 0, h], st_in_ref[pq, 1, h]], axis=-1).astype(BF16)
            both = jnp.einsum('pid,pde->pie', per_pair(qw_b), s_pair, preferred_element_type=F32)
            o_inter = jnp.where(first_of_pair, both[:, :, :RET_DV], both[:, :, RET_DV:])
            return o + o_inter.reshape(CHUNK, RET_DV)

        return _chunk_stages(z_ref, blk, lambda: tile_rows(cos_ref), lambda: tile_rows(sin_ref), ln_g_ref,
                             gn_ref, bias_ref, wpm_ref, tabs_ref, state_update, state_output, mixin_ref,
                             vrows_ref)

    def in_pieces(blk):
        return _in_pieces(x_ref[blk, :], norm_pre_ref, w_in_ref, z_ref.at[blk, :])

    def out_pieces(blk):
        return _out_pieces(x_ref[blk, :], p_ref[blk, :], mixin_ref.at[blk, :], w_out_ref, norm_post_ref,
                           w_gate_ref, w_ple_ref, y_ref.at[blk, :])

    rows = BLOCK_ROWS["sample"]
    _emit_blocks(in_pieces, out_pieces, stages, [slice(r, r + rows) for r in range(0, SAMPLE_TILE, rows)],
                 first_in=in_pieces(slice(0, rows)))


def _const_spec(shape):
    return pl.BlockSpec(shape, lambda *_: (0,) * len(shape), pipeline_mode=pl.Buffered(1))


def kernel(x_prompt, x_sample, state_ret, p_prompt, p_sample, w_in, w_out, norm_pre, norm_post, sgu_w, sgu_b,
           sgu_ln, ret_gn, w_ple_proj, w_ple_gate):
    batch, seq, _ = x_prompt.shape
    dec_batch, dec_seq, _ = x_sample.shape
    n_tiles = seq // PROMPT_TILE
    n_tok = dec_batch * dec_seq
    assert seq % PROMPT_TILE == 0 and PROMPT_TILE % BLOCK_ROWS["prompt"] == 0 and w_in.shape[0] == 1
    assert CHUNK % dec_seq == 0 and n_tok % SAMPLE_TILE == 0 and SAMPLE_TILE % BLOCK_ROWS["sample"] == 0
    assert D_MODEL % CAST_STEPS == 0 and PLE_DIM % (16 * CAST_STEPS) == 0

    small = (norm_pre[0][None, :], norm_post[0][None, :], sgu_ln[0][None, :], ret_gn[0][None, :],
             sgu_w[0], sgu_b[0])
    small_specs = [_const_spec((1, D_MODEL)), _const_spec((1, D_MODEL)), _const_spec((1, SGU_WIDTH)),
                   _const_spec((1, RET_HEADS * RET_DV)), _const_spec((SGU_GROUPS, CHUNK, CHUNK)),
                   _const_spec((SGU_GROUPS, CHUNK)), _const_spec((CHUNK, 2 * CHUNK)),
                   _const_spec((3, RET_HEADS, CHUNK, LANES))]
    table_scratch = [pltpu.VMEM((SGU_GROUPS // 2, CHUNK, 2 * CHUNK), BF16), pltpu.VMEM((CHUNK, SGU_WIDTH), F32)]
    weight_shapes = [(D_MODEL, IN_WIDTH), (D_MODEL, D_MODEL), (D_MODEL, D_MODEL), (PLE_DIM, D_MODEL)]

    tabs_p, g_chunk = _retention_tables(CHUNK)
    cos_p, sin_p = _rotary_tables(np.arange(seq))
    tile_of = lambda s: jnp.maximum(s - CAST_STEPS, 0)
    tile_spec = lambda width: pl.BlockSpec((1, PROMPT_TILE, width),
                                           lambda s: (tile_of(s) // n_tiles, tile_of(s) % n_tiles, 0))
    pos_spec = pl.BlockSpec((PROMPT_TILE, LANES), lambda s: (tile_of(s) % n_tiles, 0))
    blocks_per_tile = PROMPT_TILE // BLOCK_ROWS["prompt"]
    next_tile = lambda s: jnp.minimum(jnp.maximum(s - CAST_STEPS + 1, 0), batch * n_tiles - 1)
    next_spec = pl.BlockSpec((1, BLOCK_ROWS["prompt"], D_MODEL),
                             lambda s: (next_tile(s) // n_tiles, (next_tile(s) % n_tiles) * blocks_per_tile, 0))
    cast_spec = lambda shape: pl.BlockSpec((shape[0] // CAST_STEPS, shape[1]),
                                           lambda s: (jnp.minimum(s, CAST_STEPS - 1), 0))
    y_prompt, st_prompt, w_in_b, w_out_b, w_gate_b, w_ple_b = pl.pallas_call(
        functools.partial(_prompt_kernel, g_chunk, n_tiles),
        grid=(CAST_STEPS + batch * n_tiles,),
        in_specs=[tile_spec(D_MODEL), next_spec, tile_spec(PLE_DIM), pos_spec, pos_spec]
        + [cast_spec(shape) for shape in weight_shapes] + small_specs,
        out_specs=[tile_spec(D_MODEL),
                   pl.BlockSpec((1, RET_HEADS, RET_DK, RET_DV), lambda s: (tile_of(s) // n_tiles, 0, 0, 0))]
        + [pl.BlockSpec(shape, lambda s: (0, 0)) for shape in weight_shapes],
        out_shape=[jax.ShapeDtypeStruct((batch, seq, D_MODEL), F32),
                   jax.ShapeDtypeStruct((batch, RET_HEADS, RET_DK, RET_DV), F32)]
        + [jax.ShapeDtypeStruct(shape, BF16) for shape in weight_shapes],
        scratch_shapes=[pltpu.VMEM((PROMPT_TILE, IN_WIDTH), F32), pltpu.VMEM((BLOCK_ROWS["prompt"], IN_WIDTH), F32),
                        pltpu.VMEM((PROMPT_TILE, D_MODEL), BF16)] + table_scratch,
        compiler_params=pltpu.CompilerParams(dimension_semantics=("arbitrary",),
                                             vmem_limit_bytes=VMEM_LIMIT_BYTES),
        name="prompt_layer",
    )(x_prompt, x_prompt, p_prompt[0], cos_p, sin_p, w_in[0], w_out[0], w_ple_gate[0], w_ple_proj[0], *small,
      _mix_mask(CHUNK), tabs_p)

    tabs_s, g_seq = _retention_tables(dec_seq)
    cos_s, sin_s = _rotary_tables(PAST_LEN + np.arange(dec_seq))
    tile_seqs = SAMPLE_TILE // dec_seq
    tok_spec = lambda width: pl.BlockSpec((SAMPLE_TILE, width), lambda i: (i, 0))
    state_spec = pl.BlockSpec((tile_seqs // 2, 2, RET_HEADS, RET_DK, RET_DV), lambda i: (i, 0, 0, 0, 0))
    paired = (dec_batch // 2, 2, RET_HEADS, RET_DK, RET_DV)
    y_sample, st_sample, v_sample = pl.pallas_call(
        functools.partial(_sample_kernel, g_seq),
        grid=(n_tok // SAMPLE_TILE,),
        in_specs=[tok_spec(D_MODEL), tok_spec(PLE_DIM), _const_spec((dec_seq, LANES)),
                  _const_spec((dec_seq, LANES)), state_spec]
        + [_const_spec(shape) for shape in weight_shapes] + small_specs,
        out_specs=[tok_spec(D_MODEL), state_spec, tok_spec(SGU_WIDTH)],
        out_shape=[jax.ShapeDtypeStruct((n_tok, D_MODEL), F32),
                   jax.ShapeDtypeStruct(paired, F32), jax.ShapeDtypeStruct((n_tok, SGU_WIDTH), F32)],
        scratch_shapes=[pltpu.VMEM((SAMPLE_TILE, IN_WIDTH), F32), pltpu.VMEM((SAMPLE_TILE, D_MODEL), BF16)]
        + table_scratch,
        compiler_params=pltpu.CompilerParams(dimension_semantics=("arbitrary",),
                                             vmem_limit_bytes=VMEM_LIMIT_BYTES),
        name="sample_layer",
    )(x_sample.reshape(n_tok, D_MODEL), p_sample[0].reshape(n_tok, PLE_DIM), cos_s, sin_s,
      state_ret[0].reshape(paired),
      w_in_b, w_out_b, w_gate_b, w_ple_b, *small, _mix_mask(dec_seq), tabs_s)

    return (y_prompt, y_sample.reshape(dec_batch, dec_seq, D_MODEL), st_prompt[None],
            st_sample.reshape(state_ret.shape),
            v_sample.reshape(1, dec_batch, dec_seq, SGU_WIDTH))
```

```python
import functools

import numpy as np
import jax
import jax.numpy as jnp
from jax import lax
from jax.experimental import pallas as pl
from jax.experimental.pallas import tpu as pltpu

F32 = jnp.float32
BF16 = jnp.bfloat16

D_MODEL = 1024
PAST_LEN = 16384
SGU_WIDTH = 512
SGU_GROUPS = 8
SGU_GROUP_DIM = SGU_WIDTH // SGU_GROUPS
RET_HEADS = 4
RET_DK = 128
RET_DV = 128
CHUNK = 128
ROPE_THETA = 10000.0
PLE_DIM = 256
RMS_EPS = 1e-6
LN_EPS = 1e-5
IN_WIDTH = 3 * SGU_WIDTH + RET_HEADS * (2 * RET_DK + 2 * RET_DV)
O_SU, O_SV, O_SG = 0, SGU_WIDTH, 2 * SGU_WIDTH
O_Q = 3 * SGU_WIDTH
O_K = O_Q + RET_HEADS * RET_DK
O_V = O_K + RET_HEADS * RET_DK
O_RG = O_V + RET_HEADS * RET_DV

LANES = 128
VMEM_LIMIT_BYTES = 60 * 1024 * 1024

PROMPT_TILE = 1024
BLOCK_ROWS = {"prompt": 256, "sample": CHUNK}
PIECE_COLS = 512
CAST_STEPS = 8
SAMPLE_TILE = 2 * CHUNK


def _log_gamma():
    return np.log(1.0 - 2.0 ** (-5.0 - np.arange(RET_HEADS, dtype=np.float64)))


def _retention_tables(seg):
    lg = _log_gamma()[:, None, None]
    r = np.arange(CHUNK)
    i, j = r[:, None] % seg, r[None, :] % seg
    same = (r[:, None] // seg) == (r[None, :] // seg)
    decay = np.where(same & (i >= j), np.exp(lg * np.maximum(i - j, 0)), 0.0)
    wq = np.broadcast_to(np.exp(lg * (i + 1.0)), (RET_HEADS, CHUNK, LANES))
    wkv = np.broadcast_to(np.exp(lg * (seg - 1.0 - i)), (RET_HEADS, CHUNK, LANES))
    scale = RET_DK ** -0.5
    tabs = np.stack([decay * scale, wq, wkv * scale]).astype(np.float32)
    g_seg = [float(v) for v in np.exp(_log_gamma() * seg)]
    return tabs, g_seg


def _rotary_tables(pos):
    half = RET_DK // 2
    inv = ROPE_THETA ** (-np.arange(half, dtype=np.float64) / half)
    ang = pos.astype(np.float64)[:, None] * inv[None, :]
    cos, sin = np.cos(ang), np.sin(ang)
    return (np.concatenate([cos, cos], axis=1).astype(np.float32),
            np.concatenate([-sin, sin], axis=1).astype(np.float32))


def _mix_mask(seg):
    r = np.arange(CHUNK)
    same = (r[:, None] // seg) == (r[None, :] // seg)
    m = same & ((r[None, :] % seg) <= (r[:, None] % seg))
    return np.concatenate([m, m], axis=1).astype(np.float32)


def _gelu(x):
    c = float(np.sqrt(2.0 / np.pi))
    half = 0.5 * x
    return half + half * jnp.tanh(x * (c + (c * 0.044715) * (x * x)))


def _silu(x):
    return x * jax.nn.sigmoid(x)


def _rotate(x, cos, sin_signed):
    return x * cos + pltpu.roll(x, RET_DK // 2, 1) * sin_signed


def _normed_bf16(x, norm_ref):
    ms = jnp.mean(x * x, axis=-1, keepdims=True)
    return (x * lax.rsqrt(ms + RMS_EPS) * norm_ref[...]).astype(BF16)


def _in_pieces(x, norm_pre_ref, w_in_ref, z_ref):
    box = {}

    def piece(c0):
        def run():
            if 'h' not in box:
                box['h'] = _normed_bf16(x, norm_pre_ref)
            z_ref[:, c0:c0 + PIECE_COLS] = jnp.dot(box['h'], w_in_ref[:, c0:c0 + PIECE_COLS],
                                                   preferred_element_type=F32)
        return run
    return [piece(c0) for c0 in range(0, IN_WIDTH, PIECE_COLS)]


def _out_pieces(x, p, mixin_ref, w_out_ref, norm_post_ref, w_gate_ref, w_ple_ref, y_ref):
    box = {}
    halves = [slice(c0, c0 + PIECE_COLS) for c0 in range(0, D_MODEL, PIECE_COLS)]

    def ple():
        box['ple'] = jnp.dot(p.astype(BF16), w_ple_ref[...], preferred_element_type=F32)

    def mix(i):
        def run():
            box['mix%d' % i] = jnp.dot(mixin_ref[...], w_out_ref[:, halves[i]], preferred_element_type=F32)
        return run

    def gate(i):
        def run():
            if 'x1' not in box:
                mixes = [box['mix%d' % j] for j in range(len(halves))]
                ms = sum(jnp.sum(m * m, axis=-1, keepdims=True) for m in mixes) * (1.0 / D_MODEL)
                scale = lax.rsqrt(ms + RMS_EPS)
                box['x1'] = [x[:, halves[j]] + mixes[j] * scale * norm_post_ref[:, halves[j]]
                             for j in range(len(halves))]
                box['x1b'] = jnp.concatenate(box['x1'], axis=1).astype(BF16)
            g = jnp.dot(box['x1b'], w_gate_ref[:, halves[i]], preferred_element_type=F32)
            y_ref[:, halves[i]] = box['x1'][i] + jax.nn.sigmoid(g) * box['ple'][:, halves[i]]
        return run
    return [ple] + [mix(i) for i in range(len(halves))] + [gate(i) for i in range(len(halves))]


def _chunk_stages(z_ref, rows, cos, sin_signed, ln_g_ref, gn_ref, bias_ref, wpair_ref, tabs_ref,
                  state_update, state_output, mixin_ref, vrows_ref):
    box = {}
    zs = lambda off, i, width: z_ref[rows, off + i * width:off + (i + 1) * width]

    def gating_in():
        sv = _gelu(z_ref[rows, O_SV:O_SV + SGU_WIDTH])
        mu = jnp.mean(sv, axis=-1, keepdims=True)
        cen = sv - mu
        var = jnp.mean(cen * cen, axis=-1, keepdims=True)
        vn = cen * lax.rsqrt(var + LN_EPS) * ln_g_ref[...]
        if vrows_ref is not None:
            vrows_ref[rows, :] = vn
        low_lanes = lax.broadcasted_iota(jnp.int32, (CHUNK, LANES), 1) < SGU_GROUP_DIM
        for m in range(SGU_GROUPS // 2):
            blk = vn[:, m * LANES:(m + 1) * LANES]
            rhs = jnp.concatenate([jnp.where(low_lanes, blk, 0.0), jnp.where(low_lanes, 0.0, blk)],
                                  axis=0).astype(BF16)
            box['mixed%d' % m] = jnp.dot(wpair_ref[m], rhs, preferred_element_type=F32)

    def retention_in():
        c, s = cos(), sin_signed()
        for h in range(RET_HEADS):
            q = _rotate(zs(O_Q, h, RET_DK), c, s)
            k = _rotate(zs(O_K, h, RET_DK), c, s)
            v = zs(O_V, h, RET_DV)
            k_b = k.astype(BF16)
            box['scores%d' % h] = lax.dot_general(q.astype(BF16), k_b, (((1,), (1,)), ((), ())),
                                                  preferred_element_type=F32)
            box['qw%d' % h] = (q * tabs_ref[1, h]).astype(BF16)
            box['v%d' % h] = v.astype(BF16)
            box['u%d' % h] = state_update(h, k_b, (v * tabs_ref[2, h]).astype(BF16))

    def gating_out_retention_mid():
        for m in range(SGU_GROUPS // 2):
            cols = slice(m * LANES, (m + 1) * LANES)
            mixed = box['mixed%d' % m] + bias_ref[:, cols]
            su = _gelu(zs(O_SU, m, LANES))
            mixin_ref[rows, cols] = (_silu(zs(O_SG, m, LANES)) * (su * mixed)).astype(BF16)
        for h in range(RET_HEADS):
            scores_b = (box['scores%d' % h] * tabs_ref[0, h]).astype(BF16)
            box['o%d' % h] = state_output(h, scores_b, box['qw%d' % h], box['v%d' % h], box['u%d' % h])

    def retention_out():
        for h in range(RET_HEADS):
            o = box['o%d' % h]
            mu = jnp.mean(o, axis=-1, keepdims=True)
            cen = o - mu
            var = jnp.mean(cen * cen, axis=-1, keepdims=True)
            on = cen * lax.rsqrt(var + LN_EPS) * gn_ref[:, h * RET_DV:(h + 1) * RET_DV]
            mixin_ref[rows, SGU_WIDTH + h * RET_DV:SGU_WIDTH + (h + 1) * RET_DV] = (
                _silu(zs(O_RG, h, RET_DV)) * on).astype(BF16)

    return [gating_in, retention_in, gating_out_retention_mid, retention_out]


def _interleave(pieces, stages):
    slots = max(len(stages), 1)
    done = 0
    for i in range(slots):
        upto = -(-(i + 1) * len(pieces) // slots)
        for piece in pieces[done:upto]:
            piece()
        done = upto
        if i < len(stages):
            stages[i]()


def _emit_blocks(in_pieces, out_pieces, stages, blocks, first_in=None, next_in=None):
    _interleave(first_in or [], [])
    for i, blk in enumerate(blocks):
        nxt = in_pieces(blocks[i + 1]) if i + 1 < len(blocks) else (next_in or [])
        if i > 0:
            ple, mix0, mix1, gate0, gate1 = out_pieces(blocks[i - 1])
            pieces = [ple, mix0, mix1] + nxt[:3] + [gate0] + nxt[3:4] + [gate1] + nxt[4:]
        else:
            pieces = nxt
        _interleave(pieces, stages(blk))
    _interleave(out_pieces(blocks[-1]), [])


def _init_tables(seg, sgu_w_ref, sgu_b_ref, mask_ref, wpm_ref, bias_ref):
    rows = lax.broadcasted_iota(jnp.int32, (CHUNK, LANES), 0)
    lanes = lax.broadcasted_iota(jnp.int32, (CHUNK, LANES), 1)
    if seg == CHUNK:
        group = lambda g: sgu_w_ref[g]
        bias_rows = sgu_b_ref[...]
    else:
        reps = CHUNK // seg
        select = jnp.where((rows < seg) & (lanes % seg == rows), 1.0, 0.0).astype(BF16)
        first = lambda a: jnp.where(lanes[:seg] < seg, a, 0.0)

        def group(g):
            stacked = jnp.concatenate([first(sgu_w_ref[g, 0:seg, :])] * reps, axis=0)
            return jnp.dot(stacked.astype(BF16), select, preferred_element_type=F32)
        b_first = jnp.where(lax.broadcasted_iota(jnp.int32, (SGU_GROUPS, LANES), 1) < seg, sgu_b_ref[...], 0.0)
        bias_rows = sum(pltpu.roll(b_first, r * seg, 1) for r in range(reps))
    for m in range(SGU_GROUPS // 2):
        pair = jnp.concatenate([group(2 * m), group(2 * m + 1)], axis=1)
        wpm_ref[m] = jnp.where(mask_ref[...] > 0.0, pair, 0.0).astype(BF16)
    group_of_lane = lax.broadcasted_iota(jnp.int32, (SGU_GROUPS, SGU_WIDTH), 1) // SGU_GROUP_DIM
    expand = jnp.where(group_of_lane == lax.broadcasted_iota(jnp.int32, (SGU_GROUPS, SGU_WIDTH), 0),
                       1.0, 0.0).astype(BF16)
    total, rest = jnp.zeros((CHUNK, SGU_WIDTH), F32), bias_rows
    for _ in range(3):
        term = rest.astype(BF16)
        total = total + lax.dot_general(term, expand, (((0,), (0,)), ((), ())), preferred_element_type=F32)
        rest = rest - term.astype(F32)
    bias_ref[...] = total


def _prompt_kernel(g_chunk, n_tiles, x_ref, x_next_ref, p_ref, cos_ref, sin_ref, w_in_ref, w_out_ref, w_gate_ref,
                   w_ple_ref, norm_pre_ref, norm_post_ref, ln_g_ref, gn_ref, sgu_w_ref, sgu_b_ref, mask_ref,
                   tabs_ref, y_ref, state_ref, w_in_b_ref, w_out_b_ref, w_gate_b_ref, w_ple_b_ref,
                   z_ref, z_first_ref, mixin_ref, wpm_ref, bias_ref):
    step = pl.program_id(0)
    first = slice(0, BLOCK_ROWS["prompt"])

    def next_first_in():
        return _in_pieces(x_next_ref[0], norm_pre_ref, w_in_b_ref, z_first_ref)

    @pl.when(step == 0)
    def _():
        _init_tables(CHUNK, sgu_w_ref, sgu_b_ref, mask_ref, wpm_ref, bias_ref)

    @pl.when(step < CAST_STEPS)
    def _():
        for src, dst in ((w_in_ref, w_in_b_ref), (w_out_ref, w_out_b_ref), (w_gate_ref, w_gate_b_ref),
                         (w_ple_ref, w_ple_b_ref)):
            n = src.shape[0]
            dst[pl.ds(pl.multiple_of(step * n, n), n), :] = src[...].astype(BF16)

    @pl.when(step == CAST_STEPS - 1)
    def _():
        _interleave(next_first_in(), [])

    @pl.when(step >= CAST_STEPS)
    def _():
        @pl.when((step - CAST_STEPS) % n_tiles == 0)
        def _():
            state_ref[...] = jnp.zeros_like(state_ref)

        def state_update(h, k_b, vw_b):
            return lax.dot_general(k_b, vw_b, (((0,), (0,)), ((), ())), preferred_element_type=F32)

        def state_output(h, scores_b, qw_b, v_b, u):
            s = state_ref[0, h]
            o = jnp.dot(jnp.concatenate([scores_b, qw_b], axis=1),
                        jnp.concatenate([v_b, s.astype(BF16)], axis=0), preferred_element_type=F32)
            state_ref[0, h] = g_chunk[h] * s + u
            return o

        def in_pieces(blk):
            return _in_pieces(x_ref[0, blk, :], norm_pre_ref, w_in_b_ref, z_ref.at[blk, :])

        def out_pieces(blk):
            return _out_pieces(x_ref[0, blk, :], p_ref[0, blk, :], mixin_ref.at[blk, :], w_out_b_ref,
                               norm_post_ref, w_gate_b_ref, w_ple_b_ref, y_ref.at[0, blk, :])

        def stages(blk):
            out = []
            for r in range(blk.start, blk.stop, CHUNK):
                rows = slice(r, r + CHUNK)
                out += _chunk_stages(z_first_ref if blk == first else z_ref, rows,
                                     lambda rows=rows: cos_ref[rows, :],
                                     lambda rows=rows: sin_ref[rows, :], ln_g_ref, gn_ref, bias_ref, wpm_ref,
                                     tabs_ref, state_update, state_output, mixin_ref, None)
            return out

        rows = BLOCK_ROWS["prompt"]
        _emit_blocks(in_pieces, out_pieces, stages, [slice(r, r + rows) for r in range(0, PROMPT_TILE, rows)],
                     next_in=next_first_in())


def _sample_kernel(g_seq, x_ref, x_next_ref, p_ref, cos_ref, sin_ref, st_in_ref, w_in_ref, w_out_ref, w_gate_ref,
                   w_ple_ref, norm_pre_ref, norm_post_ref, ln_g_ref, gn_ref, sgu_w_ref, sgu_b_ref, mask_ref,
                   tabs_ref, y_ref, st_out_ref, vrows_ref, z_ref, z_first_ref, mixin_ref, wpm_ref, bias_ref):
    seq_len = cos_ref.shape[0]
    seqs = CHUNK // seq_len
    first = slice(0, BLOCK_ROWS["sample"])

    @pl.when(pl.program_id(0) == 0)
    def _():
        _init_tables(seq_len, sgu_w_ref, sgu_b_ref, mask_ref, wpm_ref, bias_ref)
        _interleave(_in_pieces(x_ref[first, :], norm_pre_ref, w_in_ref, z_first_ref), [])

    pairs = seqs // 2
    per_pair = lambda a: a.reshape(pairs, 2 * seq_len, a.shape[-1])
    first_of_pair = lax.broadcasted_iota(jnp.int32, (1, 2 * seq_len, 1), 1) < seq_len
    tile_rows = lambda ref: jnp.concatenate([ref[...]] * seqs, axis=0)

    def stages(blk):
        pq = slice(blk.start // (2 * seq_len), blk.stop // (2 * seq_len))

        def state_update(h, k_b, vw_b):
            vw = per_pair(vw_b)
            zero = jnp.zeros_like(vw)
            both = jnp.concatenate([jnp.where(first_of_pair, vw, zero), jnp.where(first_of_pair, zero, vw)],
                                   axis=-1)
            u = jnp.einsum('pjd,pje->pde', per_pair(k_b), both, preferred_element_type=F32)
            st_out_ref[pq, 0, h] = g_seq[h] * st_in_ref[pq, 0, h] + u[:, :, :RET_DV]
            st_out_ref[pq, 1, h] = g_seq[h] * st_in_ref[pq, 1, h] + u[:, :, RET_DV:]
            return None

        def state_output(h, scores_b, qw_b, v_b, _):
            o = jnp.dot(scores_b, v_b, preferred_element_type=F32)
            s_pair = jnp.concatenate([st_in_ref[pq, 0, h], st_in_ref[pq, 1, h]], axis=-1).astype(BF16)
            both = jnp.einsum('pid,pde->pie', per_pair(qw_b), s_pair, preferred_element_type=F32)
            o_inter = jnp.where(first_of_pair, both[:, :, :RET_DV], both[:, :, RET_DV:])
            return o + o_inter.reshape(CHUNK, RET_DV)

        return _chunk_stages(z_first_ref if blk == first else z_ref, blk, lambda: tile_rows(cos_ref),
                             lambda: tile_rows(sin_ref), ln_g_ref,
                             gn_ref, bias_ref, wpm_ref, tabs_ref, state_update, state_output, mixin_ref,
                             vrows_ref)

    def in_pieces(blk):
        return _in_pieces(x_ref[blk, :], norm_pre_ref, w_in_ref, z_ref.at[blk, :])

    def out_pieces(blk):
        return _out_pieces(x_ref[blk, :], p_ref[blk, :], mixin_ref.at[blk, :], w_out_ref, norm_post_ref,
                           w_gate_ref, w_ple_ref, y_ref.at[blk, :])

    rows = BLOCK_ROWS["sample"]
    _emit_blocks(in_pieces, out_pieces, stages, [slice(r, r + rows) for r in range(0, SAMPLE_TILE, rows)],
                 next_in=_in_pieces(x_next_ref[...], norm_pre_ref, w_in_ref, z_first_ref))


def _const_spec(shape):
    return pl.BlockSpec(shape, lambda *_: (0,) * len(shape), pipeline_mode=pl.Buffered(1))


def kernel(x_prompt, x_sample, state_ret, p_prompt, p_sample, w_in, w_out, norm_pre, norm_post, sgu_w, sgu_b,
           sgu_ln, ret_gn, w_ple_proj, w_ple_gate):
    batch, seq, _ = x_prompt.shape
    dec_batch, dec_seq, _ = x_sample.shape
    n_tiles = seq // PROMPT_TILE
    n_tok = dec_batch * dec_seq
    assert seq % PROMPT_TILE == 0 and PROMPT_TILE % BLOCK_ROWS["prompt"] == 0 and w_in.shape[0] == 1
    assert CHUNK % dec_seq == 0 and n_tok % SAMPLE_TILE == 0 and SAMPLE_TILE % BLOCK_ROWS["sample"] == 0
    assert D_MODEL % CAST_STEPS == 0 and PLE_DIM % (16 * CAST_STEPS) == 0

    small = (norm_pre[0][None, :], norm_post[0][None, :], sgu_ln[0][None, :], ret_gn[0][None, :],
             sgu_w[0], sgu_b[0])
    small_specs = [_const_spec((1, D_MODEL)), _const_spec((1, D_MODEL)), _const_spec((1, SGU_WIDTH)),
                   _const_spec((1, RET_HEADS * RET_DV)), _const_spec((SGU_GROUPS, CHUNK, CHUNK)),
                   _const_spec((SGU_GROUPS, CHUNK)), _const_spec((CHUNK, 2 * CHUNK)),
                   _const_spec((3, RET_HEADS, CHUNK, LANES))]
    table_scratch = [pltpu.VMEM((SGU_GROUPS // 2, CHUNK, 2 * CHUNK), BF16), pltpu.VMEM((CHUNK, SGU_WIDTH), F32)]
    weight_shapes = [(D_MODEL, IN_WIDTH), (D_MODEL, D_MODEL), (D_MODEL, D_MODEL), (PLE_DIM, D_MODEL)]

    tabs_p, g_chunk = _retention_tables(CHUNK)
    cos_p, sin_p = _rotary_tables(np.arange(seq))
    tile_of = lambda s: jnp.maximum(s - CAST_STEPS, 0)
    tile_spec = lambda width: pl.BlockSpec((1, PROMPT_TILE, width),
                                           lambda s: (tile_of(s) // n_tiles, tile_of(s) % n_tiles, 0))
    pos_spec = pl.BlockSpec((PROMPT_TILE, LANES), lambda s: (tile_of(s) % n_tiles, 0))
    blocks_per_tile = PROMPT_TILE // BLOCK_ROWS["prompt"]
    next_tile = lambda s: jnp.minimum(jnp.maximum(s - CAST_STEPS + 1, 0), batch * n_tiles - 1)
    next_spec = pl.BlockSpec((1, BLOCK_ROWS["prompt"], D_MODEL),
                             lambda s: (next_tile(s) // n_tiles, (next_tile(s) % n_tiles) * blocks_per_tile, 0))
    cast_spec = lambda shape: pl.BlockSpec((shape[0] // CAST_STEPS, shape[1]),
                                           lambda s: (jnp.minimum(s, CAST_STEPS - 1), 0))
    y_prompt, st_prompt, w_in_b, w_out_b, w_gate_b, w_ple_b = pl.pallas_call(
        functools.partial(_prompt_kernel, g_chunk, n_tiles),
        grid=(CAST_STEPS + batch * n_tiles,),
        in_specs=[tile_spec(D_MODEL), next_spec, tile_spec(PLE_DIM), pos_spec, pos_spec]
        + [cast_spec(shape) for shape in weight_shapes] + small_specs,
        out_specs=[tile_spec(D_MODEL),
                   pl.BlockSpec((1, RET_HEADS, RET_DK, RET_DV), lambda s: (tile_of(s) // n_tiles, 0, 0, 0))]
        + [pl.BlockSpec(shape, lambda s: (0, 0)) for shape in weight_shapes],
        out_shape=[jax.ShapeDtypeStruct((batch, seq, D_MODEL), F32),
                   jax.ShapeDtypeStruct((batch, RET_HEADS, RET_DK, RET_DV), F32)]
        + [jax.ShapeDtypeStruct(shape, BF16) for shape in weight_shapes],
        scratch_shapes=[pltpu.VMEM((PROMPT_TILE, IN_WIDTH), F32), pltpu.VMEM((BLOCK_ROWS["prompt"], IN_WIDTH), F32),
                        pltpu.VMEM((PROMPT_TILE, D_MODEL), BF16)] + table_scratch,
        compiler_params=pltpu.CompilerParams(dimension_semantics=("arbitrary",),
                                             vmem_limit_bytes=VMEM_LIMIT_BYTES),
        name="prompt_layer",
    )(x_prompt, x_prompt, p_prompt[0], cos_p, sin_p, w_in[0], w_out[0], w_ple_gate[0], w_ple_proj[0], *small,
      _mix_mask(CHUNK), tabs_p)

    tabs_s, g_seq = _retention_tables(dec_seq)
    cos_s, sin_s = _rotary_tables(PAST_LEN + np.arange(dec_seq))
    tile_seqs = SAMPLE_TILE // dec_seq
    tok_spec = lambda width: pl.BlockSpec((SAMPLE_TILE, width), lambda i: (i, 0))
    sample_steps = n_tok // SAMPLE_TILE
    next_tok_spec = pl.BlockSpec(
        (BLOCK_ROWS["sample"], D_MODEL),
        lambda i: (jnp.minimum(i + 1, sample_steps - 1) * (SAMPLE_TILE // BLOCK_ROWS["sample"]), 0))
    state_spec = pl.BlockSpec((tile_seqs // 2, 2, RET_HEADS, RET_DK, RET_DV), lambda i: (i, 0, 0, 0, 0))
    paired = (dec_batch // 2, 2, RET_HEADS, RET_DK, RET_DV)
    y_sample, st_sample, v_sample = pl.pallas_call(
        functools.partial(_sample_kernel, g_seq),
        grid=(sample_steps,),
        in_specs=[tok_spec(D_MODEL), next_tok_spec, tok_spec(PLE_DIM), _const_spec((dec_seq, LANES)),
                  _const_spec((dec_seq, LANES)), state_spec]
        + [_const_spec(shape) for shape in weight_shapes] + small_specs,
        out_specs=[tok_spec(D_MODEL), state_spec, tok_spec(SGU_WIDTH)],
        out_shape=[jax.ShapeDtypeStruct((n_tok, D_MODEL), F32),
                   jax.ShapeDtypeStruct(paired, F32), jax.ShapeDtypeStruct((n_tok, SGU_WIDTH), F32)],
        scratch_shapes=[pltpu.VMEM((SAMPLE_TILE, IN_WIDTH), F32), pltpu.VMEM((BLOCK_ROWS["sample"], IN_WIDTH), F32),
                        pltpu.VMEM((SAMPLE_TILE, D_MODEL), BF16)] + table_scratch,
        compiler_params=pltpu.CompilerParams(dimension_semantics=("arbitrary",),
                                             vmem_limit_bytes=VMEM_LIMIT_BYTES),
        name="sample_layer",
    )(x_sample.reshape(n_tok, D_MODEL), x_sample.reshape(n_tok, D_MODEL), p_sample[0].reshape(n_tok, PLE_DIM), cos_s, sin_s,
      state_ret[0].reshape(paired),
      w_in_b, w_out_b, w_gate_b, w_ple_b, *small, _mix_mask(dec_seq), tabs_s)

    return (y_prompt, y_sample.reshape(dec_batch, dec_seq, D_MODEL), st_prompt[None],
            st_sample.reshape(state_ret.shape),
            v_sample.reshape(1, dec_batch, dec_seq, SGU_WIDTH))
```

```python
import functools

import numpy as np
import jax
import jax.numpy as jnp
from jax import lax
from jax.experimental import pallas as pl
from jax.experimental.pallas import tpu as pltpu

F32 = jnp.float32
BF16 = jnp.bfloat16

D_MODEL = 1024
PAST_LEN = 16384
SGU_WIDTH = 512
SGU_GROUPS = 8
SGU_GROUP_DIM = SGU_WIDTH // SGU_GROUPS
RET_HEADS = 4
RET_DK = 128
RET_DV = 128
CHUNK = 128
ROPE_THETA = 10000.0
PLE_DIM = 256
RMS_EPS = 1e-6
LN_EPS = 1e-5
IN_WIDTH = 3 * SGU_WIDTH + RET_HEADS * (2 * RET_DK + 2 * RET_DV)
O_SU, O_SV, O_SG = 0, SGU_WIDTH, 2 * SGU_WIDTH
O_Q = 3 * SGU_WIDTH
O_K = O_Q + RET_HEADS * RET_DK
O_V = O_K + RET_HEADS * RET_DK
O_RG = O_V + RET_HEADS * RET_DV

LANES = 128
VMEM_LIMIT_BYTES = 60 * 1024 * 1024

PROMPT_TILE = 1024
BLOCK_ROWS = {"prompt": 256, "sample": CHUNK}
PIECE_COLS = 512
CAST_STEPS = 8
SAMPLE_TILE = 2 * CHUNK


def _log_gamma():
    return np.log(1.0 - 2.0 ** (-5.0 - np.arange(RET_HEADS, dtype=np.float64)))


def _retention_tables(seg):
    lg = _log_gamma()[:, None, None]
    r = np.arange(CHUNK)
    i, j = r[:, None] % seg, r[None, :] % seg
    same = (r[:, None] // seg) == (r[None, :] // seg)
    decay = np.where(same & (i >= j), np.exp(lg * np.maximum(i - j, 0)), 0.0)
    wq = np.broadcast_to(np.exp(lg * (i + 1.0)), (RET_HEADS, CHUNK, LANES))
    wkv = np.broadcast_to(np.exp(lg * (seg - 1.0 - i)), (RET_HEADS, CHUNK, LANES))
    scale = RET_DK ** -0.5
    tabs = np.stack([decay * scale, wq, wkv * scale]).astype(np.float32)
    g_seg = [float(v) for v in np.exp(_log_gamma() * seg)]
    return tabs, g_seg


def _rotary_tables(pos):
    half = RET_DK // 2
    inv = ROPE_THETA ** (-np.arange(half, dtype=np.float64) / half)
    ang = pos.astype(np.float64)[:, None] * inv[None, :]
    cos, sin = np.cos(ang), np.sin(ang)
    return (np.concatenate([cos, cos], axis=1).astype(np.float32),
            np.concatenate([-sin, sin], axis=1).astype(np.float32))


def _mix_mask(seg):
    r = np.arange(CHUNK)
    same = (r[:, None] // seg) == (r[None, :] // seg)
    m = same & ((r[None, :] % seg) <= (r[:, None] % seg))
    return np.concatenate([m, m], axis=1).astype(np.float32)


def _gelu(x):
    c = float(np.sqrt(2.0 / np.pi))
    half = 0.5 * x
    return half + half * jnp.tanh(x * (c + (c * 0.044715) * (x * x)))


def _silu(x):
    return x * jax.nn.sigmoid(x)


def _rotate(x, cos, sin_signed):
    return x * cos + pltpu.roll(x, RET_DK // 2, 1) * sin_signed


def _normed_bf16(x, norm_ref):
    ms = jnp.mean(x * x, axis=-1, keepdims=True)
    return (x * lax.rsqrt(ms + RMS_EPS) * norm_ref[...]).astype(BF16)


def _in_pieces(x, norm_pre_ref, w_in_ref, z_ref):
    box = {}

    def piece(c0):
        def run():
            if 'h' not in box:
                box['h'] = _normed_bf16(x, norm_pre_ref)
            z_ref[:, c0:c0 + PIECE_COLS] = jnp.dot(box['h'], w_in_ref[:, c0:c0 + PIECE_COLS],
                                                   preferred_element_type=F32)
        return run
    return [piece(c0) for c0 in range(0, IN_WIDTH, PIECE_COLS)]


def _out_pieces(x, p, mixin_ref, w_out_ref, norm_post_ref, w_gate_ref, w_ple_ref, y_ref):
    box = {}
    halves = [slice(c0, c0 + PIECE_COLS) for c0 in range(0, D_MODEL, PIECE_COLS)]

    def ple():
        box['ple'] = jnp.dot(p.astype(BF16), w_ple_ref[...], preferred_element_type=F32)

    def mix(i):
        def run():
            box['mix%d' % i] = jnp.dot(mixin_ref[...], w_out_ref[:, halves[i]], preferred_element_type=F32)
        return run

    def gate(i):
        def run():
            if 'x1' not in box:
                mixes = [box['mix%d' % j] for j in range(len(halves))]
                ms = sum(jnp.sum(m * m, axis=-1, keepdims=True) for m in mixes) * (1.0 / D_MODEL)
                scale = lax.rsqrt(ms + RMS_EPS)
                box['x1'] = [x[:, halves[j]] + mixes[j] * scale * norm_post_ref[:, halves[j]]
                             for j in range(len(halves))]
                box['x1b'] = jnp.concatenate(box['x1'], axis=1).astype(BF16)
            g = jnp.dot(box['x1b'], w_gate_ref[:, halves[i]], preferred_element_type=F32)
            y_ref[:, halves[i]] = box['x1'][i] + jax.nn.sigmoid(g) * box['ple'][:, halves[i]]
        return run
    return [ple] + [mix(i) for i in range(len(halves))] + [gate(i) for i in range(len(halves))]


def _chunk_stages(z_ref, rows, cos, sin_signed, ln_g_ref, gn_ref, bias_ref, wpair_ref, tabs_ref,
                  state_update, state_output, mixin_ref, vrows_ref):
    box = {}
    zs = lambda off, i, width: z_ref[rows, off + i * width:off + (i + 1) * width]

    def gating_in():
        sv = _gelu(z_ref[rows, O_SV:O_SV + SGU_WIDTH])
        mu = jnp.mean(sv, axis=-1, keepdims=True)
        cen = sv - mu
        var = jnp.mean(cen * cen, axis=-1, keepdims=True)
        vn = cen * lax.rsqrt(var + LN_EPS) * ln_g_ref[...]
        if vrows_ref is not None:
            vrows_ref[rows, :] = vn
        low_lanes = lax.broadcasted_iota(jnp.int32, (CHUNK, LANES), 1) < SGU_GROUP_DIM
        for m in range(SGU_GROUPS // 2):
            blk = vn[:, m * LANES:(m + 1) * LANES]
            rhs = jnp.concatenate([jnp.where(low_lanes, blk, 0.0), jnp.where(low_lanes, 0.0, blk)],
                                  axis=0).astype(BF16)
            box['mixed%d' % m] = jnp.dot(wpair_ref[m], rhs, preferred_element_type=F32)

    def retention_in():
        c, s = cos(), sin_signed()
        for h in range(RET_HEADS):
            q = _rotate(zs(O_Q, h, RET_DK), c, s)
            k = _rotate(zs(O_K, h, RET_DK), c, s)
            v = zs(O_V, h, RET_DV)
            k_b = k.astype(BF16)
            box['scores%d' % h] = lax.dot_general(q.astype(BF16), k_b, (((1,), (1,)), ((), ())),
                                                  preferred_element_type=F32)
            box['qw%d' % h] = (q * tabs_ref[1, h]).astype(BF16)
            box['v%d' % h] = v.astype(BF16)
            box['u%d' % h] = state_update(h, k_b, (v * tabs_ref[2, h]).astype(BF16))

    def gating_out_retention_mid():
        for m in range(SGU_GROUPS // 2):
            cols = slice(m * LANES, (m + 1) * LANES)
            mixed = box['mixed%d' % m] + bias_ref[:, cols]
            su = _gelu(zs(O_SU, m, LANES))
            mixin_ref[rows, cols] = (_silu(zs(O_SG, m, LANES)) * (su * mixed)).astype(BF16)
        for h in range(RET_HEADS):
            scores_b = (box['scores%d' % h] * tabs_ref[0, h]).astype(BF16)
            box['o%d' % h] = state_output(h, scores_b, box['qw%d' % h], box['v%d' % h], box['u%d' % h])

    def retention_out():
        for h in range(RET_HEADS):
            o = box['o%d' % h]
            mu = jnp.mean(o, axis=-1, keepdims=True)
            cen = o - mu
            var = jnp.mean(cen * cen, axis=-1, keepdims=True)
            on = cen * lax.rsqrt(var + LN_EPS) * gn_ref[:, h * RET_DV:(h + 1) * RET_DV]
            mixin_ref[rows, SGU_WIDTH + h * RET_DV:SGU_WIDTH + (h + 1) * RET_DV] = (
                _silu(zs(O_RG, h, RET_DV)) * on).astype(BF16)

    return [gating_in, retention_in, gating_out_retention_mid, retention_out]


def _interleave(pieces, stages):
    slots = max(len(stages), 1)
    done = 0
    for i in range(slots):
        upto = -(-(i + 1) * len(pieces) // slots)
        for piece in pieces[done:upto]:
            piece()
        done = upto
        if i < len(stages):
            stages[i]()


def _emit_blocks(in_pieces, out_pieces, stages, blocks, first_in=None, next_in=None):
    _interleave(first_in or [], [])
    for i, blk in enumerate(blocks):
        nxt = in_pieces(blocks[i + 1]) if i + 1 < len(blocks) else (next_in or [])
        if i > 0:
            ple, mix0, mix1, gate0, gate1 = out_pieces(blocks[i - 1])
            pieces = [ple, mix0, mix1] + nxt[:3] + [gate0] + nxt[3:4] + [gate1] + nxt[4:]
        else:
            pieces = nxt
        _interleave(pieces, stages(blk))
    _interleave(out_pieces(blocks[-1]), [])


def _init_tables(seg, sgu_w_ref, sgu_b_ref, mask_ref, wpm_ref, bias_ref):
    rows = lax.broadcasted_iota(jnp.int32, (CHUNK, LANES), 0)
    lanes = lax.broadcasted_iota(jnp.int32, (CHUNK, LANES), 1)
    if seg == CHUNK:
        group = lambda g: sgu_w_ref[g]
        bias_rows = sgu_b_ref[...]
    else:
        reps = CHUNK // seg
        select = jnp.where((rows < seg) & (lanes % seg == rows), 1.0, 0.0).astype(BF16)
        first = lambda a: jnp.where(lanes[:seg] < seg, a, 0.0)

        def group(g):
            stacked = jnp.concatenate([first(sgu_w_ref[g, 0:seg, :])] * reps, axis=0)
            return jnp.dot(stacked.astype(BF16), select, preferred_element_type=F32)
        b_first = jnp.where(lax.broadcasted_iota(jnp.int32, (SGU_GROUPS, LANES), 1) < seg, sgu_b_ref[...], 0.0)
        bias_rows = sum(pltpu.roll(b_first, r * seg, 1) for r in range(reps))
    for m in range(SGU_GROUPS // 2):
        pair = jnp.concatenate([group(2 * m), group(2 * m + 1)], axis=1)
        wpm_ref[m] = jnp.where(mask_ref[...] > 0.0, pair, 0.0).astype(BF16)
    group_of_lane = lax.broadcasted_iota(jnp.int32, (SGU_GROUPS, SGU_WIDTH), 1) // SGU_GROUP_DIM
    expand = jnp.where(group_of_lane == lax.broadcasted_iota(jnp.int32, (SGU_GROUPS, SGU_WIDTH), 0),
                       1.0, 0.0).astype(BF16)
    total, rest = jnp.zeros((CHUNK, SGU_WIDTH), F32), bias_rows
    for _ in range(3):
        term = rest.astype(BF16)
        total = total + lax.dot_general(term, expand, (((0,), (0,)), ((), ())), preferred_element_type=F32)
        rest = rest - term.astype(F32)
    bias_ref[...] = total


def _prompt_kernel(g_chunk, n_tiles, x_ref, x_next_ref, p_ref, cos_ref, sin_ref, w_in_ref, w_out_ref, w_gate_ref,
                   w_ple_ref, norm_pre_ref, norm_post_ref, ln_g_ref, gn_ref, sgu_w_ref, sgu_b_ref, mask_ref,
                   tabs_ref, y_ref, state_ref, w_in_b_ref, w_out_b_ref, w_gate_b_ref, w_ple_b_ref,
                   z_ref, z_first_ref, mixin_ref, wpm_ref, bias_ref):
    step = pl.program_id(0)
    first = slice(0, BLOCK_ROWS["prompt"])

    def next_first_in():
        return _in_pieces(x_next_ref[0], norm_pre_ref, w_in_b_ref, z_first_ref)

    @pl.when(step == 0)
    def _():
        _init_tables(CHUNK, sgu_w_ref, sgu_b_ref, mask_ref, wpm_ref, bias_ref)

    @pl.when(step < CAST_STEPS)
    def _():
        for src, dst in ((w_in_ref, w_in_b_ref), (w_out_ref, w_out_b_ref), (w_gate_ref, w_gate_b_ref),
                         (w_ple_ref, w_ple_b_ref)):
            n = src.shape[0]
            dst[pl.ds(pl.multiple_of(step * n, n), n), :] = src[...].astype(BF16)

    @pl.when(step == CAST_STEPS - 1)
    def _():
        _interleave(next_first_in(), [])

    @pl.when(step >= CAST_STEPS)
    def _():
        @pl.when((step - CAST_STEPS) % n_tiles == 0)
        def _():
            state_ref[...] = jnp.zeros_like(state_ref)

        def state_update(h, k_b, vw_b):
            return lax.dot_general(k_b, vw_b, (((0,), (0,)), ((), ())), preferred_element_type=F32)

        def state_output(h, scores_b, qw_b, v_b, u):
            s = state_ref[0, h]
            o = jnp.dot(jnp.concatenate([scores_b, qw_b], axis=1),
                        jnp.concatenate([v_b, s.astype(BF16)], axis=0), preferred_element_type=F32)
            state_ref[0, h] = g_chunk[h] * s + u
            return o

        def in_pieces(blk):
            return _in_pieces(x_ref[0, blk, :], norm_pre_ref, w_in_b_ref, z_ref.at[blk, :])

        def out_pieces(blk):
            return _out_pieces(x_ref[0, blk, :], p_ref[0, blk, :], mixin_ref.at[blk, :], w_out_b_ref,
                               norm_post_ref, w_gate_b_ref, w_ple_b_ref, y_ref.at[0, blk, :])

        def stages(blk):
            out = []
            for r in range(blk.start, blk.stop, CHUNK):
                rows = slice(r, r + CHUNK)
                out += _chunk_stages(z_first_ref if blk == first else z_ref, rows,
                                     lambda rows=rows: cos_ref[rows, :],
                                     lambda rows=rows: sin_ref[rows, :], ln_g_ref, gn_ref, bias_ref, wpm_ref,
                                     tabs_ref, state_update, state_output, mixin_ref, None)
            return out

        rows = BLOCK_ROWS["prompt"]
        _emit_blocks(in_pieces, out_pieces, stages, [slice(r, r + rows) for r in range(0, PROMPT_TILE, rows)],
                     next_in=next_first_in())


def _sample_kernel(g_seq, x_ref, p_ref, cos_ref, sin_ref, st_in_ref, w_in_ref, w_out_ref, w_gate_ref,
                   w_ple_ref, norm_pre_ref, norm_post_ref, ln_g_ref, gn_ref, sgu_w_ref, sgu_b_ref, mask_ref,
                   tabs_ref, y_ref, st_out_ref, vrows_ref, z_ref, mixin_ref, wpm_ref, bias_ref):
    seq_len = cos_ref.shape[0]
    seqs = CHUNK // seq_len

    @pl.when(pl.program_id(0) == 0)
    def _():
        _init_tables(seq_len, sgu_w_ref, sgu_b_ref, mask_ref, wpm_ref, bias_ref)

    pairs = seqs // 2
    per_pair = lambda a: a.reshape(pairs, 2 * seq_len, a.shape[-1])
    first_of_pair = lax.broadcasted_iota(jnp.int32, (1, 2 * seq_len, 1), 1) < seq_len
    tile_rows = lambda ref: jnp.concatenate([ref[...]] * seqs, axis=0)

    def stages(blk):
        pq = slice(blk.start // (2 * seq_len), blk.stop // (2 * seq_len))

        def state_update(h, k_b, vw_b):
            vw = per_pair(vw_b)
            zero = jnp.zeros_like(vw)
            both = jnp.concatenate([jnp.where(first_of_pair, vw, zero), jnp.where(first_of_pair, zero, vw)],
                                   axis=-1)
            u = jnp.einsum('pjd,pje->pde', per_pair(k_b), both, preferred_element_type=F32)
            st_out_ref[pq, 0, h] = g_seq[h] * st_in_ref[pq, 0, h] + u[:, :, :RET_DV]
            st_out_ref[pq, 1, h] = g_seq[h] * st_in_ref[pq, 1, h] + u[:, :, RET_DV:]
            return None

        def state_output(h, scores_b, qw_b, v_b, _):
            o = jnp.dot(scores_b, v_b, preferred_element_type=F32)
            s_pair = jnp.concatenate([st_in_ref[pq, 0, h], st_in_ref[pq, 1, h]], axis=-1).astype(BF16)
            both = jnp.einsum('pid,pde->pie', per_pair(qw_b), s_pair, preferred_element_type=F32)
            o_inter = jnp.where(first_of_pair, both[:, :, :RET_DV], both[:, :, RET_DV:])
            return o + o_inter.reshape(CHUNK, RET_DV)

        return _chunk_stages(z_ref, blk, lambda: tile_rows(cos_ref), lambda: tile_rows(sin_ref), ln_g_ref,
                             gn_ref, bias_ref, wpm_ref, tabs_ref, state_update, state_output, mixin_ref,
                             vrows_ref)

    def in_pieces(blk):
        return _in_pieces(x_ref[blk, :], norm_pre_ref, w_in_ref, z_ref.at[blk, :])

    def out_pieces(blk):
        return _out_pieces(x_ref[blk, :], p_ref[blk, :], mixin_ref.at[blk, :], w_out_ref, norm_post_ref,
                           w_gate_ref, w_ple_ref, y_ref.at[blk, :])

    rows = BLOCK_ROWS["sample"]
    _emit_blocks(in_pieces, out_pieces, stages, [slice(r, r + rows) for r in range(0, SAMPLE_TILE, rows)],
                 first_in=in_pieces(slice(0, rows)))


def _const_spec(shape):
    return pl.BlockSpec(shape, lambda *_: (0,) * len(shape), pipeline_mode=pl.Buffered(1))


def kernel(x_prompt, x_sample, state_ret, p_prompt, p_sample, w_in, w_out, norm_pre, norm_post, sgu_w, sgu_b,
           sgu_ln, ret_gn, w_ple_proj, w_ple_gate):
    batch, seq, _ = x_prompt.shape
    dec_batch, dec_seq, _ = x_sample.shape
    n_tiles = seq // PROMPT_TILE
    n_tok = dec_batch * dec_seq
    assert seq % PROMPT_TILE == 0 and PROMPT_TILE % BLOCK_ROWS["prompt"] == 0 and w_in.shape[0] == 1
    assert CHUNK % dec_seq == 0 and n_tok % SAMPLE_TILE == 0 and SAMPLE_TILE % BLOCK_ROWS["sample"] == 0
    assert D_MODEL % CAST_STEPS == 0 and PLE_DIM % (16 * CAST_STEPS) == 0
    assert (CHUNK // dec_seq) % 2 == 0 and dec_batch % (SAMPLE_TILE // dec_seq) == 0

    small = (norm_pre[0][None, :], norm_post[0][None, :], sgu_ln[0][None, :], ret_gn[0][None, :],
             sgu_w[0], sgu_b[0])
    small_specs = [_const_spec((1, D_MODEL)), _const_spec((1, D_MODEL)), _const_spec((1, SGU_WIDTH)),
                   _const_spec((1, RET_HEADS * RET_DV)), _const_spec((SGU_GROUPS, CHUNK, CHUNK)),
                   _const_spec((SGU_GROUPS, CHUNK)), _const_spec((CHUNK, 2 * CHUNK)),
                   _const_spec((3, RET_HEADS, CHUNK, LANES))]
    table_scratch = [pltpu.VMEM((SGU_GROUPS // 2, CHUNK, 2 * CHUNK), BF16), pltpu.VMEM((CHUNK, SGU_WIDTH), F32)]
    weight_shapes = [(D_MODEL, IN_WIDTH), (D_MODEL, D_MODEL), (D_MODEL, D_MODEL), (PLE_DIM, D_MODEL)]

    tabs_p, g_chunk = _retention_tables(CHUNK)
    cos_p, sin_p = _rotary_tables(np.arange(seq))
    tile_of = lambda s: jnp.maximum(s - CAST_STEPS, 0)
    tile_spec = lambda width: pl.BlockSpec((1, PROMPT_TILE, width),
                                           lambda s: (tile_of(s) // n_tiles, tile_of(s) % n_tiles, 0))
    pos_spec = pl.BlockSpec((PROMPT_TILE, LANES), lambda s: (tile_of(s) % n_tiles, 0))
    blocks_per_tile = PROMPT_TILE // BLOCK_ROWS["prompt"]
    next_tile = lambda s: jnp.minimum(jnp.maximum(s - CAST_STEPS + 1, 0), batch * n_tiles - 1)
    next_spec = pl.BlockSpec((1, BLOCK_ROWS["prompt"], D_MODEL),
                             lambda s: (next_tile(s) // n_tiles, (next_tile(s) % n_tiles) * blocks_per_tile, 0))
    cast_spec = lambda shape: pl.BlockSpec((shape[0] // CAST_STEPS, shape[1]),
                                           lambda s: (jnp.minimum(s, CAST_STEPS - 1), 0))
    y_prompt, st_prompt, w_in_b, w_out_b, w_gate_b, w_ple_b = pl.pallas_call(
        functools.partial(_prompt_kernel, g_chunk, n_tiles),
        grid=(CAST_STEPS + batch * n_tiles,),
        in_specs=[tile_spec(D_MODEL), next_spec, tile_spec(PLE_DIM), pos_spec, pos_spec]
        + [cast_spec(shape) for shape in weight_shapes] + small_specs,
        out_specs=[tile_spec(D_MODEL),
                   pl.BlockSpec((1, RET_HEADS, RET_DK, RET_DV), lambda s: (tile_of(s) // n_tiles, 0, 0, 0))]
        + [pl.BlockSpec(shape, lambda s: (0, 0)) for shape in weight_shapes],
        out_shape=[jax.ShapeDtypeStruct((batch, seq, D_MODEL), F32),
                   jax.ShapeDtypeStruct((batch, RET_HEADS, RET_DK, RET_DV), F32)]
        + [jax.ShapeDtypeStruct(shape, BF16) for shape in weight_shapes],
        scratch_shapes=[pltpu.VMEM((PROMPT_TILE, IN_WIDTH), F32), pltpu.VMEM((BLOCK_ROWS["prompt"], IN_WIDTH), F32),
                        pltpu.VMEM((PROMPT_TILE, D_MODEL), BF16)] + table_scratch,
        compiler_params=pltpu.CompilerParams(dimension_semantics=("arbitrary",),
                                             vmem_limit_bytes=VMEM_LIMIT_BYTES),
        name="prompt_layer",
    )(x_prompt, x_prompt, p_prompt[0], cos_p, sin_p, w_in[0], w_out[0], w_ple_gate[0], w_ple_proj[0], *small,
      _mix_mask(CHUNK), tabs_p)

    tabs_s, g_seq = _retention_tables(dec_seq)
    cos_s, sin_s = _rotary_tables(PAST_LEN + np.arange(dec_seq))
    tile_seqs = SAMPLE_TILE // dec_seq
    tok_spec = lambda width: pl.BlockSpec((SAMPLE_TILE, width), lambda i: (i, 0))
    state_spec = pl.BlockSpec((tile_seqs // 2, 2, RET_HEADS, RET_DK, RET_DV), lambda i: (i, 0, 0, 0, 0))
    paired = (dec_batch // 2, 2, RET_HEADS, RET_DK, RET_DV)
    y_sample, st_sample, v_sample = pl.pallas_call(
        functools.partial(_sample_kernel, g_seq),
        grid=(n_tok // SAMPLE_TILE,),
        in_specs=[tok_spec(D_MODEL), tok_spec(PLE_DIM), _const_spec((dec_seq, LANES)),
                  _const_spec((dec_seq, LANES)), state_spec]
        + [_const_spec(shape) for shape in weight_shapes] + small_specs,
        out_specs=[tok_spec(D_MODEL), state_spec, tok_spec(SGU_WIDTH)],
        out_shape=[jax.ShapeDtypeStruct((n_tok, D_MODEL), F32),
                   jax.ShapeDtypeStruct(paired, F32), jax.ShapeDtypeStruct((n_tok, SGU_WIDTH), F32)],
        scratch_shapes=[pltpu.VMEM((SAMPLE_TILE, IN_WIDTH), F32), pltpu.VMEM((SAMPLE_TILE, D_MODEL), BF16)]
        + table_scratch,
        compiler_params=pltpu.CompilerParams(dimension_semantics=("arbitrary",),
                                             vmem_limit_bytes=VMEM_LIMIT_BYTES),
        name="sample_layer",
    )(x_sample.reshape(n_tok, D_MODEL), p_sample[0].reshape(n_tok, PLE_DIM), cos_s, sin_s,
      state_ret[0].reshape(paired),
      w_in_b, w_out_b, w_gate_b, w_ple_b, *small, _mix_mask(dec_seq), tabs_s)

    return (y_prompt, y_sample.reshape(dec_batch, dec_seq, D_MODEL), st_prompt[None],
            st_sample.reshape(state_ret.shape),
            v_sample.reshape(1, dec_batch, dec_seq, SGU_WIDTH))
```

```python
import functools

import numpy as np
import jax
import jax.numpy as jnp
from jax import lax
from jax.experimental import pallas as pl
from jax.experimental.pallas import tpu as pltpu

F32 = jnp.float32
BF16 = jnp.bfloat16

D_MODEL = 1024
PAST_LEN = 16384
SGU_WIDTH = 512
SGU_GROUPS = 8
SGU_GROUP_DIM = SGU_WIDTH // SGU_GROUPS
RET_HEADS = 4
RET_DK = 128
RET_DV = 128
CHUNK = 128
ROPE_THETA = 10000.0
PLE_DIM = 256
RMS_EPS = 1e-6
LN_EPS = 1e-5
IN_WIDTH = 3 * SGU_WIDTH + RET_HEADS * (2 * RET_DK + 2 * RET_DV)
O_SU, O_SV, O_SG = 0, SGU_WIDTH, 2 * SGU_WIDTH
O_Q = 3 * SGU_WIDTH
O_K = O_Q + RET_HEADS * RET_DK
O_V = O_K + RET_HEADS * RET_DK
O_RG = O_V + RET_HEADS * RET_DV

LANES = 128
VMEM_LIMIT_BYTES = 60 * 1024 * 1024

PROMPT_BLOCK = 512
BLOCK_ROWS = {"sample": CHUNK}
PIECE_COLS = 512
CAST_STEPS = 8
SAMPLE_TILE = 2 * CHUNK


def _log_gamma():
    return np.log(1.0 - 2.0 ** (-5.0 - np.arange(RET_HEADS, dtype=np.float64)))


def _retention_tables(seg):
    lg = _log_gamma()[:, None, None]
    r = np.arange(CHUNK)
    i, j = r[:, None] % seg, r[None, :] % seg
    same = (r[:, None] // seg) == (r[None, :] // seg)
    decay = np.where(same & (i >= j), np.exp(lg * np.maximum(i - j, 0)), 0.0)
    wq = np.broadcast_to(np.exp(lg * (i + 1.0)), (RET_HEADS, CHUNK, LANES))
    wkv = np.broadcast_to(np.exp(lg * (seg - 1.0 - i)), (RET_HEADS, CHUNK, LANES))
    scale = RET_DK ** -0.5
    tabs = np.stack([decay * scale, wq, wkv * scale]).astype(np.float32)
    g_seg = [float(v) for v in np.exp(_log_gamma() * seg)]
    return tabs, g_seg


def _rotary_tables(pos):
    half = RET_DK // 2
    inv = ROPE_THETA ** (-np.arange(half, dtype=np.float64) / half)
    ang = pos.astype(np.float64)[:, None] * inv[None, :]
    cos, sin = np.cos(ang), np.sin(ang)
    return (np.concatenate([cos, cos], axis=1).astype(np.float32),
            np.concatenate([-sin, sin], axis=1).astype(np.float32))


def _mix_mask(seg):
    r = np.arange(CHUNK)
    same = (r[:, None] // seg) == (r[None, :] // seg)
    m = same & ((r[None, :] % seg) <= (r[:, None] % seg))
    return np.concatenate([m, m], axis=1).astype(np.float32)


def _gelu(x):
    c = float(np.sqrt(2.0 / np.pi))
    half = 0.5 * x
    return half + half * jnp.tanh(x * (c + (c * 0.044715) * (x * x)))


def _silu(x):
    return x * jax.nn.sigmoid(x)


def _rotate(x, cos, sin_signed):
    return x * cos + pltpu.roll(x, RET_DK // 2, 1) * sin_signed


def _normed_bf16(x, norm_ref):
    ms = jnp.mean(x * x, axis=-1, keepdims=True)
    return (x * lax.rsqrt(ms + RMS_EPS) * norm_ref[...]).astype(BF16)


def _in_pieces(x, norm_pre_ref, w_in_ref, z_ref):
    box = {}

    def piece(c0):
        def run():
            if 'h' not in box:
                box['h'] = _normed_bf16(x, norm_pre_ref)
            z_ref[:, c0:c0 + PIECE_COLS] = jnp.dot(box['h'], w_in_ref[:, c0:c0 + PIECE_COLS],
                                                   preferred_element_type=F32)
        return run
    return [piece(c0) for c0 in range(0, IN_WIDTH, PIECE_COLS)]


def _out_pieces(x, p, mixin_ref, w_out_ref, norm_post_ref, w_gate_ref, w_ple_ref, y_ref):
    box = {}
    halves = [slice(c0, c0 + PIECE_COLS) for c0 in range(0, D_MODEL, PIECE_COLS)]

    def ple():
        box['ple'] = jnp.dot(p.astype(BF16), w_ple_ref[...], preferred_element_type=F32)

    def mix(i):
        def run():
            box['mix%d' % i] = jnp.dot(mixin_ref[...], w_out_ref[:, halves[i]], preferred_element_type=F32)
        return run

    def gate(i):
        def run():
            if 'x1' not in box:
                mixes = [box['mix%d' % j] for j in range(len(halves))]
                ms = sum(jnp.sum(m * m, axis=-1, keepdims=True) for m in mixes) * (1.0 / D_MODEL)
                scale = lax.rsqrt(ms + RMS_EPS)
                box['x1'] = [x[:, halves[j]] + mixes[j] * scale * norm_post_ref[:, halves[j]]
                             for j in range(len(halves))]
                box['x1b'] = jnp.concatenate(box['x1'], axis=1).astype(BF16)
            g = jnp.dot(box['x1b'], w_gate_ref[:, halves[i]], preferred_element_type=F32)
            y_ref[:, halves[i]] = box['x1'][i] + jax.nn.sigmoid(g) * box['ple'][:, halves[i]]
        return run
    return [ple] + [mix(i) for i in range(len(halves))] + [gate(i) for i in range(len(halves))]


def _chunk_stages(z_ref, rows, cos, sin_signed, ln_g_ref, gn_ref, bias_ref, wpair_ref, tabs_ref,
                  state_update, state_output, mixin_ref, vrows_ref):
    box = {}
    zs = lambda off, i, width: z_ref[rows, off + i * width:off + (i + 1) * width]

    def gating_in():
        sv = _gelu(z_ref[rows, O_SV:O_SV + SGU_WIDTH])
        mu = jnp.mean(sv, axis=-1, keepdims=True)
        cen = sv - mu
        var = jnp.mean(cen * cen, axis=-1, keepdims=True)
        vn = cen * lax.rsqrt(var + LN_EPS) * ln_g_ref[...]
        if vrows_ref is not None:
            vrows_ref[rows, :] = vn
        low_lanes = lax.broadcasted_iota(jnp.int32, (CHUNK, LANES), 1) < SGU_GROUP_DIM
        for m in range(SGU_GROUPS // 2):
            blk = vn[:, m * LANES:(m + 1) * LANES]
            rhs = jnp.concatenate([jnp.where(low_lanes, blk, 0.0), jnp.where(low_lanes, 0.0, blk)],
                                  axis=0).astype(BF16)
            box['mixed%d' % m] = jnp.dot(wpair_ref[m], rhs, preferred_element_type=F32)

    def retention_in():
        c, s = cos(), sin_signed()
        for h in range(RET_HEADS):
            q = _rotate(zs(O_Q, h, RET_DK), c, s)
            k = _rotate(zs(O_K, h, RET_DK), c, s)
            v = zs(O_V, h, RET_DV)
            k_b = k.astype(BF16)
            box['scores%d' % h] = lax.dot_general(q.astype(BF16), k_b, (((1,), (1,)), ((), ())),
                                                  preferred_element_type=F32)
            box['qw%d' % h] = (q * tabs_ref[1, h]).astype(BF16)
            box['v%d' % h] = v.astype(BF16)
            box['u%d' % h] = state_update(h, k_b, (v * tabs_ref[2, h]).astype(BF16))

    def gating_out_retention_mid():
        for m in range(SGU_GROUPS // 2):
            cols = slice(m * LANES, (m + 1) * LANES)
            mixed = box['mixed%d' % m] + bias_ref[:, cols]
            su = _gelu(zs(O_SU, m, LANES))
            mixin_ref[rows, cols] = (_silu(zs(O_SG, m, LANES)) * (su * mixed)).astype(BF16)
        for h in range(RET_HEADS):
            scores_b = (box['scores%d' % h] * tabs_ref[0, h]).astype(BF16)
            box['o%d' % h] = state_output(h, scores_b, box['qw%d' % h], box['v%d' % h], box['u%d' % h])

    def retention_out():
        for h in range(RET_HEADS):
            o = box['o%d' % h]
            mu = jnp.mean(o, axis=-1, keepdims=True)
            cen = o - mu
            var = jnp.mean(cen * cen, axis=-1, keepdims=True)
            on = cen * lax.rsqrt(var + LN_EPS) * gn_ref[:, h * RET_DV:(h + 1) * RET_DV]
            mixin_ref[rows, SGU_WIDTH + h * RET_DV:SGU_WIDTH + (h + 1) * RET_DV] = (
                _silu(zs(O_RG, h, RET_DV)) * on).astype(BF16)

    return [gating_in, retention_in, gating_out_retention_mid, retention_out]


def _interleave(pieces, stages):
    slots = max(len(stages), 1)
    done = 0
    for i in range(slots):
        upto = -(-(i + 1) * len(pieces) // slots)
        for piece in pieces[done:upto]:
            piece()
        done = upto
        if i < len(stages):
            stages[i]()


def _emit_blocks(in_pieces, out_pieces, stages, blocks, first_in=None, next_in=None):
    _interleave(first_in or [], [])
    for i, blk in enumerate(blocks):
        nxt = in_pieces(blocks[i + 1]) if i + 1 < len(blocks) else (next_in or [])
        if i > 0:
            ple, mix0, mix1, gate0, gate1 = out_pieces(blocks[i - 1])
            pieces = [ple, mix0, mix1] + nxt[:3] + [gate0] + nxt[3:4] + [gate1] + nxt[4:]
        else:
            pieces = nxt
        _interleave(pieces, stages(blk))
    _interleave(out_pieces(blocks[-1]), [])


def _init_tables(seg, sgu_w_ref, sgu_b_ref, mask_ref, wpm_ref, bias_ref):
    rows = lax.broadcasted_iota(jnp.int32, (CHUNK, LANES), 0)
    lanes = lax.broadcasted_iota(jnp.int32, (CHUNK, LANES), 1)
    if seg == CHUNK:
        group = lambda g: sgu_w_ref[g]
        bias_rows = sgu_b_ref[...]
    else:
        reps = CHUNK // seg
        select = jnp.where((rows < seg) & (lanes % seg == rows), 1.0, 0.0).astype(BF16)
        first = lambda a: jnp.where(lanes[:seg] < seg, a, 0.0)

        def group(g):
            stacked = jnp.concatenate([first(sgu_w_ref[g, 0:seg, :])] * reps, axis=0)
            return jnp.dot(stacked.astype(BF16), select, preferred_element_type=F32)
        b_first = jnp.where(lax.broadcasted_iota(jnp.int32, (SGU_GROUPS, LANES), 1) < seg, sgu_b_ref[...], 0.0)
        bias_rows = sum(pltpu.roll(b_first, r * seg, 1) for r in range(reps))
    for m in range(SGU_GROUPS // 2):
        pair = jnp.concatenate([group(2 * m), group(2 * m + 1)], axis=1)
        wpm_ref[m] = jnp.where(mask_ref[...] > 0.0, pair, 0.0).astype(BF16)
    group_of_lane = lax.broadcasted_iota(jnp.int32, (SGU_GROUPS, SGU_WIDTH), 1) // SGU_GROUP_DIM
    expand = jnp.where(group_of_lane == lax.broadcasted_iota(jnp.int32, (SGU_GROUPS, SGU_WIDTH), 0),
                       1.0, 0.0).astype(BF16)
    total, rest = jnp.zeros((CHUNK, SGU_WIDTH), F32), bias_rows
    for _ in range(3):
        term = rest.astype(BF16)
        total = total + lax.dot_general(term, expand, (((0,), (0,)), ((), ())), preferred_element_type=F32)
        rest = rest - term.astype(F32)
    bias_ref[...] = total


def _prompt_kernel(g_chunk, n_blocks, blocks_per_row, x_next_ref, x_prev_ref, p_prev_ref, cos_ref, sin_ref,
                   w_in_ref, w_out_ref, w_gate_ref, w_ple_ref, norm_pre_ref, norm_post_ref, ln_g_ref, gn_ref,
                   sgu_w_ref, sgu_b_ref, mask_ref, tabs_ref, y_ref, state_ref, w_in_b_ref, w_out_b_ref,
                   w_gate_b_ref, w_ple_b_ref, z_ref, mixin_ref, state_s_ref, wpm_ref, bias_ref):
    step = pl.program_id(0)
    j = step - CAST_STEPS

    def in_pieces(slot):
        return _in_pieces(x_next_ref[0], norm_pre_ref, w_in_b_ref, z_ref.at[slot])

    def out_pieces(slot):
        return _out_pieces(x_prev_ref[0], p_prev_ref[0], mixin_ref.at[slot], w_out_b_ref, norm_post_ref,
                           w_gate_b_ref, w_ple_b_ref, y_ref.at[0])

    @pl.when(step == 0)
    def _():
        _init_tables(CHUNK, sgu_w_ref, sgu_b_ref, mask_ref, wpm_ref, bias_ref)
        mixin_ref[...] = jnp.zeros_like(mixin_ref)
        state_s_ref[...] = jnp.zeros_like(state_s_ref)

    @pl.when(step < CAST_STEPS)
    def _():
        for src, dst in ((w_in_ref, w_in_b_ref), (w_out_ref, w_out_b_ref), (w_gate_ref, w_gate_b_ref),
                         (w_ple_ref, w_ple_b_ref)):
            n = src.shape[0]
            dst[pl.ds(pl.multiple_of(step * n, n), n), :] = src[...].astype(BF16)

    @pl.when(step == CAST_STEPS - 1)
    def _():
        _interleave(in_pieces(0), [])

    def pipeline_step(cur, other):
        row_start = lax.rem(j, blocks_per_row) == 0
        z_cur, mixin_cur = z_ref.at[cur], mixin_ref.at[cur]

        def state_update(h, k_b, vw_b):
            return lax.dot_general(k_b, vw_b, (((0,), (0,)), ((), ())), preferred_element_type=F32)

        def make_state_output(first_chunk):
            def state_output(h, scores_b, qw_b, v_b, u):
                s = state_s_ref[h]
                if first_chunk:
                    s = jnp.where(row_start, 0.0, s)
                o = jnp.dot(jnp.concatenate([scores_b, qw_b], axis=1),
                            jnp.concatenate([v_b, s.astype(BF16)], axis=0), preferred_element_type=F32)
                state_s_ref[h] = g_chunk[h] * s + u
                return o
            return state_output

        stages = []
        for r in range(0, PROMPT_BLOCK, CHUNK):
            rows = slice(r, r + CHUNK)
            stages += _chunk_stages(z_cur, rows, lambda rows=rows: cos_ref[rows, :],
                                    lambda rows=rows: sin_ref[rows, :], ln_g_ref, gn_ref, bias_ref, wpm_ref,
                                    tabs_ref, state_update, make_state_output(r == 0), mixin_cur, None)
        ple, mix0, mix1, gate0, gate1 = out_pieces(other)
        nxt = in_pieces(other)
        _interleave([ple, mix0, mix1] + nxt[:3] + [gate0] + nxt[3:4] + [gate1] + nxt[4:], stages)
        state_ref[0] = state_s_ref[...]

    for parity in (0, 1):
        pl.when((j >= 0) & (j < n_blocks) & (lax.rem(j + 2, 2) == parity))(
            functools.partial(pipeline_step, parity, 1 - parity))

    @pl.when(j == n_blocks)
    def _():
        _interleave(out_pieces((n_blocks - 1) % 2), [])


def _sample_kernel(g_seq, x_ref, p_ref, cos_ref, sin_ref, st_in_ref, w_in_ref, w_out_ref, w_gate_ref,
                   w_ple_ref, norm_pre_ref, norm_post_ref, ln_g_ref, gn_ref, sgu_w_ref, sgu_b_ref, mask_ref,
                   tabs_ref, y_ref, st_out_ref, vrows_ref, z_ref, mixin_ref, wpm_ref, bias_ref):
    seq_len = cos_ref.shape[0]
    seqs = CHUNK // seq_len

    @pl.when(pl.program_id(0) == 0)
    def _():
        _init_tables(seq_len, sgu_w_ref, sgu_b_ref, mask_ref, wpm_ref, bias_ref)

    pairs = seqs // 2
    per_pair = lambda a: a.reshape(pairs, 2 * seq_len, a.shape[-1])
    first_of_pair = lax.broadcasted_iota(jnp.int32, (1, 2 * seq_len, 1), 1) < seq_len
    tile_rows = lambda ref: jnp.concatenate([ref[...]] * seqs, axis=0)

    def stages(blk):
        pq = slice(blk.start // (2 * seq_len), blk.stop // (2 * seq_len))

        def state_update(h, k_b, vw_b):
            vw = per_pair(vw_b)
            zero = jnp.zeros_like(vw)
            both = jnp.concatenate([jnp.where(first_of_pair, vw, zero), jnp.where(first_of_pair, zero, vw)],
                                   axis=-1)
            u = jnp.einsum('pjd,pje->pde', per_pair(k_b), both, preferred_element_type=F32)
            st_out_ref[pq, 0, h] = g_seq[h] * st_in_ref[pq, 0, h] + u[:, :, :RET_DV]
            st_out_ref[pq, 1, h] = g_seq[h] * st_in_ref[pq, 1, h] + u[:, :, RET_DV:]
            return None

        def state_output(h, scores_b, qw_b, v_b, _):
            o = jnp.dot(scores_b, v_b, preferred_element_type=F32)
            s_pair = jnp.concatenate([st_in_ref[pq, 0, h], st_in_ref[pq, 1, h]], axis=-1).astype(BF16)
            both = jnp.einsum('pid,pde->pie', per_pair(qw_b), s_pair, preferred_element_type=F32)
            o_inter = jnp.where(first_of_pair, both[:, :, :RET_DV], both[:, :, RET_DV:])
            return o + o_inter.reshape(CHUNK, RET_DV)

        return _chunk_stages(z_ref, blk, lambda: tile_rows(cos_ref), lambda: tile_rows(sin_ref), ln_g_ref,
                             gn_ref, bias_ref, wpm_ref, tabs_ref, state_update, state_output, mixin_ref,
                             vrows_ref)

    def in_pieces(blk):
        return _in_pieces(x_ref[blk, :], norm_pre_ref, w_in_ref, z_ref.at[blk, :])

    def out_pieces(blk):
        return _out_pieces(x_ref[blk, :], p_ref[blk, :], mixin_ref.at[blk, :], w_out_ref, norm_post_ref,
                           w_gate_ref, w_ple_ref, y_ref.at[blk, :])

    rows = BLOCK_ROWS["sample"]
    _emit_blocks(in_pieces, out_pieces, stages, [slice(r, r + rows) for r in range(0, SAMPLE_TILE, rows)],
                 first_in=in_pieces(slice(0, rows)))


def _const_spec(shape):
    return pl.BlockSpec(shape, lambda *_: (0,) * len(shape), pipeline_mode=pl.Buffered(1))


def kernel(x_prompt, x_sample, state_ret, p_prompt, p_sample, w_in, w_out, norm_pre, norm_post, sgu_w, sgu_b,
           sgu_ln, ret_gn, w_ple_proj, w_ple_gate):
    batch, seq, _ = x_prompt.shape
    dec_batch, dec_seq, _ = x_sample.shape
    blocks_per_row = seq // PROMPT_BLOCK
    n_tok = dec_batch * dec_seq
    assert seq % PROMPT_BLOCK == 0 and PROMPT_BLOCK % CHUNK == 0 and w_in.shape[0] == 1
    assert CHUNK % dec_seq == 0 and n_tok % SAMPLE_TILE == 0 and SAMPLE_TILE % BLOCK_ROWS["sample"] == 0
    assert D_MODEL % CAST_STEPS == 0 and PLE_DIM % (16 * CAST_STEPS) == 0
    assert (CHUNK // dec_seq) % 2 == 0 and dec_batch % (SAMPLE_TILE // dec_seq) == 0

    small = (norm_pre[0][None, :], norm_post[0][None, :], sgu_ln[0][None, :], ret_gn[0][None, :],
             sgu_w[0], sgu_b[0])
    small_specs = [_const_spec((1, D_MODEL)), _const_spec((1, D_MODEL)), _const_spec((1, SGU_WIDTH)),
                   _const_spec((1, RET_HEADS * RET_DV)), _const_spec((SGU_GROUPS, CHUNK, CHUNK)),
                   _const_spec((SGU_GROUPS, CHUNK)), _const_spec((CHUNK, 2 * CHUNK)),
                   _const_spec((3, RET_HEADS, CHUNK, LANES))]
    table_scratch = [pltpu.VMEM((SGU_GROUPS // 2, CHUNK, 2 * CHUNK), BF16), pltpu.VMEM((CHUNK, SGU_WIDTH), F32)]
    weight_shapes = [(D_MODEL, IN_WIDTH), (D_MODEL, D_MODEL), (D_MODEL, D_MODEL), (PLE_DIM, D_MODEL)]

    tabs_p, g_chunk = _retention_tables(CHUNK)
    cos_p, sin_p = _rotary_tables(np.arange(seq))
    n_blocks = batch * blocks_per_row
    block_of = lambda s, lag: jnp.clip(s - CAST_STEPS + lag, 0, n_blocks - 1)
    blk_spec = lambda width, lag: pl.BlockSpec(
        (1, PROMPT_BLOCK, width),
        lambda s: (block_of(s, lag) // blocks_per_row, block_of(s, lag) % blocks_per_row, 0))
    pos_spec = pl.BlockSpec((PROMPT_BLOCK, LANES), lambda s: (block_of(s, 0) % blocks_per_row, 0))
    cast_spec = lambda shape: pl.BlockSpec((shape[0] // CAST_STEPS, shape[1]),
                                           lambda s: (jnp.minimum(s, CAST_STEPS - 1), 0))
    y_prompt, st_prompt, w_in_b, w_out_b, w_gate_b, w_ple_b = pl.pallas_call(
        functools.partial(_prompt_kernel, g_chunk, n_blocks, blocks_per_row),
        grid=(CAST_STEPS + n_blocks + 1,),
        in_specs=[blk_spec(D_MODEL, 1), blk_spec(D_MODEL, -1), blk_spec(PLE_DIM, -1), pos_spec, pos_spec]
        + [cast_spec(shape) for shape in weight_shapes] + small_specs,
        out_specs=[blk_spec(D_MODEL, -1),
                   pl.BlockSpec((1, RET_HEADS, RET_DK, RET_DV),
                                lambda s: (block_of(s, 0) // blocks_per_row, 0, 0, 0))]
        + [pl.BlockSpec(shape, lambda s: (0, 0)) for shape in weight_shapes],
        out_shape=[jax.ShapeDtypeStruct((batch, seq, D_MODEL), F32),
                   jax.ShapeDtypeStruct((batch, RET_HEADS, RET_DK, RET_DV), F32)]
        + [jax.ShapeDtypeStruct(shape, BF16) for shape in weight_shapes],
        scratch_shapes=[pltpu.VMEM((2, PROMPT_BLOCK, IN_WIDTH), F32), pltpu.VMEM((2, PROMPT_BLOCK, D_MODEL), BF16),
                        pltpu.VMEM((RET_HEADS, RET_DK, RET_DV), F32)] + table_scratch,
        compiler_params=pltpu.CompilerParams(dimension_semantics=("arbitrary",),
                                             vmem_limit_bytes=VMEM_LIMIT_BYTES),
        name="prompt_layer",
    )(x_prompt, x_prompt, p_prompt[0], cos_p, sin_p, w_in[0], w_out[0], w_ple_gate[0], w_ple_proj[0], *small,
      _mix_mask(CHUNK), tabs_p)

    tabs_s, g_seq = _retention_tables(dec_seq)
    cos_s, sin_s = _rotary_tables(PAST_LEN + np.arange(dec_seq))
    tile_seqs = SAMPLE_TILE // dec_seq
    tok_spec = lambda width: pl.BlockSpec((SAMPLE_TILE, width), lambda i: (i, 0))
    state_spec = pl.BlockSpec((tile_seqs // 2, 2, RET_HEADS, RET_DK, RET_DV), lambda i: (i, 0, 0, 0, 0))
    paired = (dec_batch // 2, 2, RET_HEADS, RET_DK, RET_DV)
    y_sample, st_sample, v_sample = pl.pallas_call(
        functools.partial(_sample_kernel, g_seq),
        grid=(n_tok // SAMPLE_TILE,),
        in_specs=[tok_spec(D_MODEL), tok_spec(PLE_DIM), _const_spec((dec_seq, LANES)),
                  _const_spec((dec_seq, LANES)), state_spec]
        + [_const_spec(shape) for shape in weight_shapes] + small_specs,
        out_specs=[tok_spec(D_MODEL), state_spec, tok_spec(SGU_WIDTH)],
        out_shape=[jax.ShapeDtypeStruct((n_tok, D_MODEL), F32),
                   jax.ShapeDtypeStruct(paired, F32), jax.ShapeDtypeStruct((n_tok, SGU_WIDTH), F32)],
        scratch_shapes=[pltpu.VMEM((SAMPLE_TILE, IN_WIDTH), F32), pltpu.VMEM((SAMPLE_TILE, D_MODEL), BF16)]
        + table_scratch,
        compiler_params=pltpu.CompilerParams(dimension_semantics=("arbitrary",),
                                             vmem_limit_bytes=VMEM_LIMIT_BYTES),
        name="sample_layer",
    )(x_sample.reshape(n_tok, D_MODEL), p_sample[0].reshape(n_tok, PLE_DIM), cos_s, sin_s,
      state_ret[0].reshape(paired),
      w_in_b, w_out_b, w_gate_b, w_ple_b, *small, _mix_mask(dec_seq), tabs_s)

    return (y_prompt, y_sample.reshape(dec_batch, dec_seq, D_MODEL), st_prompt[None],
            st_sample.reshape(state_ret.shape),
            v_sample.reshape(1, dec_batch, dec_seq, SGU_WIDTH))
```

```python
import functools

import numpy as np
import jax
import jax.numpy as jnp
from jax import lax
from jax.experimental import pallas as pl
from jax.experimental.pallas import tpu as pltpu

F32 = jnp.float32
BF16 = jnp.bfloat16

D_MODEL = 1024
PAST_LEN = 16384
SGU_WIDTH = 512
SGU_GROUPS = 8
SGU_GROUP_DIM = SGU_WIDTH // SGU_GROUPS
RET_HEADS = 4
RET_DK = 128
RET_DV = 128
CHUNK = 128
ROPE_THETA = 10000.0
PLE_DIM = 256
RMS_EPS = 1e-6
LN_EPS = 1e-5
IN_WIDTH = 3 * SGU_WIDTH + RET_HEADS * (2 * RET_DK + 2 * RET_DV)
O_SU, O_SV, O_SG = 0, SGU_WIDTH, 2 * SGU_WIDTH
O_Q = 3 * SGU_WIDTH
O_K = O_Q + RET_HEADS * RET_DK
O_V = O_K + RET_HEADS * RET_DK
O_RG = O_V + RET_HEADS * RET_DV

LANES = 128
VMEM_LIMIT_BYTES = 60 * 1024 * 1024

PROMPT_TILE = 1024
BLOCK_ROWS = {"prompt": 256, "sample": CHUNK}
PIECE_COLS = 512
CAST_STEPS = 8
SAMPLE_TILE = 2 * CHUNK


def _log_gamma():
    return np.log(1.0 - 2.0 ** (-5.0 - np.arange(RET_HEADS, dtype=np.float64)))


def _retention_tables(seg):
    lg = _log_gamma()[:, None, None]
    r = np.arange(CHUNK)
    i, j = r[:, None] % seg, r[None, :] % seg
    same = (r[:, None] // seg) == (r[None, :] // seg)
    decay = np.where(same & (i >= j), np.exp(lg * np.maximum(i - j, 0)), 0.0)
    wq = np.broadcast_to(np.exp(lg * (i + 1.0)), (RET_HEADS, CHUNK, LANES))
    wkv = np.broadcast_to(np.exp(lg * (seg - 1.0 - i)), (RET_HEADS, CHUNK, LANES))
    scale = RET_DK ** -0.5
    tabs = np.stack([decay * scale, wq, wkv * scale]).astype(np.float32)
    g_seg = [float(v) for v in np.exp(_log_gamma() * seg)]
    return tabs, g_seg


def _rotary_tables(pos):
    half = RET_DK // 2
    inv = ROPE_THETA ** (-np.arange(half, dtype=np.float64) / half)
    ang = pos.astype(np.float64)[:, None] * inv[None, :]
    cos, sin = np.cos(ang), np.sin(ang)
    return (np.concatenate([cos, cos], axis=1).astype(np.float32),
            np.concatenate([-sin, sin], axis=1).astype(np.float32))


def _mix_mask(seg):
    r = np.arange(CHUNK)
    same = (r[:, None] // seg) == (r[None, :] // seg)
    m = same & ((r[None, :] % seg) <= (r[:, None] % seg))
    return np.concatenate([m, m], axis=1).astype(np.float32)


def _gelu(x):
    c = float(np.sqrt(2.0 / np.pi))
    half = 0.5 * x
    return half + half * jnp.tanh(x * (c + (c * 0.044715) * (x * x)))


def _silu(x):
    return x * jax.nn.sigmoid(x)


def _rotate(x, cos, sin_signed):
    return x * cos + pltpu.roll(x, RET_DK // 2, 1) * sin_signed


def _normed_bf16(x, norm_ref):
    ms = jnp.mean(x * x, axis=-1, keepdims=True)
    return (x * lax.rsqrt(ms + RMS_EPS) * norm_ref[...]).astype(BF16)


def _in_pieces(x, norm_pre_ref, w_in_ref, z_ref):
    box = {}

    def piece(c0):
        def run():
            if 'h' not in box:
                box['h'] = _normed_bf16(x, norm_pre_ref)
            z_ref[:, c0:c0 + PIECE_COLS] = jnp.dot(box['h'], w_in_ref[:, c0:c0 + PIECE_COLS],
                                                   preferred_element_type=F32)
        return run
    return [piece(c0) for c0 in range(0, IN_WIDTH, PIECE_COLS)]


def _out_pieces(x, p, mixin_ref, w_out_ref, norm_post_ref, w_gate_ref, w_ple_ref, y_ref):
    box = {}
    halves = [slice(c0, c0 + PIECE_COLS) for c0 in range(0, D_MODEL, PIECE_COLS)]

    def ple():
        box['ple'] = jnp.dot(p.astype(BF16), w_ple_ref[...], preferred_element_type=F32)

    def mix(i):
        def run():
            box['mix%d' % i] = jnp.dot(mixin_ref[...], w_out_ref[:, halves[i]], preferred_element_type=F32)
        return run

    def gate(i):
        def run():
            if 'x1' not in box:
                mixes = [box['mix%d' % j] for j in range(len(halves))]
                ms = sum(jnp.sum(m * m, axis=-1, keepdims=True) for m in mixes) * (1.0 / D_MODEL)
                scale = lax.rsqrt(ms + RMS_EPS)
                box['x1'] = [x[:, halves[j]] + mixes[j] * scale * norm_post_ref[:, halves[j]]
                             for j in range(len(halves))]
                box['x1b'] = jnp.concatenate(box['x1'], axis=1).astype(BF16)
            g = jnp.dot(box['x1b'], w_gate_ref[:, halves[i]], preferred_element_type=F32)
            y_ref[:, halves[i]] = box['x1'][i] + jax.nn.sigmoid(g) * box['ple'][:, halves[i]]
        return run
    return [ple] + [mix(i) for i in range(len(halves))] + [gate(i) for i in range(len(halves))]


def _chunk_stages(z_ref, rows, cos, sin_signed, ln_g_ref, gn_ref, bias_ref, wpair_ref, tabs_ref,
                  state_update, state_output, mixin_ref, vrows_ref):
    box = {}
    zs = lambda off, i, width: z_ref[rows, off + i * width:off + (i + 1) * width]

    def gating_in():
        sv = _gelu(z_ref[rows, O_SV:O_SV + SGU_WIDTH])
        mu = jnp.mean(sv, axis=-1, keepdims=True)
        cen = sv - mu
        var = jnp.mean(cen * cen, axis=-1, keepdims=True)
        vn = cen * lax.rsqrt(var + LN_EPS) * ln_g_ref[...]
        if vrows_ref is not None:
            vrows_ref[rows, :] = vn
        low_lanes = lax.broadcasted_iota(jnp.int32, (CHUNK, LANES), 1) < SGU_GROUP_DIM
        for m in range(SGU_GROUPS // 2):
            blk = vn[:, m * LANES:(m + 1) * LANES]
            rhs = jnp.concatenate([jnp.where(low_lanes, blk, 0.0), jnp.where(low_lanes, 0.0, blk)],
                                  axis=0).astype(BF16)
            box['mixed%d' % m] = jnp.dot(wpair_ref[m], rhs, preferred_element_type=F32)

    def retention_in():
        c, s = cos(), sin_signed()
        for h in range(RET_HEADS):
            q = _rotate(zs(O_Q, h, RET_DK), c, s)
            k = _rotate(zs(O_K, h, RET_DK), c, s)
            v = zs(O_V, h, RET_DV)
            k_b = k.astype(BF16)
            box['scores%d' % h] = lax.dot_general(q.astype(BF16), k_b, (((1,), (1,)), ((), ())),
                                                  preferred_element_type=F32)
            box['qw%d' % h] = (q * tabs_ref[1, h]).astype(BF16)
            box['v%d' % h] = v.astype(BF16)
            box['u%d' % h] = state_update(h, k_b, (v * tabs_ref[2, h]).astype(BF16))

    def gating_out_retention_mid():
        for m in range(SGU_GROUPS // 2):
            cols = slice(m * LANES, (m + 1) * LANES)
            mixed = box['mixed%d' % m] + bias_ref[:, cols]
            su = _gelu(zs(O_SU, m, LANES))
            mixin_ref[rows, cols] = (_silu(zs(O_SG, m, LANES)) * (su * mixed)).astype(BF16)
        for h in range(RET_HEADS):
            scores_b = (box['scores%d' % h] * tabs_ref[0, h]).astype(BF16)
            box['o%d' % h] = state_output(h, scores_b, box['qw%d' % h], box['v%d' % h], box['u%d' % h])

    def retention_out():
        for h in range(RET_HEADS):
            o = box['o%d' % h]
            mu = jnp.mean(o, axis=-1, keepdims=True)
            cen = o - mu
            var = jnp.mean(cen * cen, axis=-1, keepdims=True)
            on = cen * lax.rsqrt(var + LN_EPS) * gn_ref[:, h * RET_DV:(h + 1) * RET_DV]
            mixin_ref[rows, SGU_WIDTH + h * RET_DV:SGU_WIDTH + (h + 1) * RET_DV] = (
                _silu(zs(O_RG, h, RET_DV)) * on).astype(BF16)

    def both_in():
        gating_in()
        retention_in()

    return [both_in, gating_out_retention_mid, retention_out]


def _interleave(pieces, stages):
    slots = max(len(stages), 1)
    done = 0
    for i in range(slots):
        upto = -(-(i + 1) * len(pieces) // slots)
        for piece in pieces[done:upto]:
            piece()
        done = upto
        if i < len(stages):
            stages[i]()


def _emit_blocks(in_pieces, out_pieces, stages, blocks, first_in=None, next_in=None):
    _interleave(first_in or [], [])
    for i, blk in enumerate(blocks):
        nxt = in_pieces(blocks[i + 1]) if i + 1 < len(blocks) else (next_in or [])
        if i > 0:
            ple, mix0, mix1, gate0, gate1 = out_pieces(blocks[i - 1])
            pieces = [ple, mix0, mix1] + nxt[:3] + [gate0] + nxt[3:4] + [gate1] + nxt[4:]
        else:
            pieces = nxt
        _interleave(pieces, stages(blk))
    _interleave(out_pieces(blocks[-1]), [])


def _init_tables(seg, sgu_w_ref, sgu_b_ref, mask_ref, wpm_ref, bias_ref):
    rows = lax.broadcasted_iota(jnp.int32, (CHUNK, LANES), 0)
    lanes = lax.broadcasted_iota(jnp.int32, (CHUNK, LANES), 1)
    if seg == CHUNK:
        group = lambda g: sgu_w_ref[g]
        bias_rows = sgu_b_ref[...]
    else:
        reps = CHUNK // seg
        select = jnp.where((rows < seg) & (lanes % seg == rows), 1.0, 0.0).astype(BF16)
        first = lambda a: jnp.where(lanes[:seg] < seg, a, 0.0)

        def group(g):
            stacked = jnp.concatenate([first(sgu_w_ref[g, 0:seg, :])] * reps, axis=0)
            return jnp.dot(stacked.astype(BF16), select, preferred_element_type=F32)
        b_first = jnp.where(lax.broadcasted_iota(jnp.int32, (SGU_GROUPS, LANES), 1) < seg, sgu_b_ref[...], 0.0)
        bias_rows = sum(pltpu.roll(b_first, r * seg, 1) for r in range(reps))
    for m in range(SGU_GROUPS // 2):
        pair = jnp.concatenate([group(2 * m), group(2 * m + 1)], axis=1)
        wpm_ref[m] = jnp.where(mask_ref[...] > 0.0, pair, 0.0).astype(BF16)
    group_of_lane = lax.broadcasted_iota(jnp.int32, (SGU_GROUPS, SGU_WIDTH), 1) // SGU_GROUP_DIM
    expand = jnp.where(group_of_lane == lax.broadcasted_iota(jnp.int32, (SGU_GROUPS, SGU_WIDTH), 0),
                       1.0, 0.0).astype(BF16)
    total, rest = jnp.zeros((CHUNK, SGU_WIDTH), F32), bias_rows
    for _ in range(3):
        term = rest.astype(BF16)
        total = total + lax.dot_general(term, expand, (((0,), (0,)), ((), ())), preferred_element_type=F32)
        rest = rest - term.astype(F32)
    bias_ref[...] = total


def _prompt_kernel(g_chunk, n_tiles, x_ref, x_next_ref, p_ref, cos_ref, sin_ref, w_in_ref, w_out_ref, w_gate_ref,
                   w_ple_ref, norm_pre_ref, norm_post_ref, ln_g_ref, gn_ref, sgu_w_ref, sgu_b_ref, mask_ref,
                   tabs_ref, y_ref, state_ref, w_in_b_ref, w_out_b_ref, w_gate_b_ref, w_ple_b_ref,
                   z_ref, z_first_ref, mixin_ref, wpm_ref, bias_ref):
    step = pl.program_id(0)
    first = slice(0, BLOCK_ROWS["prompt"])

    def next_first_in():
        return _in_pieces(x_next_ref[0], norm_pre_ref, w_in_b_ref, z_first_ref)

    @pl.when(step == 0)
    def _():
        _init_tables(CHUNK, sgu_w_ref, sgu_b_ref, mask_ref, wpm_ref, bias_ref)

    @pl.when(step < CAST_STEPS)
    def _():
        for src, dst in ((w_in_ref, w_in_b_ref), (w_out_ref, w_out_b_ref), (w_gate_ref, w_gate_b_ref),
                         (w_ple_ref, w_ple_b_ref)):
            n = src.shape[0]
            dst[pl.ds(pl.multiple_of(step * n, n), n), :] = src[...].astype(BF16)

    @pl.when(step == CAST_STEPS - 1)
    def _():
        _interleave(next_first_in(), [])

    @pl.when(step >= CAST_STEPS)
    def _():
        @pl.when((step - CAST_STEPS) % n_tiles == 0)
        def _():
            state_ref[...] = jnp.zeros_like(state_ref)

        def state_update(h, k_b, vw_b):
            return lax.dot_general(k_b, vw_b, (((0,), (0,)), ((), ())), preferred_element_type=F32)

        def state_output(h, scores_b, qw_b, v_b, u):
            s = state_ref[0, h]
            o = jnp.dot(jnp.concatenate([scores_b, qw_b], axis=1),
                        jnp.concatenate([v_b, s.astype(BF16)], axis=0), preferred_element_type=F32)
            state_ref[0, h] = g_chunk[h] * s + u
            return o

        def in_pieces(blk):
            return _in_pieces(x_ref[0, blk, :], norm_pre_ref, w_in_b_ref, z_ref.at[blk, :])

        def out_pieces(blk):
            return _out_pieces(x_ref[0, blk, :], p_ref[0, blk, :], mixin_ref.at[blk, :], w_out_b_ref,
                               norm_post_ref, w_gate_b_ref, w_ple_b_ref, y_ref.at[0, blk, :])

        def stages(blk):
            out = []
            for r in range(blk.start, blk.stop, CHUNK):
                rows = slice(r, r + CHUNK)
                out += _chunk_stages(z_first_ref if blk == first else z_ref, rows,
                                     lambda rows=rows: cos_ref[rows, :],
                                     lambda rows=rows: sin_ref[rows, :], ln_g_ref, gn_ref, bias_ref, wpm_ref,
                                     tabs_ref, state_update, state_output, mixin_ref, None)
            return out

        rows = BLOCK_ROWS["prompt"]
        _emit_blocks(in_pieces, out_pieces, stages, [slice(r, r + rows) for r in range(0, PROMPT_TILE, rows)],
                     next_in=next_first_in())


def _sample_kernel(g_seq, x_ref, p_ref, cos_ref, sin_ref, st_in_ref, w_in_ref, w_out_ref, w_gate_ref,
                   w_ple_ref, norm_pre_ref, norm_post_ref, ln_g_ref, gn_ref, sgu_w_ref, sgu_b_ref, mask_ref,
                   tabs_ref, y_ref, st_out_ref, vrows_ref, z_ref, mixin_ref, wpm_ref, bias_ref):
    seq_len = cos_ref.shape[0]
    seqs = CHUNK // seq_len

    @pl.when(pl.program_id(0) == 0)
    def _():
        _init_tables(seq_len, sgu_w_ref, sgu_b_ref, mask_ref, wpm_ref, bias_ref)

    pairs = seqs // 2
    per_pair = lambda a: a.reshape(pairs, 2 * seq_len, a.shape[-1])
    first_of_pair = lax.broadcasted_iota(jnp.int32, (1, 2 * seq_len, 1), 1) < seq_len
    tile_rows = lambda ref: jnp.concatenate([ref[...]] * seqs, axis=0)

    def stages(blk):
        pq = slice(blk.start // (2 * seq_len), blk.stop // (2 * seq_len))

        def state_update(h, k_b, vw_b):
            vw = per_pair(vw_b)
            zero = jnp.zeros_like(vw)
            both = jnp.concatenate([jnp.where(first_of_pair, vw, zero), jnp.where(first_of_pair, zero, vw)],
                                   axis=-1)
            u = jnp.einsum('pjd,pje->pde', per_pair(k_b), both, preferred_element_type=F32)
            st_out_ref[pq, 0, h] = g_seq[h] * st_in_ref[pq, 0, h] + u[:, :, :RET_DV]
            st_out_ref[pq, 1, h] = g_seq[h] * st_in_ref[pq, 1, h] + u[:, :, RET_DV:]
            return None

        def state_output(h, scores_b, qw_b, v_b, _):
            o = jnp.dot(scores_b, v_b, preferred_element_type=F32)
            s_pair = jnp.concatenate([st_in_ref[pq, 0, h], st_in_ref[pq, 1, h]], axis=-1).astype(BF16)
            both = jnp.einsum('pid,pde->pie', per_pair(qw_b), s_pair, preferred_element_type=F32)
            o_inter = jnp.where(first_of_pair, both[:, :, :RET_DV], both[:, :, RET_DV:])
            return o + o_inter.reshape(CHUNK, RET_DV)

        return _chunk_stages(z_ref, blk, lambda: tile_rows(cos_ref), lambda: tile_rows(sin_ref), ln_g_ref,
                             gn_ref, bias_ref, wpm_ref, tabs_ref, state_update, state_output, mixin_ref,
                             vrows_ref)

    def in_pieces(blk):
        return _in_pieces(x_ref[blk, :], norm_pre_ref, w_in_ref, z_ref.at[blk, :])

    def out_pieces(blk):
        return _out_pieces(x_ref[blk, :], p_ref[blk, :], mixin_ref.at[blk, :], w_out_ref, norm_post_ref,
                           w_gate_ref, w_ple_ref, y_ref.at[blk, :])

    rows = BLOCK_ROWS["sample"]
    _emit_blocks(in_pieces, out_pieces, stages, [slice(r, r + rows) for r in range(0, SAMPLE_TILE, rows)],
                 first_in=in_pieces(slice(0, rows)))


def _const_spec(shape):
    return pl.BlockSpec(shape, lambda *_: (0,) * len(shape), pipeline_mode=pl.Buffered(1))


def kernel(x_prompt, x_sample, state_ret, p_prompt, p_sample, w_in, w_out, norm_pre, norm_post, sgu_w, sgu_b,
           sgu_ln, ret_gn, w_ple_proj, w_ple_gate):
    batch, seq, _ = x_prompt.shape
    dec_batch, dec_seq, _ = x_sample.shape
    n_tiles = seq // PROMPT_TILE
    n_tok = dec_batch * dec_seq
    assert seq % PROMPT_TILE == 0 and PROMPT_TILE % BLOCK_ROWS["prompt"] == 0 and w_in.shape[0] == 1
    assert CHUNK % dec_seq == 0 and n_tok % SAMPLE_TILE == 0 and SAMPLE_TILE % BLOCK_ROWS["sample"] == 0
    assert D_MODEL % CAST_STEPS == 0 and PLE_DIM % (16 * CAST_STEPS) == 0
    assert (CHUNK // dec_seq) % 2 == 0 and dec_batch % (SAMPLE_TILE // dec_seq) == 0

    small = (norm_pre[0][None, :], norm_post[0][None, :], sgu_ln[0][None, :], ret_gn[0][None, :],
             sgu_w[0], sgu_b[0])
    small_specs = [_const_spec((1, D_MODEL)), _const_spec((1, D_MODEL)), _const_spec((1, SGU_WIDTH)),
                   _const_spec((1, RET_HEADS * RET_DV)), _const_spec((SGU_GROUPS, CHUNK, CHUNK)),
                   _const_spec((SGU_GROUPS, CHUNK)), _const_spec((CHUNK, 2 * CHUNK)),
                   _const_spec((3, RET_HEADS, CHUNK, LANES))]
    table_scratch = [pltpu.VMEM((SGU_GROUPS // 2, CHUNK, 2 * CHUNK), BF16), pltpu.VMEM((CHUNK, SGU_WIDTH), F32)]
    weight_shapes = [(D_MODEL, IN_WIDTH), (D_MODEL, D_MODEL), (D_MODEL, D_MODEL), (PLE_DIM, D_MODEL)]

    tabs_p, g_chunk = _retention_tables(CHUNK)
    cos_p, sin_p = _rotary_tables(np.arange(seq))
    tile_of = lambda s: jnp.maximum(s - CAST_STEPS, 0)
    tile_spec = lambda width: pl.BlockSpec((1, PROMPT_TILE, width),
                                           lambda s: (tile_of(s) // n_tiles, tile_of(s) % n_tiles, 0))
    pos_spec = pl.BlockSpec((PROMPT_TILE, LANES), lambda s: (tile_of(s) % n_tiles, 0))
    blocks_per_tile = PROMPT_TILE // BLOCK_ROWS["prompt"]
    next_tile = lambda s: jnp.minimum(jnp.maximum(s - CAST_STEPS + 1, 0), batch * n_tiles - 1)
    next_spec = pl.BlockSpec((1, BLOCK_ROWS["prompt"], D_MODEL),
                             lambda s: (next_tile(s) // n_tiles, (next_tile(s) % n_tiles) * blocks_per_tile, 0))
    cast_spec = lambda shape: pl.BlockSpec((shape[0] // CAST_STEPS, shape[1]),
                                           lambda s: (jnp.minimum(s, CAST_STEPS - 1), 0))
    y_prompt, st_prompt, w_in_b, w_out_b, w_gate_b, w_ple_b = pl.pallas_call(
        functools.partial(_prompt_kernel, g_chunk, n_tiles),
        grid=(CAST_STEPS + batch * n_tiles,),
        in_specs=[tile_spec(D_MODEL), next_spec, tile_spec(PLE_DIM), pos_spec, pos_spec]
        + [cast_spec(shape) for shape in weight_shapes] + small_specs,
        out_specs=[tile_spec(D_MODEL),
                   pl.BlockSpec((1, RET_HEADS, RET_DK, RET_DV), lambda s: (tile_of(s) // n_tiles, 0, 0, 0))]
        + [pl.BlockSpec(shape, lambda s: (0, 0)) for shape in weight_shapes],
        out_shape=[jax.ShapeDtypeStruct((batch, seq, D_MODEL), F32),
                   jax.ShapeDtypeStruct((batch, RET_HEADS, RET_DK, RET_DV), F32)]
        + [jax.ShapeDtypeStruct(shape, BF16) for shape in weight_shapes],
        scratch_shapes=[pltpu.VMEM((PROMPT_TILE, IN_WIDTH), F32), pltpu.VMEM((BLOCK_ROWS["prompt"], IN_WIDTH), F32),
                        pltpu.VMEM((PROMPT_TILE, D_MODEL), BF16)] + table_scratch,
        compiler_params=pltpu.CompilerParams(dimension_semantics=("arbitrary",),
                                             vmem_limit_bytes=VMEM_LIMIT_BYTES),
        name="prompt_layer",
    )(x_prompt, x_prompt, p_prompt[0], cos_p, sin_p, w_in[0], w_out[0], w_ple_gate[0], w_ple_proj[0], *small,
      _mix_mask(CHUNK), tabs_p)

    tabs_s, g_seq = _retention_tables(dec_seq)
    cos_s, sin_s = _rotary_tables(PAST_LEN + np.arange(dec_seq))
    tile_seqs = SAMPLE_TILE // dec_seq
    tok_spec = lambda width: pl.BlockSpec((SAMPLE_TILE, width), lambda i: (i, 0))
    state_spec = pl.BlockSpec((tile_seqs // 2, 2, RET_HEADS, RET_DK, RET_DV), lambda i: (i, 0, 0, 0, 0))
    paired = (dec_batch // 2, 2, RET_HEADS, RET_DK, RET_DV)
    y_sample, st_sample, v_sample = pl.pallas_call(
        functools.partial(_sample_kernel, g_seq),
        grid=(n_tok // SAMPLE_TILE,),
        in_specs=[tok_spec(D_MODEL), tok_spec(PLE_DIM), _const_spec((dec_seq, LANES)),
                  _const_spec((dec_seq, LANES)), state_spec]
        + [_const_spec(shape) for shape in weight_shapes] + small_specs,
        out_specs=[tok_spec(D_MODEL), state_spec, tok_spec(SGU_WIDTH)],
        out_shape=[jax.ShapeDtypeStruct((n_tok, D_MODEL), F32),
                   jax.ShapeDtypeStruct(paired, F32), jax.ShapeDtypeStruct((n_tok, SGU_WIDTH), F32)],
        scratch_shapes=[pltpu.VMEM((SAMPLE_TILE, IN_WIDTH), F32), pltpu.VMEM((SAMPLE_TILE, D_MODEL), BF16)]
        + table_scratch,
        compiler_params=pltpu.CompilerParams(dimension_semantics=("arbitrary",),
                                             vmem_limit_bytes=VMEM_LIMIT_BYTES),
        name="sample_layer",
    )(x_sample.reshape(n_tok, D_MODEL), p_sample[0].reshape(n_tok, PLE_DIM), cos_s, sin_s,
      state_ret[0].reshape(paired),
      w_in_b, w_out_b, w_gate_b, w_ple_b, *small, _mix_mask(dec_seq), tabs_s)

    return (y_prompt, y_sample.reshape(dec_batch, dec_seq, D_MODEL), st_prompt[None],
            st_sample.reshape(state_ret.shape),
            v_sample.reshape(1, dec_batch, dec_seq, SGU_WIDTH))
```

```python
import functools

import numpy as np
import jax
import jax.numpy as jnp
from jax import lax
from jax.experimental import pallas as pl
from jax.experimental.pallas import tpu as pltpu

F32 = jnp.float32
BF16 = jnp.bfloat16

D_MODEL = 1024
PAST_LEN = 16384
SGU_WIDTH = 512
SGU_GROUPS = 8
SGU_GROUP_DIM = SGU_WIDTH // SGU_GROUPS
RET_HEADS = 4
RET_DK = 128
RET_DV = 128
CHUNK = 128
ROPE_THETA = 10000.0
PLE_DIM = 256
RMS_EPS = 1e-6
LN_EPS = 1e-5
IN_WIDTH = 3 * SGU_WIDTH + RET_HEADS * (2 * RET_DK + 2 * RET_DV)
O_SU, O_SV, O_SG = 0, SGU_WIDTH, 2 * SGU_WIDTH
O_Q = 3 * SGU_WIDTH
O_K = O_Q + RET_HEADS * RET_DK
O_V = O_K + RET_HEADS * RET_DK
O_RG = O_V + RET_HEADS * RET_DV

LANES = 128
VMEM_LIMIT_BYTES = 60 * 1024 * 1024

PROMPT_TILE = 1024
BLOCK_ROWS = {"prompt": 256, "sample": CHUNK}
PIECE_COLS = 512
CAST_STEPS = 8
SAMPLE_TILE = 2 * CHUNK


def _log_gamma():
    return np.log(1.0 - 2.0 ** (-5.0 - np.arange(RET_HEADS, dtype=np.float64)))


def _retention_tables(seg):
    lg = _log_gamma()[:, None, None]
    r = np.arange(CHUNK)
    i, j = r[:, None] % seg, r[None, :] % seg
    same = (r[:, None] // seg) == (r[None, :] // seg)
    decay = np.where(same & (i >= j), np.exp(lg * np.maximum(i - j, 0)), 0.0)
    wq = np.broadcast_to(np.exp(lg * (i + 1.0)), (RET_HEADS, CHUNK, LANES))
    wkv = np.broadcast_to(np.exp(lg * (seg - 1.0 - i)), (RET_HEADS, CHUNK, LANES))
    scale = RET_DK ** -0.5
    tabs = np.stack([decay * scale, wq, wkv * scale]).astype(np.float32)
    g_seg = [float(v) for v in np.exp(_log_gamma() * seg)]
    return tabs, g_seg


def _rotary_tables(pos):
    half = RET_DK // 2
    inv = ROPE_THETA ** (-np.arange(half, dtype=np.float64) / half)
    ang = pos.astype(np.float64)[:, None] * inv[None, :]
    cos, sin = np.cos(ang), np.sin(ang)
    return (np.concatenate([cos, cos], axis=1).astype(np.float32),
            np.concatenate([-sin, sin], axis=1).astype(np.float32))


def _mix_mask(seg):
    r = np.arange(CHUNK)
    same = (r[:, None] // seg) == (r[None, :] // seg)
    m = same & ((r[None, :] % seg) <= (r[:, None] % seg))
    return np.concatenate([m, m], axis=1).astype(np.float32)


def _gelu(x):
    c = float(np.sqrt(2.0 / np.pi))
    half = 0.5 * x
    return half + half * jnp.tanh(x * (c + (c * 0.044715) * (x * x)))


def _silu(x):
    return x * jax.nn.sigmoid(x)


def _rotate(x, cos, sin_signed):
    return x * cos + pltpu.roll(x, RET_DK // 2, 1) * sin_signed


def _normed_bf16(x, norm_ref):
    ms = jnp.mean(x * x, axis=-1, keepdims=True)
    return (x * lax.rsqrt(ms + RMS_EPS) * norm_ref[...]).astype(BF16)


def _in_pieces(x, norm_pre_ref, w_in_ref, z_ref):
    box = {}

    def piece(c0):
        def run():
            if 'h' not in box:
                box['h'] = _normed_bf16(x, norm_pre_ref)
            z_ref[:, c0:c0 + PIECE_COLS] = jnp.dot(box['h'], w_in_ref[:, c0:c0 + PIECE_COLS],
                                                   preferred_element_type=F32)
        return run
    return [piece(c0) for c0 in range(0, IN_WIDTH, PIECE_COLS)]


def _out_pieces(x, p, mixin_ref, w_out_ref, norm_post_ref, w_gate_ref, w_ple_ref, y_ref):
    box = {}
    halves = [slice(c0, c0 + PIECE_COLS) for c0 in range(0, D_MODEL, PIECE_COLS)]

    def ple():
        box['ple'] = jnp.dot(p.astype(BF16), w_ple_ref[...], preferred_element_type=F32)

    def mix(i):
        def run():
            box['mix%d' % i] = jnp.dot(mixin_ref[...], w_out_ref[:, halves[i]], preferred_element_type=F32)
        return run

    def gate(i):
        def run():
            if 'x1' not in box:
                mixes = [box['mix%d' % j] for j in range(len(halves))]
                ms = sum(jnp.sum(m * m, axis=-1, keepdims=True) for m in mixes) * (1.0 / D_MODEL)
                scale = lax.rsqrt(ms + RMS_EPS)
                box['x1'] = [x[:, halves[j]] + mixes[j] * scale * norm_post_ref[:, halves[j]]
                             for j in range(len(halves))]
                box['x1b'] = jnp.concatenate(box['x1'], axis=1).astype(BF16)
            g = jnp.dot(box['x1b'], w_gate_ref[:, halves[i]], preferred_element_type=F32)
            y_ref[:, halves[i]] = box['x1'][i] + jax.nn.sigmoid(g) * box['ple'][:, halves[i]]
        return run
    return [ple] + [mix(i) for i in range(len(halves))] + [gate(i) for i in range(len(halves))]


def _chunk_stages(z_ref, rows, cos, sin_signed, ln_g_ref, gn_ref, bias_ref, wpair_ref, tabs_ref,
                  state_update, state_output, mixin_ref, vrows_ref):
    box = {}
    zs = lambda off, i, width: z_ref[rows, off + i * width:off + (i + 1) * width]

    def gating_in():
        sv = _gelu(z_ref[rows, O_SV:O_SV + SGU_WIDTH])
        mu = jnp.mean(sv, axis=-1, keepdims=True)
        cen = sv - mu
        var = jnp.mean(cen * cen, axis=-1, keepdims=True)
        vn = cen * lax.rsqrt(var + LN_EPS) * ln_g_ref[...]
        if vrows_ref is not None:
            vrows_ref[rows, :] = vn
        low_lanes = lax.broadcasted_iota(jnp.int32, (CHUNK, LANES), 1) < SGU_GROUP_DIM
        for m in range(SGU_GROUPS // 2):
            blk = vn[:, m * LANES:(m + 1) * LANES]
            rhs = jnp.concatenate([jnp.where(low_lanes, blk, 0.0), jnp.where(low_lanes, 0.0, blk)],
                                  axis=0).astype(BF16)
            box['mixed%d' % m] = jnp.dot(wpair_ref[m], rhs, preferred_element_type=F32)

    def retention_in(heads):
        c, s = cos(), sin_signed()
        for h in heads:
            q = _rotate(zs(O_Q, h, RET_DK), c, s)
            k = _rotate(zs(O_K, h, RET_DK), c, s)
            v = zs(O_V, h, RET_DV)
            k_b = k.astype(BF16)
            box['scores%d' % h] = lax.dot_general(q.astype(BF16), k_b, (((1,), (1,)), ((), ())),
                                                  preferred_element_type=F32)
            box['qw%d' % h] = (q * tabs_ref[1, h]).astype(BF16)
            box['v%d' % h] = v.astype(BF16)
            box['u%d' % h] = state_update(h, k_b, (v * tabs_ref[2, h]).astype(BF16))

    def gating_out():
        for m in range(SGU_GROUPS // 2):
            cols = slice(m * LANES, (m + 1) * LANES)
            mixed = box['mixed%d' % m] + bias_ref[:, cols]
            su = _gelu(zs(O_SU, m, LANES))
            mixin_ref[rows, cols] = (_silu(zs(O_SG, m, LANES)) * (su * mixed)).astype(BF16)

    def retention_mid(heads):
        for h in heads:
            scores_b = (box['scores%d' % h] * tabs_ref[0, h]).astype(BF16)
            box['o%d' % h] = state_output(h, scores_b, box['qw%d' % h], box['v%d' % h], box['u%d' % h])

    def retention_out(heads):
        for h in heads:
            o = box['o%d' % h]
            mu = jnp.mean(o, axis=-1, keepdims=True)
            cen = o - mu
            var = jnp.mean(cen * cen, axis=-1, keepdims=True)
            on = cen * lax.rsqrt(var + LN_EPS) * gn_ref[:, h * RET_DV:(h + 1) * RET_DV]
            mixin_ref[rows, SGU_WIDTH + h * RET_DV:SGU_WIDTH + (h + 1) * RET_DV] = (
                _silu(zs(O_RG, h, RET_DV)) * on).astype(BF16)

    lo, hi = range(0, RET_HEADS // 2), range(RET_HEADS // 2, RET_HEADS)
    part = functools.partial
    return [gating_in, part(retention_in, lo), part(retention_in, hi), gating_out, part(retention_mid, lo),
            part(retention_mid, hi), part(retention_out, lo), part(retention_out, hi)]


def _interleave(pieces, stages):
    slots = max(len(stages), 1)
    done = 0
    for i in range(slots):
        upto = -(-(i + 1) * len(pieces) // slots)
        for piece in pieces[done:upto]:
            piece()
        done = upto
        if i < len(stages):
            stages[i]()


def _emit_blocks(in_pieces, out_pieces, stages, blocks, first_in=None, next_in=None):
    _interleave(first_in or [], [])
    for i, blk in enumerate(blocks):
        nxt = in_pieces(blocks[i + 1]) if i + 1 < len(blocks) else (next_in or [])
        if i > 0:
            ple, mix0, mix1, gate0, gate1 = out_pieces(blocks[i - 1])
            pieces = [ple, mix0, mix1] + nxt[:3] + [gate0] + nxt[3:4] + [gate1] + nxt[4:]
        else:
            pieces = nxt
        _interleave(pieces, stages(blk))
    _interleave(out_pieces(blocks[-1]), [])


def _init_tables(seg, sgu_w_ref, sgu_b_ref, mask_ref, wpm_ref, bias_ref):
    rows = lax.broadcasted_iota(jnp.int32, (CHUNK, LANES), 0)
    lanes = lax.broadcasted_iota(jnp.int32, (CHUNK, LANES), 1)
    if seg == CHUNK:
        group = lambda g: sgu_w_ref[g]
        bias_rows = sgu_b_ref[...]
    else:
        reps = CHUNK // seg
        select = jnp.where((rows < seg) & (lanes % seg == rows), 1.0, 0.0).astype(BF16)
        first = lambda a: jnp.where(lanes[:seg] < seg, a, 0.0)

        def group(g):
            stacked = jnp.concatenate([first(sgu_w_ref[g, 0:seg, :])] * reps, axis=0)
            return jnp.dot(stacked.astype(BF16), select, preferred_element_type=F32)
        b_first = jnp.where(lax.broadcasted_iota(jnp.int32, (SGU_GROUPS, LANES), 1) < seg, sgu_b_ref[...], 0.0)
        bias_rows = sum(pltpu.roll(b_first, r * seg, 1) for r in range(reps))
    for m in range(SGU_GROUPS // 2):
        pair = jnp.concatenate([group(2 * m), group(2 * m + 1)], axis=1)
        wpm_ref[m] = jnp.where(mask_ref[...] > 0.0, pair, 0.0).astype(BF16)
    group_of_lane = lax.broadcasted_iota(jnp.int32, (SGU_GROUPS, SGU_WIDTH), 1) // SGU_GROUP_DIM
    expand = jnp.where(group_of_lane == lax.broadcasted_iota(jnp.int32, (SGU_GROUPS, SGU_WIDTH), 0),
                       1.0, 0.0).astype(BF16)
    total, rest = jnp.zeros((CHUNK, SGU_WIDTH), F32), bias_rows
    for _ in range(3):
        term = rest.astype(BF16)
        total = total + lax.dot_general(term, expand, (((0,), (0,)), ((), ())), preferred_element_type=F32)
        rest = rest - term.astype(F32)
    bias_ref[...] = total


def _prompt_kernel(g_chunk, n_tiles, x_ref, x_next_ref, p_ref, cos_ref, sin_ref, w_in_ref, w_out_ref, w_gate_ref,
                   w_ple_ref, norm_pre_ref, norm_post_ref, ln_g_ref, gn_ref, sgu_w_ref, sgu_b_ref, mask_ref,
                   tabs_ref, y_ref, state_ref, w_in_b_ref, w_out_b_ref, w_gate_b_ref, w_ple_b_ref,
                   z_ref, z_first_ref, mixin_ref, wpm_ref, bias_ref):
    step = pl.program_id(0)
    first = slice(0, BLOCK_ROWS["prompt"])

    def next_first_in():
        return _in_pieces(x_next_ref[0], norm_pre_ref, w_in_b_ref, z_first_ref)

    @pl.when(step == 0)
    def _():
        _init_tables(CHUNK, sgu_w_ref, sgu_b_ref, mask_ref, wpm_ref, bias_ref)

    @pl.when(step < CAST_STEPS)
    def _():
        for src, dst in ((w_in_ref, w_in_b_ref), (w_out_ref, w_out_b_ref), (w_gate_ref, w_gate_b_ref),
                         (w_ple_ref, w_ple_b_ref)):
            n = src.shape[0]
            dst[pl.ds(pl.multiple_of(step * n, n), n), :] = src[...].astype(BF16)

    @pl.when(step == CAST_STEPS - 1)
    def _():
        _interleave(next_first_in(), [])

    @pl.when(step >= CAST_STEPS)
    def _():
        @pl.when((step - CAST_STEPS) % n_tiles == 0)
        def _():
            state_ref[...] = jnp.zeros_like(state_ref)

        def state_update(h, k_b, vw_b):
            return lax.dot_general(k_b, vw_b, (((0,), (0,)), ((), ())), preferred_element_type=F32)

        def state_output(h, scores_b, qw_b, v_b, u):
            s = state_ref[0, h]
            o = jnp.dot(jnp.concatenate([scores_b, qw_b], axis=1),
                        jnp.concatenate([v_b, s.astype(BF16)], axis=0), preferred_element_type=F32)
            state_ref[0, h] = g_chunk[h] * s + u
            return o

        def in_pieces(blk):
            return _in_pieces(x_ref[0, blk, :], norm_pre_ref, w_in_b_ref, z_ref.at[blk, :])

        def out_pieces(blk):
            return _out_pieces(x_ref[0, blk, :], p_ref[0, blk, :], mixin_ref.at[blk, :], w_out_b_ref,
                               norm_post_ref, w_gate_b_ref, w_ple_b_ref, y_ref.at[0, blk, :])

        def stages(blk):
            out = []
            for r in range(blk.start, blk.stop, CHUNK):
                rows = slice(r, r + CHUNK)
                out += _chunk_stages(z_first_ref if blk == first else z_ref, rows,
                                     lambda rows=rows: cos_ref[rows, :],
                                     lambda rows=rows: sin_ref[rows, :], ln_g_ref, gn_ref, bias_ref, wpm_ref,
                                     tabs_ref, state_update, state_output, mixin_ref, None)
            return out

        rows = BLOCK_ROWS["prompt"]
        _emit_blocks(in_pieces, out_pieces, stages, [slice(r, r + rows) for r in range(0, PROMPT_TILE, rows)],
                     next_in=next_first_in())


def _sample_kernel(g_seq, x_ref, p_ref, cos_ref, sin_ref, st_in_ref, w_in_ref, w_out_ref, w_gate_ref,
                   w_ple_ref, norm_pre_ref, norm_post_ref, ln_g_ref, gn_ref, sgu_w_ref, sgu_b_ref, mask_ref,
                   tabs_ref, y_ref, st_out_ref, vrows_ref, z_ref, mixin_ref, wpm_ref, bias_ref):
    seq_len = cos_ref.shape[0]
    seqs = CHUNK // seq_len

    @pl.when(pl.program_id(0) == 0)
    def _():
        _init_tables(seq_len, sgu_w_ref, sgu_b_ref, mask_ref, wpm_ref, bias_ref)

    pairs = seqs // 2
    per_pair = lambda a: a.reshape(pairs, 2 * seq_len, a.shape[-1])
    first_of_pair = lax.broadcasted_iota(jnp.int32, (1, 2 * seq_len, 1), 1) < seq_len
    tile_rows = lambda ref: jnp.concatenate([ref[...]] * seqs, axis=0)

    def stages(blk):
        pq = slice(blk.start // (2 * seq_len), blk.stop // (2 * seq_len))

        def state_update(h, k_b, vw_b):
            vw = per_pair(vw_b)
            zero = jnp.zeros_like(vw)
            both = jnp.concatenate([jnp.where(first_of_pair, vw, zero), jnp.where(first_of_pair, zero, vw)],
                                   axis=-1)
            u = jnp.einsum('pjd,pje->pde', per_pair(k_b), both, preferred_element_type=F32)
            st_out_ref[pq, 0, h] = g_seq[h] * st_in_ref[pq, 0, h] + u[:, :, :RET_DV]
            st_out_ref[pq, 1, h] = g_seq[h] * st_in_ref[pq, 1, h] + u[:, :, RET_DV:]
            return None

        def state_output(h, scores_b, qw_b, v_b, _):
            o = jnp.dot(scores_b, v_b, preferred_element_type=F32)
            s_pair = jnp.concatenate([st_in_ref[pq, 0, h], st_in_ref[pq, 1, h]], axis=-1).astype(BF16)
            both = jnp.einsum('pid,pde->pie', per_pair(qw_b), s_pair, preferred_element_type=F32)
            o_inter = jnp.where(first_of_pair, both[:, :, :RET_DV], both[:, :, RET_DV:])
            return o + o_inter.reshape(CHUNK, RET_DV)

        return _chunk_stages(z_ref, blk, lambda: tile_rows(cos_ref), lambda: tile_rows(sin_ref), ln_g_ref,
                             gn_ref, bias_ref, wpm_ref, tabs_ref, state_update, state_output, mixin_ref,
                             vrows_ref)

    def in_pieces(blk):
        return _in_pieces(x_ref[blk, :], norm_pre_ref, w_in_ref, z_ref.at[blk, :])

    def out_pieces(blk):
        return _out_pieces(x_ref[blk, :], p_ref[blk, :], mixin_ref.at[blk, :], w_out_ref, norm_post_ref,
                           w_gate_ref, w_ple_ref, y_ref.at[blk, :])

    rows = BLOCK_ROWS["sample"]
    _emit_blocks(in_pieces, out_pieces, stages, [slice(r, r + rows) for r in range(0, SAMPLE_TILE, rows)],
                 first_in=in_pieces(slice(0, rows)))


def _const_spec(shape):
    return pl.BlockSpec(shape, lambda *_: (0,) * len(shape), pipeline_mode=pl.Buffered(1))


def kernel(x_prompt, x_sample, state_ret, p_prompt, p_sample, w_in, w_out, norm_pre, norm_post, sgu_w, sgu_b,
           sgu_ln, ret_gn, w_ple_proj, w_ple_gate):
    batch, seq, _ = x_prompt.shape
    dec_batch, dec_seq, _ = x_sample.shape
    n_tiles = seq // PROMPT_TILE
    n_tok = dec_batch * dec_seq
    assert seq % PROMPT_TILE == 0 and PROMPT_TILE % BLOCK_ROWS["prompt"] == 0 and w_in.shape[0] == 1
    assert CHUNK % dec_seq == 0 and n_tok % SAMPLE_TILE == 0 and SAMPLE_TILE % BLOCK_ROWS["sample"] == 0
    assert D_MODEL % CAST_STEPS == 0 and PLE_DIM % (16 * CAST_STEPS) == 0
    assert (CHUNK // dec_seq) % 2 == 0 and dec_batch % (SAMPLE_TILE // dec_seq) == 0

    small = (norm_pre[0][None, :], norm_post[0][None, :], sgu_ln[0][None, :], ret_gn[0][None, :],
             sgu_w[0], sgu_b[0])
    small_specs = [_const_spec((1, D_MODEL)), _const_spec((1, D_MODEL)), _const_spec((1, SGU_WIDTH)),
                   _const_spec((1, RET_HEADS * RET_DV)), _const_spec((SGU_GROUPS, CHUNK, CHUNK)),
                   _const_spec((SGU_GROUPS, CHUNK)), _const_spec((CHUNK, 2 * CHUNK)),
                   _const_spec((3, RET_HEADS, CHUNK, LANES))]
    table_scratch = [pltpu.VMEM((SGU_GROUPS // 2, CHUNK, 2 * CHUNK), BF16), pltpu.VMEM((CHUNK, SGU_WIDTH), F32)]
    weight_shapes = [(D_MODEL, IN_WIDTH), (D_MODEL, D_MODEL), (D_MODEL, D_MODEL), (PLE_DIM, D_MODEL)]

    tabs_p, g_chunk = _retention_tables(CHUNK)
    cos_p, sin_p = _rotary_tables(np.arange(seq))
    tile_of = lambda s: jnp.maximum(s - CAST_STEPS, 0)
    tile_spec = lambda width: pl.BlockSpec((1, PROMPT_TILE, width),
                                           lambda s: (tile_of(s) // n_tiles, tile_of(s) % n_tiles, 0))
    pos_spec = pl.BlockSpec((PROMPT_TILE, LANES), lambda s: (tile_of(s) % n_tiles, 0))
    blocks_per_tile = PROMPT_TILE // BLOCK_ROWS["prompt"]
    next_tile = lambda s: jnp.minimum(jnp.maximum(s - CAST_STEPS + 1, 0), batch * n_tiles - 1)
    next_spec = pl.BlockSpec((1, BLOCK_ROWS["prompt"], D_MODEL),
                             lambda s: (next_tile(s) // n_tiles, (next_tile(s) % n_tiles) * blocks_per_tile, 0))
    cast_spec = lambda shape: pl.BlockSpec((shape[0] // CAST_STEPS, shape[1]),
                                           lambda s: (jnp.minimum(s, CAST_STEPS - 1), 0))
    y_prompt, st_prompt, w_in_b, w_out_b, w_gate_b, w_ple_b = pl.pallas_call(
        functools.partial(_prompt_kernel, g_chunk, n_tiles),
        grid=(CAST_STEPS + batch * n_tiles,),
        in_specs=[tile_spec(D_MODEL), next_spec, tile_spec(PLE_DIM), pos_spec, pos_spec]
        + [cast_spec(shape) for shape in weight_shapes] + small_specs,
        out_specs=[tile_spec(D_MODEL),
                   pl.BlockSpec((1, RET_HEADS, RET_DK, RET_DV), lambda s: (tile_of(s) // n_tiles, 0, 0, 0))]
        + [pl.BlockSpec(shape, lambda s: (0, 0)) for shape in weight_shapes],
        out_shape=[jax.ShapeDtypeStruct((batch, seq, D_MODEL), F32),
                   jax.ShapeDtypeStruct((batch, RET_HEADS, RET_DK, RET_DV), F32)]
        + [jax.ShapeDtypeStruct(shape, BF16) for shape in weight_shapes],
        scratch_shapes=[pltpu.VMEM((PROMPT_TILE, IN_WIDTH), F32), pltpu.VMEM((BLOCK_ROWS["prompt"], IN_WIDTH), F32),
                        pltpu.VMEM((PROMPT_TILE, D_MODEL), BF16)] + table_scratch,
        compiler_params=pltpu.CompilerParams(dimension_semantics=("arbitrary",),
                                             vmem_limit_bytes=VMEM_LIMIT_BYTES),
        name="prompt_layer",
    )(x_prompt, x_prompt, p_prompt[0], cos_p, sin_p, w_in[0], w_out[0], w_ple_gate[0], w_ple_proj[0], *small,
      _mix_mask(CHUNK), tabs_p)

    tabs_s, g_seq = _retention_tables(dec_seq)
    cos_s, sin_s = _rotary_tables(PAST_LEN + np.arange(dec_seq))
    tile_seqs = SAMPLE_TILE // dec_seq
    tok_spec = lambda width: pl.BlockSpec((SAMPLE_TILE, width), lambda i: (i, 0))
    state_spec = pl.BlockSpec((tile_seqs // 2, 2, RET_HEADS, RET_DK, RET_DV), lambda i: (i, 0, 0, 0, 0))
    paired = (dec_batch // 2, 2, RET_HEADS, RET_DK, RET_DV)
    y_sample, st_sample, v_sample = pl.pallas_call(
        functools.partial(_sample_kernel, g_seq),
        grid=(n_tok // SAMPLE_TILE,),
        in_specs=[tok_spec(D_MODEL), tok_spec(PLE_DIM), _const_spec((dec_seq, LANES)),
                  _const_spec((dec_seq, LANES)), state_spec]
        + [_const_spec(shape) for shape in weight_shapes] + small_specs,
        out_specs=[tok_spec(D_MODEL), state_spec, tok_spec(SGU_WIDTH)],
        out_shape=[jax.ShapeDtypeStruct((n_tok, D_MODEL), F32),
                   jax.ShapeDtypeStruct(paired, F32), jax.ShapeDtypeStruct((n_tok, SGU_WIDTH), F32)],
        scratch_shapes=[pltpu.VMEM((SAMPLE_TILE, IN_WIDTH), F32), pltpu.VMEM((SAMPLE_TILE, D_MODEL), BF16)]
        + table_scratch,
        compiler_params=pltpu.CompilerParams(dimension_semantics=("arbitrary",),
                                             vmem_limit_bytes=VMEM_LIMIT_BYTES),
        name="sample_layer",
    )(x_sample.reshape(n_tok, D_MODEL), p_sample[0].reshape(n_tok, PLE_DIM), cos_s, sin_s,
      state_ret[0].reshape(paired),
      w_in_b, w_out_b, w_gate_b, w_ple_b, *small, _mix_mask(dec_seq), tabs_s)

    return (y_prompt, y_sample.reshape(dec_batch, dec_seq, D_MODEL), st_prompt[None],
            st_sample.reshape(state_ret.shape),
            v_sample.reshape(1, dec_batch, dec_seq, SGU_WIDTH))
```

```python
import functools

import numpy as np
import jax
import jax.numpy as jnp
from jax import lax
from jax.experimental import pallas as pl
from jax.experimental.pallas import tpu as pltpu

F32 = jnp.float32
BF16 = jnp.bfloat16

D_MODEL = 1024
PAST_LEN = 16384
SGU_WIDTH = 512
SGU_GROUPS = 8
SGU_GROUP_DIM = SGU_WIDTH // SGU_GROUPS
RET_HEADS = 4
RET_DK = 128
RET_DV = 128
CHUNK = 128
ROPE_THETA = 10000.0
PLE_DIM = 256
RMS_EPS = 1e-6
LN_EPS = 1e-5
IN_WIDTH = 3 * SGU_WIDTH + RET_HEADS * (2 * RET_DK + 2 * RET_DV)
O_SU, O_SV, O_SG = 0, SGU_WIDTH, 2 * SGU_WIDTH
O_Q = 3 * SGU_WIDTH
O_K = O_Q + RET_HEADS * RET_DK
O_V = O_K + RET_HEADS * RET_DK
O_RG = O_V + RET_HEADS * RET_DV

LANES = 128
VMEM_LIMIT_BYTES = 60 * 1024 * 1024

PROMPT_TILE = 1024
BLOCK_ROWS = {"prompt": 256, "sample": CHUNK}
PIECE_COLS = 512
CAST_STEPS = 8
SAMPLE_TILE = 2 * CHUNK


def _log_gamma():
    return np.log(1.0 - 2.0 ** (-5.0 - np.arange(RET_HEADS, dtype=np.float64)))


def _retention_tables(seg):
    lg = _log_gamma()[:, None, None]
    r = np.arange(CHUNK)
    i, j = r[:, None] % seg, r[None, :] % seg
    same = (r[:, None] // seg) == (r[None, :] // seg)
    decay = np.where(same & (i >= j), np.exp(lg * np.maximum(i - j, 0)), 0.0)
    wq = np.broadcast_to(np.exp(lg * (i + 1.0)), (RET_HEADS, CHUNK, LANES))
    wkv = np.broadcast_to(np.exp(lg * (seg - 1.0 - i)), (RET_HEADS, CHUNK, LANES))
    scale = RET_DK ** -0.5
    tabs = np.stack([decay * scale, wq, wkv * scale]).astype(np.float32)
    g_seg = [float(v) for v in np.exp(_log_gamma() * seg)]
    return tabs, g_seg


def _rotary_tables(pos):
    half = RET_DK // 2
    inv = ROPE_THETA ** (-np.arange(half, dtype=np.float64) / half)
    ang = pos.astype(np.float64)[:, None] * inv[None, :]
    cos, sin = np.cos(ang), np.sin(ang)
    return (np.concatenate([cos, cos], axis=1).astype(np.float32),
            np.concatenate([-sin, sin], axis=1).astype(np.float32))


def _mix_mask(seg):
    r = np.arange(CHUNK)
    same = (r[:, None] // seg) == (r[None, :] // seg)
    m = same & ((r[None, :] % seg) <= (r[:, None] % seg))
    return np.concatenate([m, m], axis=1).astype(np.float32)


def _gelu(x):
    c = float(np.sqrt(2.0 / np.pi))
    half = 0.5 * x
    return half + half * jnp.tanh(x * (c + (c * 0.044715) * (x * x)))


def _silu(x):
    return x * jax.nn.sigmoid(x)


def _rotate(x, cos, sin_signed):
    return x * cos + pltpu.roll(x, RET_DK // 2, 1) * sin_signed


def _normed_bf16(x, norm_ref):
    ms = jnp.mean(x * x, axis=-1, keepdims=True)
    return (x * lax.rsqrt(ms + RMS_EPS) * norm_ref[...]).astype(BF16)


def _in_pieces(x, norm_pre_ref, w_in_ref, z_ref):
    box = {}

    def piece(c0):
        def run():
            if 'h' not in box:
                box['h'] = _normed_bf16(x, norm_pre_ref)
            z_ref[:, c0:c0 + PIECE_COLS] = jnp.dot(box['h'], w_in_ref[:, c0:c0 + PIECE_COLS],
                                                   preferred_element_type=F32)
        return run
    return [piece(c0) for c0 in range(0, IN_WIDTH, PIECE_COLS)]


def _out_pieces(x, p, mixin_ref, w_out_ref, norm_post_ref, w_gate_ref, w_ple_ref, y_ref):
    box = {}
    halves = [slice(c0, c0 + PIECE_COLS) for c0 in range(0, D_MODEL, PIECE_COLS)]

    def ple():
        box['ple'] = jnp.dot(p.astype(BF16), w_ple_ref[...], preferred_element_type=F32)

    def mix(i):
        def run():
            box['mix%d' % i] = jnp.dot(mixin_ref[...], w_out_ref[:, halves[i]], preferred_element_type=F32)
        return run

    def gate(i):
        def run():
            if 'x1' not in box:
                mixes = [box['mix%d' % j] for j in range(len(halves))]
                ms = sum(jnp.sum(m * m, axis=-1, keepdims=True) for m in mixes) * (1.0 / D_MODEL)
                scale = lax.rsqrt(ms + RMS_EPS)
                box['x1'] = [x[:, halves[j]] + mixes[j] * scale * norm_post_ref[:, halves[j]]
                             for j in range(len(halves))]
                box['x1b'] = jnp.concatenate(box['x1'], axis=1).astype(BF16)
            g = jnp.dot(box['x1b'], w_gate_ref[:, halves[i]], preferred_element_type=F32)
            y_ref[:, halves[i]] = box['x1'][i] + jax.nn.sigmoid(g) * box['ple'][:, halves[i]]
        return run
    return [ple] + [mix(i) for i in range(len(halves))] + [gate(i) for i in range(len(halves))]


def _chunk_stages(z_ref, rows, cos, sin_signed, ln_g_ref, gn_ref, bias_ref, wpair_ref, tabs_ref,
                  state_update, state_output, mixin_ref, vrows_ref, head_groups):
    box = {}
    zs = lambda off, i, width: z_ref[rows, off + i * width:off + (i + 1) * width]

    def gating_in():
        sv = _gelu(z_ref[rows, O_SV:O_SV + SGU_WIDTH])
        mu = jnp.mean(sv, axis=-1, keepdims=True)
        cen = sv - mu
        var = jnp.mean(cen * cen, axis=-1, keepdims=True)
        vn = cen * lax.rsqrt(var + LN_EPS) * ln_g_ref[...]
        if vrows_ref is not None:
            vrows_ref[rows, :] = vn
        low_lanes = lax.broadcasted_iota(jnp.int32, (CHUNK, LANES), 1) < SGU_GROUP_DIM
        for m in range(SGU_GROUPS // 2):
            blk = vn[:, m * LANES:(m + 1) * LANES]
            rhs = jnp.concatenate([jnp.where(low_lanes, blk, 0.0), jnp.where(low_lanes, 0.0, blk)],
                                  axis=0).astype(BF16)
            box['mixed%d' % m] = jnp.dot(wpair_ref[m], rhs, preferred_element_type=F32)

    def retention_in(heads):
        c, s = cos(), sin_signed()
        for h in heads:
            q = _rotate(zs(O_Q, h, RET_DK), c, s)
            k = _rotate(zs(O_K, h, RET_DK), c, s)
            v = zs(O_V, h, RET_DV)
            k_b = k.astype(BF16)
            box['scores%d' % h] = lax.dot_general(q.astype(BF16), k_b, (((1,), (1,)), ((), ())),
                                                  preferred_element_type=F32)
            box['qw%d' % h] = (q * tabs_ref[1, h]).astype(BF16)
            box['v%d' % h] = v.astype(BF16)
            box['u%d' % h] = state_update(h, k_b, (v * tabs_ref[2, h]).astype(BF16))

    def gating_out(pairs):
        for m in pairs:
            cols = slice(m * LANES, (m + 1) * LANES)
            mixed = box['mixed%d' % m] + bias_ref[:, cols]
            su = _gelu(zs(O_SU, m, LANES))
            mixin_ref[rows, cols] = (_silu(zs(O_SG, m, LANES)) * (su * mixed)).astype(BF16)

    def retention_mid(heads):
        for h in heads:
            scores_b = (box['scores%d' % h] * tabs_ref[0, h]).astype(BF16)
            box['o%d' % h] = state_output(h, scores_b, box['qw%d' % h], box['v%d' % h], box['u%d' % h])

    def retention_out(heads):
        for h in heads:
            o = box['o%d' % h]
            mu = jnp.mean(o, axis=-1, keepdims=True)
            cen = o - mu
            var = jnp.mean(cen * cen, axis=-1, keepdims=True)
            on = cen * lax.rsqrt(var + LN_EPS) * gn_ref[:, h * RET_DV:(h + 1) * RET_DV]
            mixin_ref[rows, SGU_WIDTH + h * RET_DV:SGU_WIDTH + (h + 1) * RET_DV] = (
                _silu(zs(O_RG, h, RET_DV)) * on).astype(BF16)

    part = functools.partial
    n_pairs = SGU_GROUPS // 2
    pair_groups = [range(n_pairs)] if len(head_groups) <= 2 else [range(n_pairs // 2), range(n_pairs // 2, n_pairs)]
    return ([gating_in] + [part(retention_in, g) for g in head_groups] + [part(gating_out, p) for p in pair_groups]
            + [part(retention_mid, g) for g in head_groups] + [part(retention_out, g) for g in head_groups])


def _interleave(pieces, stages):
    slots = max(len(stages), 1)
    done = 0
    for i in range(slots):
        upto = -(-(i + 1) * len(pieces) // slots)
        for piece in pieces[done:upto]:
            piece()
        done = upto
        if i < len(stages):
            stages[i]()


def _emit_blocks(in_pieces, out_pieces, stages, blocks, first_in=None, next_in=None):
    _interleave(first_in or [], [])
    for i, blk in enumerate(blocks):
        nxt = in_pieces(blocks[i + 1]) if i + 1 < len(blocks) else (next_in or [])
        if i > 0:
            ple, mix0, mix1, gate0, gate1 = out_pieces(blocks[i - 1])
            pieces = [ple, mix0, mix1] + nxt[:3] + [gate0] + nxt[3:4] + [gate1] + nxt[4:]
        else:
            pieces = nxt
        _interleave(pieces, stages(blk))
    _interleave(out_pieces(blocks[-1]), [])


def _init_tables(seg, sgu_w_ref, sgu_b_ref, mask_ref, wpm_ref, bias_ref):
    rows = lax.broadcasted_iota(jnp.int32, (CHUNK, LANES), 0)
    lanes = lax.broadcasted_iota(jnp.int32, (CHUNK, LANES), 1)
    if seg == CHUNK:
        group = lambda g: sgu_w_ref[g]
        bias_rows = sgu_b_ref[...]
    else:
        reps = CHUNK // seg
        select = jnp.where((rows < seg) & (lanes % seg == rows), 1.0, 0.0).astype(BF16)
        first = lambda a: jnp.where(lanes[:seg] < seg, a, 0.0)

        def group(g):
            stacked = jnp.concatenate([first(sgu_w_ref[g, 0:seg, :])] * reps, axis=0)
            return jnp.dot(stacked.astype(BF16), select, preferred_element_type=F32)
        b_first = jnp.where(lax.broadcasted_iota(jnp.int32, (SGU_GROUPS, LANES), 1) < seg, sgu_b_ref[...], 0.0)
        bias_rows = sum(pltpu.roll(b_first, r * seg, 1) for r in range(reps))
    for m in range(SGU_GROUPS // 2):
        pair = jnp.concatenate([group(2 * m), group(2 * m + 1)], axis=1)
        wpm_ref[m] = jnp.where(mask_ref[...] > 0.0, pair, 0.0).astype(BF16)
    group_of_lane = lax.broadcasted_iota(jnp.int32, (SGU_GROUPS, SGU_WIDTH), 1) // SGU_GROUP_DIM
    expand = jnp.where(group_of_lane == lax.broadcasted_iota(jnp.int32, (SGU_GROUPS, SGU_WIDTH), 0),
                       1.0, 0.0).astype(BF16)
    total, rest = jnp.zeros((CHUNK, SGU_WIDTH), F32), bias_rows
    for _ in range(3):
        term = rest.astype(BF16)
        total = total + lax.dot_general(term, expand, (((0,), (0,)), ((), ())), preferred_element_type=F32)
        rest = rest - term.astype(F32)
    bias_ref[...] = total


def _prompt_kernel(g_chunk, n_tiles, x_ref, x_next_ref, p_ref, cos_ref, sin_ref, w_in_ref, w_out_ref, w_gate_ref,
                   w_ple_ref, norm_pre_ref, norm_post_ref, ln_g_ref, gn_ref, sgu_w_ref, sgu_b_ref, mask_ref,
                   tabs_ref, y_ref, state_ref, w_in_b_ref, w_out_b_ref, w_gate_b_ref, w_ple_b_ref,
                   z_ref, z_first_ref, mixin_ref, wpm_ref, bias_ref):
    step = pl.program_id(0)
    first = slice(0, BLOCK_ROWS["prompt"])

    def next_first_in():
        return _in_pieces(x_next_ref[0], norm_pre_ref, w_in_b_ref, z_first_ref)

    @pl.when(step == 0)
    def _():
        _init_tables(CHUNK, sgu_w_ref, sgu_b_ref, mask_ref, wpm_ref, bias_ref)

    @pl.when(step < CAST_STEPS)
    def _():
        for src, dst in ((w_in_ref, w_in_b_ref), (w_out_ref, w_out_b_ref), (w_gate_ref, w_gate_b_ref),
                         (w_ple_ref, w_ple_b_ref)):
            n = src.shape[0]
            dst[pl.ds(pl.multiple_of(step * n, n), n), :] = src[...].astype(BF16)

    @pl.when(step == CAST_STEPS - 1)
    def _():
        _interleave(next_first_in(), [])

    @pl.when(step >= CAST_STEPS)
    def _():
        @pl.when((step - CAST_STEPS) % n_tiles == 0)
        def _():
            state_ref[...] = jnp.zeros_like(state_ref)

        def state_update(h, k_b, vw_b):
            return lax.dot_general(k_b, vw_b, (((0,), (0,)), ((), ())), preferred_element_type=F32)

        def state_output(h, scores_b, qw_b, v_b, u):
            s = state_ref[0, h]
            o = jnp.dot(jnp.concatenate([scores_b, qw_b], axis=1),
                        jnp.concatenate([v_b, s.astype(BF16)], axis=0), preferred_element_type=F32)
            state_ref[0, h] = g_chunk[h] * s + u
            return o

        def in_pieces(blk):
            return _in_pieces(x_ref[0, blk, :], norm_pre_ref, w_in_b_ref, z_ref.at[blk, :])

        def out_pieces(blk):
            return _out_pieces(x_ref[0, blk, :], p_ref[0, blk, :], mixin_ref.at[blk, :], w_out_b_ref,
                               norm_post_ref, w_gate_b_ref, w_ple_b_ref, y_ref.at[0, blk, :])

        def stages(blk):
            out = []
            for r in range(blk.start, blk.stop, CHUNK):
                rows = slice(r, r + CHUNK)
                out += _chunk_stages(z_first_ref if blk == first else z_ref, rows,
                                     lambda rows=rows: cos_ref[rows, :],
                                     lambda rows=rows: sin_ref[rows, :], ln_g_ref, gn_ref, bias_ref, wpm_ref,
                                     tabs_ref, state_update, state_output, mixin_ref, None,
                                     [[h] for h in range(RET_HEADS)])
            return out

        rows = BLOCK_ROWS["prompt"]
        _emit_blocks(in_pieces, out_pieces, stages, [slice(r, r + rows) for r in range(0, PROMPT_TILE, rows)],
                     next_in=next_first_in())


def _sample_kernel(g_seq, x_ref, p_ref, cos_ref, sin_ref, st_in_ref, w_in_ref, w_out_ref, w_gate_ref,
                   w_ple_ref, norm_pre_ref, norm_post_ref, ln_g_ref, gn_ref, sgu_w_ref, sgu_b_ref, mask_ref,
                   tabs_ref, y_ref, st_out_ref, vrows_ref, z_ref, mixin_ref, wpm_ref, bias_ref):
    seq_len = cos_ref.shape[0]
    seqs = CHUNK // seq_len

    @pl.when(pl.program_id(0) == 0)
    def _():
        _init_tables(seq_len, sgu_w_ref, sgu_b_ref, mask_ref, wpm_ref, bias_ref)

    pairs = seqs // 2
    per_pair = lambda a: a.reshape(pairs, 2 * seq_len, a.shape[-1])
    first_of_pair = lax.broadcasted_iota(jnp.int32, (1, 2 * seq_len, 1), 1) < seq_len
    tile_rows = lambda ref: jnp.concatenate([ref[...]] * seqs, axis=0)

    def stages(blk):
        pq = slice(blk.start // (2 * seq_len), blk.stop // (2 * seq_len))

        def state_update(h, k_b, vw_b):
            vw = per_pair(vw_b)
            zero = jnp.zeros_like(vw)
            both = jnp.concatenate([jnp.where(first_of_pair, vw, zero), jnp.where(first_of_pair, zero, vw)],
                                   axis=-1)
            u = jnp.einsum('pjd,pje->pde', per_pair(k_b), both, preferred_element_type=F32)
            st_out_ref[pq, 0, h] = g_seq[h] * st_in_ref[pq, 0, h] + u[:, :, :RET_DV]
            st_out_ref[pq, 1, h] = g_seq[h] * st_in_ref[pq, 1, h] + u[:, :, RET_DV:]
            return None

        def state_output(h, scores_b, qw_b, v_b, _):
            o = jnp.dot(scores_b, v_b, preferred_element_type=F32)
            s_pair = jnp.concatenate([st_in_ref[pq, 0, h], st_in_ref[pq, 1, h]], axis=-1).astype(BF16)
            both = jnp.einsum('pid,pde->pie', per_pair(qw_b), s_pair, preferred_element_type=F32)
            o_inter = jnp.where(first_of_pair, both[:, :, :RET_DV], both[:, :, RET_DV:])
            return o + o_inter.reshape(CHUNK, RET_DV)

        return _chunk_stages(z_ref, blk, lambda: tile_rows(cos_ref), lambda: tile_rows(sin_ref), ln_g_ref,
                             gn_ref, bias_ref, wpm_ref, tabs_ref, state_update, state_output, mixin_ref,
                             vrows_ref, [range(RET_HEADS)])

    def in_pieces(blk):
        return _in_pieces(x_ref[blk, :], norm_pre_ref, w_in_ref, z_ref.at[blk, :])

    def out_pieces(blk):
        return _out_pieces(x_ref[blk, :], p_ref[blk, :], mixin_ref.at[blk, :], w_out_ref, norm_post_ref,
                           w_gate_ref, w_ple_ref, y_ref.at[blk, :])

    rows = BLOCK_ROWS["sample"]
    _emit_blocks(in_pieces, out_pieces, stages, [slice(r, r + rows) for r in range(0, SAMPLE_TILE, rows)],
                 first_in=in_pieces(slice(0, rows)))


def _const_spec(shape):
    return pl.BlockSpec(shape, lambda *_: (0,) * len(shape), pipeline_mode=pl.Buffered(1))


def kernel(x_prompt, x_sample, state_ret, p_prompt, p_sample, w_in, w_out, norm_pre, norm_post, sgu_w, sgu_b,
           sgu_ln, ret_gn, w_ple_proj, w_ple_gate):
    batch, seq, _ = x_prompt.shape
    dec_batch, dec_seq, _ = x_sample.shape
    n_tiles = seq // PROMPT_TILE
    n_tok = dec_batch * dec_seq
    assert seq % PROMPT_TILE == 0 and PROMPT_TILE % BLOCK_ROWS["prompt"] == 0 and w_in.shape[0] == 1
    assert CHUNK % dec_seq == 0 and n_tok % SAMPLE_TILE == 0 and SAMPLE_TILE % BLOCK_ROWS["sample"] == 0
    assert D_MODEL % CAST_STEPS == 0 and PLE_DIM % (16 * CAST_STEPS) == 0
    assert (CHUNK // dec_seq) % 2 == 0 and dec_batch % (SAMPLE_TILE // dec_seq) == 0

    small = (norm_pre[0][None, :], norm_post[0][None, :], sgu_ln[0][None, :], ret_gn[0][None, :],
             sgu_w[0], sgu_b[0])
    small_specs = [_const_spec((1, D_MODEL)), _const_spec((1, D_MODEL)), _const_spec((1, SGU_WIDTH)),
                   _const_spec((1, RET_HEADS * RET_DV)), _const_spec((SGU_GROUPS, CHUNK, CHUNK)),
                   _const_spec((SGU_GROUPS, CHUNK)), _const_spec((CHUNK, 2 * CHUNK)),
                   _const_spec((3, RET_HEADS, CHUNK, LANES))]
    table_scratch = [pltpu.VMEM((SGU_GROUPS // 2, CHUNK, 2 * CHUNK), BF16), pltpu.VMEM((CHUNK, SGU_WIDTH), F32)]
    weight_shapes = [(D_MODEL, IN_WIDTH), (D_MODEL, D_MODEL), (D_MODEL, D_MODEL), (PLE_DIM, D_MODEL)]

    tabs_p, g_chunk = _retention_tables(CHUNK)
    cos_p, sin_p = _rotary_tables(np.arange(seq))
    tile_of = lambda s: jnp.maximum(s - CAST_STEPS, 0)
    tile_spec = lambda width: pl.BlockSpec((1, PROMPT_TILE, width),
                                           lambda s: (tile_of(s) // n_tiles, tile_of(s) % n_tiles, 0))
    pos_spec = pl.BlockSpec((PROMPT_TILE, LANES), lambda s: (tile_of(s) % n_tiles, 0))
    blocks_per_tile = PROMPT_TILE // BLOCK_ROWS["prompt"]
    next_tile = lambda s: jnp.minimum(jnp.maximum(s - CAST_STEPS + 1, 0), batch * n_tiles - 1)
    next_spec = pl.BlockSpec((1, BLOCK_ROWS["prompt"], D_MODEL),
                             lambda s: (next_tile(s) // n_tiles, (next_tile(s) % n_tiles) * blocks_per_tile, 0))
    cast_spec = lambda shape: pl.BlockSpec((shape[0] // CAST_STEPS, shape[1]),
                                           lambda s: (jnp.minimum(s, CAST_STEPS - 1), 0))
    y_prompt, st_prompt, w_in_b, w_out_b, w_gate_b, w_ple_b = pl.pallas_call(
        functools.partial(_prompt_kernel, g_chunk, n_tiles),
        grid=(CAST_STEPS + batch * n_tiles,),
        in_specs=[tile_spec(D_MODEL), next_spec, tile_spec(PLE_DIM), pos_spec, pos_spec]
        + [cast_spec(shape) for shape in weight_shapes] + small_specs,
        out_specs=[tile_spec(D_MODEL),
                   pl.BlockSpec((1, RET_HEADS, RET_DK, RET_DV), lambda s: (tile_of(s) // n_tiles, 0, 0, 0))]
        + [pl.BlockSpec(shape, lambda s: (0, 0)) for shape in weight_shapes],
        out_shape=[jax.ShapeDtypeStruct((batch, seq, D_MODEL), F32),
                   jax.ShapeDtypeStruct((batch, RET_HEADS, RET_DK, RET_DV), F32)]
        + [jax.ShapeDtypeStruct(shape, BF16) for shape in weight_shapes],
        scratch_shapes=[pltpu.VMEM((PROMPT_TILE, IN_WIDTH), F32), pltpu.VMEM((BLOCK_ROWS["prompt"], IN_WIDTH), F32),
                        pltpu.VMEM((PROMPT_TILE, D_MODEL), BF16)] + table_scratch,
        compiler_params=pltpu.CompilerParams(dimension_semantics=("arbitrary",),
                                             vmem_limit_bytes=VMEM_LIMIT_BYTES),
        name="prompt_layer",
    )(x_prompt, x_prompt, p_prompt[0], cos_p, sin_p, w_in[0], w_out[0], w_ple_gate[0], w_ple_proj[0], *small,
      _mix_mask(CHUNK), tabs_p)

    tabs_s, g_seq = _retention_tables(dec_seq)
    cos_s, sin_s = _rotary_tables(PAST_LEN + np.arange(dec_seq))
    tile_seqs = SAMPLE_TILE // dec_seq
    tok_spec = lambda width: pl.BlockSpec((SAMPLE_TILE, width), lambda i: (i, 0))
    state_spec = pl.BlockSpec((tile_seqs // 2, 2, RET_HEADS, RET_DK, RET_DV), lambda i: (i, 0, 0, 0, 0))
    paired = (dec_batch // 2, 2, RET_HEADS, RET_DK, RET_DV)
    y_sample, st_sample, v_sample = pl.pallas_call(
        functools.partial(_sample_kernel, g_seq),
        grid=(n_tok // SAMPLE_TILE,),
        in_specs=[tok_spec(D_MODEL), tok_spec(PLE_DIM), _const_spec((dec_seq, LANES)),
                  _const_spec((dec_seq, LANES)), state_spec]
        + [_const_spec(shape) for shape in weight_shapes] + small_specs,
        out_specs=[tok_spec(D_MODEL), state_spec, tok_spec(SGU_WIDTH)],
        out_shape=[jax.ShapeDtypeStruct((n_tok, D_MODEL), F32),
                   jax.ShapeDtypeStruct(paired, F32), jax.ShapeDtypeStruct((n_tok, SGU_WIDTH), F32)],
        scratch_shapes=[pltpu.VMEM((SAMPLE_TILE, IN_WIDTH), F32), pltpu.VMEM((SAMPLE_TILE, D_MODEL), BF16)]
        + table_scratch,
        compiler_params=pltpu.CompilerParams(dimension_semantics=("arbitrary",),
                                             vmem_limit_bytes=VMEM_LIMIT_BYTES),
        name="sample_layer",
    )(x_sample.reshape(n_tok, D_MODEL), p_sample[0].reshape(n_tok, PLE_DIM), cos_s, sin_s,
      state_ret[0].reshape(paired),
      w_in_b, w_out_b, w_gate_b, w_ple_b, *small, _mix_mask(dec_seq), tabs_s)

    return (y_prompt, y_sample.reshape(dec_batch, dec_seq, D_MODEL), st_prompt[None],
            st_sample.reshape(state_ret.shape),
            v_sample.reshape(1, dec_batch, dec_seq, SGU_WIDTH))
```

```python
import functools

import numpy as np
import jax
import jax.numpy as jnp
from jax import lax
from jax.experimental import pallas as pl
from jax.experimental.pallas import tpu as pltpu

F32 = jnp.float32
BF16 = jnp.bfloat16

D_MODEL = 1024
PAST_LEN = 16384
SGU_WIDTH = 512
SGU_GROUPS = 8
SGU_GROUP_DIM = SGU_WIDTH // SGU_GROUPS
RET_HEADS = 4
RET_DK = 128
RET_DV = 128
CHUNK = 128
ROPE_THETA = 10000.0
PLE_DIM = 256
RMS_EPS = 1e-6
LN_EPS = 1e-5
IN_WIDTH = 3 * SGU_WIDTH + RET_HEADS * (2 * RET_DK + 2 * RET_DV)
O_SU, O_SV, O_SG = 0, SGU_WIDTH, 2 * SGU_WIDTH
O_Q = 3 * SGU_WIDTH
O_K = O_Q + RET_HEADS * RET_DK
O_V = O_K + RET_HEADS * RET_DK
O_RG = O_V + RET_HEADS * RET_DV

LANES = 128
VMEM_LIMIT_BYTES = 60 * 1024 * 1024

PROMPT_TILE = 1024
BLOCK_ROWS = {"prompt": 256, "sample": CHUNK}
PIECE_COLS = 512
CAST_STEPS = 8
SAMPLE_TILE = 2 * CHUNK


def _log_gamma():
    return np.log(1.0 - 2.0 ** (-5.0 - np.arange(RET_HEADS, dtype=np.float64)))


def _retention_tables(seg):
    lg = _log_gamma()[:, None, None]
    r = np.arange(CHUNK)
    i, j = r[:, None] % seg, r[None, :] % seg
    same = (r[:, None] // seg) == (r[None, :] // seg)
    decay = np.where(same & (i >= j), np.exp(lg * np.maximum(i - j, 0)), 0.0)
    wq = np.broadcast_to(np.exp(lg * (i + 1.0)), (RET_HEADS, CHUNK, LANES))
    wkv = np.broadcast_to(np.exp(lg * (seg - 1.0 - i)), (RET_HEADS, CHUNK, LANES))
    scale = RET_DK ** -0.5
    tabs = np.stack([decay * scale, wq, wkv * scale]).astype(np.float32)
    g_seg = [float(v) for v in np.exp(_log_gamma() * seg)]
    return tabs, g_seg


def _rotary_tables(pos):
    half = RET_DK // 2
    inv = ROPE_THETA ** (-np.arange(half, dtype=np.float64) / half)
    ang = pos.astype(np.float64)[:, None] * inv[None, :]
    cos, sin = np.cos(ang), np.sin(ang)
    return (np.concatenate([cos, cos], axis=1).astype(np.float32),
            np.concatenate([-sin, sin], axis=1).astype(np.float32))


def _mix_mask(seg):
    r = np.arange(CHUNK)
    same = (r[:, None] // seg) == (r[None, :] // seg)
    m = same & ((r[None, :] % seg) <= (r[:, None] % seg))
    return np.concatenate([m, m], axis=1).astype(np.float32)


def _gelu(x):
    c = float(np.sqrt(2.0 / np.pi))
    half = 0.5 * x
    return half + half * jnp.tanh(x * (c + (c * 0.044715) * (x * x)))


def _silu(x):
    return x * jax.nn.sigmoid(x)


def _rotate(x, cos, sin_signed):
    return x * cos + pltpu.roll(x, RET_DK // 2, 1) * sin_signed


def _normed_bf16(x, norm_ref):
    ms = jnp.mean(x * x, axis=-1, keepdims=True)
    return (x * lax.rsqrt(ms + RMS_EPS) * norm_ref[...]).astype(BF16)


def _in_pieces(x, norm_pre_ref, w_in_ref, z_ref):
    box = {}

    def piece(c0):
        def run():
            if 'h' not in box:
                box['h'] = _normed_bf16(x, norm_pre_ref)
            z_ref[:, c0:c0 + PIECE_COLS] = jnp.dot(box['h'], w_in_ref[:, c0:c0 + PIECE_COLS],
                                                   preferred_element_type=F32)
        return run
    return [piece(c0) for c0 in range(0, IN_WIDTH, PIECE_COLS)]


def _out_pieces(x, p, mixin_ref, w_out_ref, norm_post_ref, w_gate_ref, w_ple_ref, y_ref):
    box = {}
    halves = [slice(c0, c0 + PIECE_COLS) for c0 in range(0, D_MODEL, PIECE_COLS)]

    def ple():
        box['ple'] = jnp.dot(p.astype(BF16), w_ple_ref[...], preferred_element_type=F32)

    def mix(i):
        def run():
            box['mix%d' % i] = jnp.dot(mixin_ref[...], w_out_ref[:, halves[i]], preferred_element_type=F32)
        return run

    def gate(i):
        def run():
            if 'x1' not in box:
                mixes = [box['mix%d' % j] for j in range(len(halves))]
                ms = sum(jnp.sum(m * m, axis=-1, keepdims=True) for m in mixes) * (1.0 / D_MODEL)
                scale = lax.rsqrt(ms + RMS_EPS)
                box['x1'] = [x[:, halves[j]] + mixes[j] * scale * norm_post_ref[:, halves[j]]
                             for j in range(len(halves))]
                box['x1b'] = jnp.concatenate(box['x1'], axis=1).astype(BF16)
            g = jnp.dot(box['x1b'], w_gate_ref[:, halves[i]], preferred_element_type=F32)
            y_ref[:, halves[i]] = box['x1'][i] + jax.nn.sigmoid(g) * box['ple'][:, halves[i]]
        return run
    return [ple] + [mix(i) for i in range(len(halves))] + [gate(i) for i in range(len(halves))]


def _chunk_stages(z_ref, rows, cos, sin_signed, ln_g_ref, gn_ref, bias_ref, wpair_ref, tabs_ref,
                  state_update, state_output, mixin_ref, vrows_ref, head_groups):
    box = {}
    zs = lambda off, i, width: z_ref[rows, off + i * width:off + (i + 1) * width]

    def gating_in():
        sv = _gelu(z_ref[rows, O_SV:O_SV + SGU_WIDTH])
        mu = jnp.mean(sv, axis=-1, keepdims=True)
        cen = sv - mu
        var = jnp.mean(cen * cen, axis=-1, keepdims=True)
        vn = cen * lax.rsqrt(var + LN_EPS) * ln_g_ref[...]
        if vrows_ref is not None:
            vrows_ref[rows, :] = vn
        low_lanes = lax.broadcasted_iota(jnp.int32, (CHUNK, LANES), 1) < SGU_GROUP_DIM
        for m in range(SGU_GROUPS // 2):
            blk = vn[:, m * LANES:(m + 1) * LANES]
            rhs = jnp.concatenate([jnp.where(low_lanes, blk, 0.0), jnp.where(low_lanes, 0.0, blk)],
                                  axis=0).astype(BF16)
            box['mixed%d' % m] = jnp.dot(wpair_ref[m], rhs, preferred_element_type=F32)

    def retention_in(heads):
        c, s = cos(), sin_signed()
        for h in heads:
            q = _rotate(zs(O_Q, h, RET_DK), c, s)
            k = _rotate(zs(O_K, h, RET_DK), c, s)
            v = zs(O_V, h, RET_DV)
            k_b = k.astype(BF16)
            box['scores%d' % h] = lax.dot_general(q.astype(BF16), k_b, (((1,), (1,)), ((), ())),
                                                  preferred_element_type=F32)
            box['qw%d' % h] = (q * tabs_ref[1, h]).astype(BF16)
            box['v%d' % h] = v.astype(BF16)
            box['u%d' % h] = state_update(h, k_b, (v * tabs_ref[2, h]).astype(BF16))

    def gating_out(pairs):
        for m in pairs:
            cols = slice(m * LANES, (m + 1) * LANES)
            mixed = box['mixed%d' % m] + bias_ref[:, cols]
            su = _gelu(zs(O_SU, m, LANES))
            mixin_ref[rows, cols] = (_silu(zs(O_SG, m, LANES)) * (su * mixed)).astype(BF16)

    def retention_mid(heads):
        for h in heads:
            scores_b = (box['scores%d' % h] * tabs_ref[0, h]).astype(BF16)
            box['o%d' % h] = state_output(h, scores_b, box['qw%d' % h], box['v%d' % h], box['u%d' % h])

    def retention_out(heads):
        for h in heads:
            o = box['o%d' % h]
            mu = jnp.mean(o, axis=-1, keepdims=True)
            cen = o - mu
            var = jnp.mean(cen * cen, axis=-1, keepdims=True)
            on = cen * lax.rsqrt(var + LN_EPS) * gn_ref[:, h * RET_DV:(h + 1) * RET_DV]
            mixin_ref[rows, SGU_WIDTH + h * RET_DV:SGU_WIDTH + (h + 1) * RET_DV] = (
                _silu(zs(O_RG, h, RET_DV)) * on).astype(BF16)

    part = functools.partial
    n_pairs = SGU_GROUPS // 2
    pair_groups = [range(n_pairs)] if len(head_groups) <= 2 else [range(n_pairs // 2), range(n_pairs // 2, n_pairs)]
    return ([gating_in] + [part(retention_in, g) for g in head_groups] + [part(gating_out, p) for p in pair_groups]
            + [part(retention_mid, g) for g in head_groups] + [part(retention_out, g) for g in head_groups])


def _interleave(pieces, stages):
    slots = max(len(stages), 1)
    done = 0
    for i in range(slots):
        upto = -(-(i + 1) * len(pieces) // slots)
        for piece in pieces[done:upto]:
            piece()
        done = upto
        if i < len(stages):
            stages[i]()


def _emit_blocks(in_pieces, out_pieces, stages, blocks, first_in=None, next_in=None):
    _interleave(first_in or [], [])
    for i, blk in enumerate(blocks):
        nxt = in_pieces(blocks[i + 1]) if i + 1 < len(blocks) else (next_in or [])
        if i > 0:
            ple, mix0, mix1, gate0, gate1 = out_pieces(blocks[i - 1])
            pieces = [ple, mix0, mix1] + nxt[:3] + [gate0] + nxt[3:4] + [gate1] + nxt[4:]
        else:
            pieces = nxt
        _interleave(pieces, stages(blk))
    _interleave(out_pieces(blocks[-1]), [])


def _init_tables(seg, sgu_w_ref, sgu_b_ref, mask_ref, wpm_ref, bias_ref):
    rows = lax.broadcasted_iota(jnp.int32, (CHUNK, LANES), 0)
    lanes = lax.broadcasted_iota(jnp.int32, (CHUNK, LANES), 1)
    if seg == CHUNK:
        group = lambda g: sgu_w_ref[g]
        bias_rows = sgu_b_ref[...]
    else:
        reps = CHUNK // seg
        select = jnp.where((rows < seg) & (lanes % seg == rows), 1.0, 0.0).astype(BF16)
        first = lambda a: jnp.where(lanes[:seg] < seg, a, 0.0)

        def group(g):
            stacked = jnp.concatenate([first(sgu_w_ref[g, 0:seg, :])] * reps, axis=0)
            return jnp.dot(stacked.astype(BF16), select, preferred_element_type=F32)
        b_first = jnp.where(lax.broadcasted_iota(jnp.int32, (SGU_GROUPS, LANES), 1) < seg, sgu_b_ref[...], 0.0)
        bias_rows = sum(pltpu.roll(b_first, r * seg, 1) for r in range(reps))
    for m in range(SGU_GROUPS // 2):
        pair = jnp.concatenate([group(2 * m), group(2 * m + 1)], axis=1)
        wpm_ref[m] = jnp.where(mask_ref[...] > 0.0, pair, 0.0).astype(BF16)
    group_of_lane = lax.broadcasted_iota(jnp.int32, (SGU_GROUPS, SGU_WIDTH), 1) // SGU_GROUP_DIM
    expand = jnp.where(group_of_lane == lax.broadcasted_iota(jnp.int32, (SGU_GROUPS, SGU_WIDTH), 0),
                       1.0, 0.0).astype(BF16)
    total, rest = jnp.zeros((CHUNK, SGU_WIDTH), F32), bias_rows
    for _ in range(3):
        term = rest.astype(BF16)
        total = total + lax.dot_general(term, expand, (((0,), (0,)), ((), ())), preferred_element_type=F32)
        rest = rest - term.astype(F32)
    bias_ref[...] = total


def _prompt_kernel(g_chunk, n_tiles, x_ref, x_next_ref, p_ref, cos_ref, sin_ref, w_in_ref, w_out_ref, w_gate_ref,
                   w_ple_ref, norm_pre_ref, norm_post_ref, ln_g_ref, gn_ref, sgu_w_ref, sgu_b_ref, mask_ref,
                   tabs_ref, y_ref, state_ref, w_in_b_ref, w_out_b_ref, w_gate_b_ref, w_ple_b_ref,
                   z_ref, z_first_ref, mixin_ref, wpm_ref, bias_ref):
    step = pl.program_id(0)
    first = slice(0, BLOCK_ROWS["prompt"])

    def next_first_in():
        return _in_pieces(x_next_ref[0], norm_pre_ref, w_in_b_ref, z_first_ref)

    @pl.when(step == 0)
    def _():
        _init_tables(CHUNK, sgu_w_ref, sgu_b_ref, mask_ref, wpm_ref, bias_ref)

    @pl.when(step < CAST_STEPS)
    def _():
        for src, dst in ((w_in_ref, w_in_b_ref), (w_out_ref, w_out_b_ref), (w_gate_ref, w_gate_b_ref),
                         (w_ple_ref, w_ple_b_ref)):
            n = src.shape[0]
            dst[pl.ds(pl.multiple_of(step * n, n), n), :] = src[...].astype(BF16)

    @pl.when(step == CAST_STEPS - 1)
    def _():
        _interleave(next_first_in(), [])

    @pl.when(step >= CAST_STEPS)
    def _():
        @pl.when((step - CAST_STEPS) % n_tiles == 0)
        def _():
            state_ref[...] = jnp.zeros_like(state_ref)

        def state_update(h, k_b, vw_b):
            return lax.dot_general(k_b, vw_b, (((0,), (0,)), ((), ())), preferred_element_type=F32)

        def state_output(h, scores_b, qw_b, v_b, u):
            s = state_ref[0, h]
            o = jnp.dot(jnp.concatenate([scores_b, qw_b], axis=1),
                        jnp.concatenate([v_b, s.astype(BF16)], axis=0), preferred_element_type=F32)
            state_ref[0, h] = g_chunk[h] * s + u
            return o

        def in_pieces(blk):
            return _in_pieces(x_ref[0, blk, :], norm_pre_ref, w_in_b_ref, z_ref.at[blk, :])

        def out_pieces(blk):
            return _out_pieces(x_ref[0, blk, :], p_ref[0, blk, :], mixin_ref.at[blk, :], w_out_b_ref,
                               norm_post_ref, w_gate_b_ref, w_ple_b_ref, y_ref.at[0, blk, :])

        def stages(blk):
            out = []
            for r in range(blk.start, blk.stop, CHUNK):
                rows = slice(r, r + CHUNK)
                out += _chunk_stages(z_first_ref if blk == first else z_ref, rows,
                                     lambda rows=rows: cos_ref[rows, :],
                                     lambda rows=rows: sin_ref[rows, :], ln_g_ref, gn_ref, bias_ref, wpm_ref,
                                     tabs_ref, state_update, state_output, mixin_ref, None,
                                     [range(0, RET_HEADS // 2), range(RET_HEADS // 2, RET_HEADS)])
            return out

        rows = BLOCK_ROWS["prompt"]
        _emit_blocks(in_pieces, out_pieces, stages, [slice(r, r + rows) for r in range(0, PROMPT_TILE, rows)],
                     next_in=next_first_in())


def _sample_kernel(g_seq, x_ref, p_ref, cos_ref, sin_ref, st_in_ref, w_in_hbm, w_out_hbm, w_gate_hbm,
                   w_ple_ref, norm_pre_ref, norm_post_ref, ln_g_ref, gn_ref, sgu_w_ref, sgu_b_ref, mask_ref,
                   tabs_ref, y_ref, st_out_ref, vrows_ref, z_ref, mixin_ref, wpm_ref, bias_ref,
                   w_in_ref, w_out_ref, w_gate_ref, w_sem):
    seq_len = cos_ref.shape[0]
    seqs = CHUNK // seq_len
    rows = BLOCK_ROWS["sample"]
    first = slice(0, rows)

    def in_pieces(blk):
        return _in_pieces(x_ref[blk, :], norm_pre_ref, w_in_ref, z_ref.at[blk, :])

    n_in = IN_WIDTH // PIECE_COLS
    cols = lambda i: slice(i * PIECE_COLS, (i + 1) * PIECE_COLS)
    copies = [pltpu.make_async_copy(w_in_hbm.at[:, cols(i)], w_in_ref.at[:, cols(i)], w_sem.at[i])
              for i in range(n_in)]
    copies += [pltpu.make_async_copy(w_out_hbm, w_out_ref, w_sem.at[n_in]),
               pltpu.make_async_copy(w_gate_hbm, w_gate_ref, w_sem.at[n_in + 1])]

    @pl.when(pl.program_id(0) == 0)
    def _():
        for copy in copies:
            copy.start()
        _init_tables(seq_len, sgu_w_ref, sgu_b_ref, mask_ref, wpm_ref, bias_ref)
        for copy, piece in zip(copies, in_pieces(first)):
            copy.wait()
            piece()
        for copy in copies[n_in:]:
            copy.wait()

    @pl.when(pl.program_id(0) != 0)
    def _():
        _interleave(in_pieces(first), [])

    pairs = seqs // 2
    per_pair = lambda a: a.reshape(pairs, 2 * seq_len, a.shape[-1])
    first_of_pair = lax.broadcasted_iota(jnp.int32, (1, 2 * seq_len, 1), 1) < seq_len
    tile_rows = lambda ref: jnp.concatenate([ref[...]] * seqs, axis=0)

    def stages(blk):
        pq = slice(blk.start // (2 * seq_len), blk.stop // (2 * seq_len))

        def state_update(h, k_b, vw_b):
            vw = per_pair(vw_b)
            zero = jnp.zeros_like(vw)
            both = jnp.concatenate([jnp.where(first_of_pair, vw, zero), jnp.where(first_of_pair, zero, vw)],
                                   axis=-1)
            u = jnp.einsum('pjd,pje->pde', per_pair(k_b), both, preferred_element_type=F32)
            st_out_ref[pq, 0, h] = g_seq[h] * st_in_ref[pq, 0, h] + u[:, :, :RET_DV]
            st_out_ref[pq, 1, h] = g_seq[h] * st_in_ref[pq, 1, h] + u[:, :, RET_DV:]
            return None

        def state_output(h, scores_b, qw_b, v_b, _):
            o = jnp.dot(scores_b, v_b, preferred_element_type=F32)
            s_pair = jnp.concatenate([st_in_ref[pq, 0, h], st_in_ref[pq, 1, h]], axis=-1).astype(BF16)
            both = jnp.einsum('pid,pde->pie', per_pair(qw_b), s_pair, preferred_element_type=F32)
            o_inter = jnp.where(first_of_pair, both[:, :, :RET_DV], both[:, :, RET_DV:])
            return o + o_inter.reshape(CHUNK, RET_DV)

        return _chunk_stages(z_ref, blk, lambda: tile_rows(cos_ref), lambda: tile_rows(sin_ref), ln_g_ref,
                             gn_ref, bias_ref, wpm_ref, tabs_ref, state_update, state_output, mixin_ref,
                             vrows_ref, [range(RET_HEADS)])

    def out_pieces(blk):
        return _out_pieces(x_ref[blk, :], p_ref[blk, :], mixin_ref.at[blk, :], w_out_ref, norm_post_ref,
                           w_gate_ref, w_ple_ref, y_ref.at[blk, :])

    _emit_blocks(in_pieces, out_pieces, stages, [slice(r, r + rows) for r in range(0, SAMPLE_TILE, rows)])


def _const_spec(shape):
    return pl.BlockSpec(shape, lambda *_: (0,) * len(shape), pipeline_mode=pl.Buffered(1))


def kernel(x_prompt, x_sample, state_ret, p_prompt, p_sample, w_in, w_out, norm_pre, norm_post, sgu_w, sgu_b,
           sgu_ln, ret_gn, w_ple_proj, w_ple_gate):
    batch, seq, _ = x_prompt.shape
    dec_batch, dec_seq, _ = x_sample.shape
    n_tiles = seq // PROMPT_TILE
    n_tok = dec_batch * dec_seq
    assert seq % PROMPT_TILE == 0 and PROMPT_TILE % BLOCK_ROWS["prompt"] == 0 and w_in.shape[0] == 1
    assert CHUNK % dec_seq == 0 and n_tok % SAMPLE_TILE == 0 and SAMPLE_TILE % BLOCK_ROWS["sample"] == 0
    assert D_MODEL % CAST_STEPS == 0 and PLE_DIM % (16 * CAST_STEPS) == 0
    assert (CHUNK // dec_seq) % 2 == 0 and dec_batch % (SAMPLE_TILE // dec_seq) == 0

    small = (norm_pre[0][None, :], norm_post[0][None, :], sgu_ln[0][None, :], ret_gn[0][None, :],
             sgu_w[0], sgu_b[0])
    small_specs = [_const_spec((1, D_MODEL)), _const_spec((1, D_MODEL)), _const_spec((1, SGU_WIDTH)),
                   _const_spec((1, RET_HEADS * RET_DV)), _const_spec((SGU_GROUPS, CHUNK, CHUNK)),
                   _const_spec((SGU_GROUPS, CHUNK)), _const_spec((CHUNK, 2 * CHUNK)),
                   _const_spec((3, RET_HEADS, CHUNK, LANES))]
    table_scratch = [pltpu.VMEM((SGU_GROUPS // 2, CHUNK, 2 * CHUNK), BF16), pltpu.VMEM((CHUNK, SGU_WIDTH), F32)]
    weight_shapes = [(D_MODEL, IN_WIDTH), (D_MODEL, D_MODEL), (D_MODEL, D_MODEL), (PLE_DIM, D_MODEL)]

    tabs_p, g_chunk = _retention_tables(CHUNK)
    cos_p, sin_p = _rotary_tables(np.arange(seq))
    tile_of = lambda s: jnp.maximum(s - CAST_STEPS, 0)
    tile_spec = lambda width: pl.BlockSpec((1, PROMPT_TILE, width),
                                           lambda s: (tile_of(s) // n_tiles, tile_of(s) % n_tiles, 0))
    pos_spec = pl.BlockSpec((PROMPT_TILE, LANES), lambda s: (tile_of(s) % n_tiles, 0))
    blocks_per_tile = PROMPT_TILE // BLOCK_ROWS["prompt"]
    next_tile = lambda s: jnp.minimum(jnp.maximum(s - CAST_STEPS + 1, 0), batch * n_tiles - 1)
    next_spec = pl.BlockSpec((1, BLOCK_ROWS["prompt"], D_MODEL),
                             lambda s: (next_tile(s) // n_tiles, (next_tile(s) % n_tiles) * blocks_per_tile, 0))
    cast_spec = lambda shape: pl.BlockSpec((shape[0] // CAST_STEPS, shape[1]),
                                           lambda s: (jnp.minimum(s, CAST_STEPS - 1), 0))
    y_prompt, st_prompt, w_in_b, w_out_b, w_gate_b, w_ple_b = pl.pallas_call(
        functools.partial(_prompt_kernel, g_chunk, n_tiles),
        grid=(CAST_STEPS + batch * n_tiles,),
        in_specs=[tile_spec(D_MODEL), next_spec, tile_spec(PLE_DIM), pos_spec, pos_spec]
        + [cast_spec(shape) for shape in weight_shapes] + small_specs,
        out_specs=[tile_spec(D_MODEL),
                   pl.BlockSpec((1, RET_HEADS, RET_DK, RET_DV), lambda s: (tile_of(s) // n_tiles, 0, 0, 0))]
        + [pl.BlockSpec(shape, lambda s: (0, 0)) for shape in weight_shapes],
        out_shape=[jax.ShapeDtypeStruct((batch, seq, D_MODEL), F32),
                   jax.ShapeDtypeStruct((batch, RET_HEADS, RET_DK, RET_DV), F32)]
        + [jax.ShapeDtypeStruct(shape, BF16) for shape in weight_shapes],
        scratch_shapes=[pltpu.VMEM((PROMPT_TILE, IN_WIDTH), F32), pltpu.VMEM((BLOCK_ROWS["prompt"], IN_WIDTH), F32),
                        pltpu.VMEM((PROMPT_TILE, D_MODEL), BF16)] + table_scratch,
        compiler_params=pltpu.CompilerParams(dimension_semantics=("arbitrary",),
                                             vmem_limit_bytes=VMEM_LIMIT_BYTES),
        name="prompt_layer",
    )(x_prompt, x_prompt, p_prompt[0], cos_p, sin_p, w_in[0], w_out[0], w_ple_gate[0], w_ple_proj[0], *small,
      _mix_mask(CHUNK), tabs_p)

    tabs_s, g_seq = _retention_tables(dec_seq)
    cos_s, sin_s = _rotary_tables(PAST_LEN + np.arange(dec_seq))
    tile_seqs = SAMPLE_TILE // dec_seq
    tok_spec = lambda width: pl.BlockSpec((SAMPLE_TILE, width), lambda i: (i, 0))
    state_spec = pl.BlockSpec((tile_seqs // 2, 2, RET_HEADS, RET_DK, RET_DV), lambda i: (i, 0, 0, 0, 0))
    paired = (dec_batch // 2, 2, RET_HEADS, RET_DK, RET_DV)
    y_sample, st_sample, v_sample = pl.pallas_call(
        functools.partial(_sample_kernel, g_seq),
        grid=(n_tok // SAMPLE_TILE,),
        in_specs=[tok_spec(D_MODEL), tok_spec(PLE_DIM), _const_spec((dec_seq, LANES)),
                  _const_spec((dec_seq, LANES)), state_spec]
        + [pl.BlockSpec(memory_space=pl.ANY)] * 3 + [_const_spec(weight_shapes[3])] + small_specs,
        out_specs=[tok_spec(D_MODEL), state_spec, tok_spec(SGU_WIDTH)],
        out_shape=[jax.ShapeDtypeStruct((n_tok, D_MODEL), F32),
                   jax.ShapeDtypeStruct(paired, F32), jax.ShapeDtypeStruct((n_tok, SGU_WIDTH), F32)],
        scratch_shapes=[pltpu.VMEM((SAMPLE_TILE, IN_WIDTH), F32), pltpu.VMEM((SAMPLE_TILE, D_MODEL), BF16)]
        + table_scratch + [pltpu.VMEM(shape, BF16) for shape in weight_shapes[:3]]
        + [pltpu.SemaphoreType.DMA((IN_WIDTH // PIECE_COLS + 2,))],
        compiler_params=pltpu.CompilerParams(dimension_semantics=("arbitrary",),
                                             vmem_limit_bytes=VMEM_LIMIT_BYTES),
        name="sample_layer",
    )(x_sample.reshape(n_tok, D_MODEL), p_sample[0].reshape(n_tok, PLE_DIM), cos_s, sin_s,
      state_ret[0].reshape(paired),
      w_in_b, w_out_b, w_gate_b, w_ple_b, *small, _mix_mask(dec_seq), tabs_s)

    return (y_prompt, y_sample.reshape(dec_batch, dec_seq, D_MODEL), st_prompt[None],
            st_sample.reshape(state_ret.shape),
            v_sample.reshape(1, dec_batch, dec_seq, SGU_WIDTH))
```

```python
import functools

import numpy as np
import jax
import jax.numpy as jnp
from jax import lax
from jax.experimental import pallas as pl
from jax.experimental.pallas import tpu as pltpu

F32 = jnp.float32
BF16 = jnp.bfloat16

D_MODEL = 1024
PAST_LEN = 16384
SGU_WIDTH = 512
SGU_GROUPS = 8
SGU_GROUP_DIM = SGU_WIDTH // SGU_GROUPS
RET_HEADS = 4
RET_DK = 128
RET_DV = 128
CHUNK = 128
ROPE_THETA = 10000.0
PLE_DIM = 256
RMS_EPS = 1e-6
LN_EPS = 1e-5
IN_WIDTH = 3 * SGU_WIDTH + RET_HEADS * (2 * RET_DK + 2 * RET_DV)
O_SU, O_SV, O_SG = 0, SGU_WIDTH, 2 * SGU_WIDTH
O_Q = 3 * SGU_WIDTH
O_K = O_Q + RET_HEADS * RET_DK
O_V = O_K + RET_HEADS * RET_DK
O_RG = O_V + RET_HEADS * RET_DV

LANES = 128
VMEM_LIMIT_BYTES = 60 * 1024 * 1024

PROMPT_TILE = 1024
BLOCK_ROWS = {"prompt": 256, "sample": CHUNK}
PIECE_COLS = 512
CAST_STEPS = 8
SAMPLE_TILE = 2 * CHUNK


def _log_gamma():
    return np.log(1.0 - 2.0 ** (-5.0 - np.arange(RET_HEADS, dtype=np.float64)))


def _retention_tables(seg):
    lg = _log_gamma()[:, None, None]
    r = np.arange(CHUNK)
    i, j = r[:, None] % seg, r[None, :] % seg
    same = (r[:, None] // seg) == (r[None, :] // seg)
    decay = np.where(same & (i >= j), np.exp(lg * np.maximum(i - j, 0)), 0.0)
    wq = np.broadcast_to(np.exp(lg * (i + 1.0)), (RET_HEADS, CHUNK, LANES))
    wkv = np.broadcast_to(np.exp(lg * (seg - 1.0 - i)), (RET_HEADS, CHUNK, LANES))
    scale = RET_DK ** -0.5
    tabs = np.stack([decay * scale, wq, wkv * scale]).astype(np.float32)
    g_seg = [float(v) for v in np.exp(_log_gamma() * seg)]
    return tabs, g_seg


def _rotary_tables(pos):
    half = RET_DK // 2
    inv = ROPE_THETA ** (-np.arange(half, dtype=np.float64) / half)
    ang = pos.astype(np.float64)[:, None] * inv[None, :]
    cos, sin = np.cos(ang), np.sin(ang)
    return (np.concatenate([cos, cos], axis=1).astype(np.float32),
            np.concatenate([-sin, sin], axis=1).astype(np.float32))


def _mix_mask(seg):
    r = np.arange(CHUNK)
    same = (r[:, None] // seg) == (r[None, :] // seg)
    m = same & ((r[None, :] % seg) <= (r[:, None] % seg))
    return np.concatenate([m, m], axis=1).astype(np.float32)


def _gelu(x):
    c = float(np.sqrt(2.0 / np.pi))
    half = 0.5 * x
    return half + half * jnp.tanh(x * (c + (c * 0.044715) * (x * x)))


def _silu(x):
    return x * jax.nn.sigmoid(x)


def _rotate(x, cos, sin_signed):
    return x * cos + pltpu.roll(x, RET_DK // 2, 1) * sin_signed


def _normed_bf16(x, norm_ref):
    ms = jnp.mean(x * x, axis=-1, keepdims=True)
    return (x * lax.rsqrt(ms + RMS_EPS) * norm_ref[...]).astype(BF16)


def _in_pieces(x, norm_pre_ref, w_in_ref, z_ref):
    box = {}

    def piece(c0):
        def run():
            if 'h' not in box:
                box['h'] = _normed_bf16(x, norm_pre_ref)
            z_ref[:, c0:c0 + PIECE_COLS] = jnp.dot(box['h'], w_in_ref[:, c0:c0 + PIECE_COLS],
                                                   preferred_element_type=F32)
        return run
    return [piece(c0) for c0 in range(0, IN_WIDTH, PIECE_COLS)]


def _out_pieces(x, p, mixin_ref, w_out_ref, norm_post_ref, w_gate_ref, w_ple_ref, y_ref):
    box = {}
    halves = [slice(c0, c0 + PIECE_COLS) for c0 in range(0, D_MODEL, PIECE_COLS)]

    def ple():
        box['ple'] = jnp.dot(p.astype(BF16), w_ple_ref[...], preferred_element_type=F32)

    def mix(i):
        def run():
            box['mix%d' % i] = jnp.dot(mixin_ref[...], w_out_ref[:, halves[i]], preferred_element_type=F32)
        return run

    def gate(i):
        def run():
            if 'x1' not in box:
                mixes = [box['mix%d' % j] for j in range(len(halves))]
                ms = sum(jnp.sum(m * m, axis=-1, keepdims=True) for m in mixes) * (1.0 / D_MODEL)
                scale = lax.rsqrt(ms + RMS_EPS)
                box['x1'] = [x[:, halves[j]] + mixes[j] * scale * norm_post_ref[:, halves[j]]
                             for j in range(len(halves))]
                box['x1b'] = jnp.concatenate(box['x1'], axis=1).astype(BF16)
            g = jnp.dot(box['x1b'], w_gate_ref[:, halves[i]], preferred_element_type=F32)
            y_ref[:, halves[i]] = box['x1'][i] + jax.nn.sigmoid(g) * box['ple'][:, halves[i]]
        return run
    return [ple] + [mix(i) for i in range(len(halves))] + [gate(i) for i in range(len(halves))]


def _chunk_stages(z_ref, rows, cos, sin_signed, ln_g_ref, gn_ref, bias_ref, wpair_ref, tabs_ref,
                  state_update, state_output, mixin_ref, vrows_ref, head_groups):
    box = {}
    zs = lambda off, i, width: z_ref[rows, off + i * width:off + (i + 1) * width]

    def gating_in():
        sv = _gelu(z_ref[rows, O_SV:O_SV + SGU_WIDTH])
        mu = jnp.mean(sv, axis=-1, keepdims=True)
        cen = sv - mu
        var = jnp.mean(cen * cen, axis=-1, keepdims=True)
        vn = cen * lax.rsqrt(var + LN_EPS) * ln_g_ref[...]
        if vrows_ref is not None:
            vrows_ref[rows, :] = vn
        low_lanes = lax.broadcasted_iota(jnp.int32, (CHUNK, LANES), 1) < SGU_GROUP_DIM
        for m in range(SGU_GROUPS // 2):
            blk = vn[:, m * LANES:(m + 1) * LANES]
            rhs = jnp.concatenate([jnp.where(low_lanes, blk, 0.0), jnp.where(low_lanes, 0.0, blk)],
                                  axis=0).astype(BF16)
            box['mixed%d' % m] = jnp.dot(wpair_ref[m], rhs, preferred_element_type=F32)

    def retention_in(heads):
        c, s = cos(), sin_signed()
        for h in heads:
            q = _rotate(zs(O_Q, h, RET_DK), c, s)
            k = _rotate(zs(O_K, h, RET_DK), c, s)
            v = zs(O_V, h, RET_DV)
            k_b = k.astype(BF16)
            box['scores%d' % h] = lax.dot_general(q.astype(BF16), k_b, (((1,), (1,)), ((), ())),
                                                  preferred_element_type=F32)
            box['qw%d' % h] = (q * tabs_ref[1, h]).astype(BF16)
            box['v%d' % h] = v.astype(BF16)
            box['u%d' % h] = state_update(h, k_b, (v * tabs_ref[2, h]).astype(BF16))

    def gating_out(pairs):
        for m in pairs:
            cols = slice(m * LANES, (m + 1) * LANES)
            mixed = box['mixed%d' % m] + bias_ref[:, cols]
            su = _gelu(zs(O_SU, m, LANES))
            mixin_ref[rows, cols] = (_silu(zs(O_SG, m, LANES)) * (su * mixed)).astype(BF16)

    def retention_mid(heads):
        for h in heads:
            scores_b = (box['scores%d' % h] * tabs_ref[0, h]).astype(BF16)
            box['o%d' % h] = state_output(h, scores_b, box['qw%d' % h], box['v%d' % h], box['u%d' % h])

    def retention_out(heads):
        for h in heads:
            o = box['o%d' % h]
            mu = jnp.mean(o, axis=-1, keepdims=True)
            cen = o - mu
            var = jnp.mean(cen * cen, axis=-1, keepdims=True)
            on = cen * lax.rsqrt(var + LN_EPS) * gn_ref[:, h * RET_DV:(h + 1) * RET_DV]
            mixin_ref[rows, SGU_WIDTH + h * RET_DV:SGU_WIDTH + (h + 1) * RET_DV] = (
                _silu(zs(O_RG, h, RET_DV)) * on).astype(BF16)

    part = functools.partial
    n_pairs = SGU_GROUPS // 2
    pair_groups = [range(n_pairs)] if len(head_groups) <= 2 else [range(n_pairs // 2), range(n_pairs // 2, n_pairs)]
    return ([gating_in] + [part(retention_in, g) for g in head_groups] + [part(gating_out, p) for p in pair_groups]
            + [part(retention_mid, g) for g in head_groups] + [part(retention_out, g) for g in head_groups])


def _interleave(pieces, stages):
    slots = max(len(stages), 1)
    done = 0
    for i in range(slots):
        upto = -(-(i + 1) * len(pieces) // slots)
        for piece in pieces[done:upto]:
            piece()
        done = upto
        if i < len(stages):
            stages[i]()


def _emit_blocks(in_pieces, out_pieces, stages, blocks, first_in=None, next_in=None):
    _interleave(first_in or [], [])
    for i, blk in enumerate(blocks):
        nxt = in_pieces(blocks[i + 1]) if i + 1 < len(blocks) else (next_in or [])
        if i > 0:
            ple, mix0, mix1, gate0, gate1 = out_pieces(blocks[i - 1])
            pieces = [ple, mix0, mix1] + nxt[:3] + [gate0] + nxt[3:4] + [gate1] + nxt[4:]
        else:
            pieces = nxt
        _interleave(pieces, stages(blk))
    _interleave(out_pieces(blocks[-1]), [])


def _init_tables(seg, sgu_w_ref, sgu_b_ref, mask_ref, wpm_ref, bias_ref):
    rows = lax.broadcasted_iota(jnp.int32, (CHUNK, LANES), 0)
    lanes = lax.broadcasted_iota(jnp.int32, (CHUNK, LANES), 1)
    if seg == CHUNK:
        group = lambda g: sgu_w_ref[g]
        bias_rows = sgu_b_ref[...]
    else:
        reps = CHUNK // seg
        select = jnp.where((rows < seg) & (lanes % seg == rows), 1.0, 0.0).astype(BF16)
        first = lambda a: jnp.where(lanes[:seg] < seg, a, 0.0)

        def group(g):
            stacked = jnp.concatenate([first(sgu_w_ref[g, 0:seg, :])] * reps, axis=0)
            return jnp.dot(stacked.astype(BF16), select, preferred_element_type=F32)
        b_first = jnp.where(lax.broadcasted_iota(jnp.int32, (SGU_GROUPS, LANES), 1) < seg, sgu_b_ref[...], 0.0)
        bias_rows = sum(pltpu.roll(b_first, r * seg, 1) for r in range(reps))
    for m in range(SGU_GROUPS // 2):
        pair = jnp.concatenate([group(2 * m), group(2 * m + 1)], axis=1)
        wpm_ref[m] = jnp.where(mask_ref[...] > 0.0, pair, 0.0).astype(BF16)
    group_of_lane = lax.broadcasted_iota(jnp.int32, (SGU_GROUPS, SGU_WIDTH), 1) // SGU_GROUP_DIM
    expand = jnp.where(group_of_lane == lax.broadcasted_iota(jnp.int32, (SGU_GROUPS, SGU_WIDTH), 0),
                       1.0, 0.0).astype(BF16)
    total, rest = jnp.zeros((CHUNK, SGU_WIDTH), F32), bias_rows
    for _ in range(3):
        term = rest.astype(BF16)
        total = total + lax.dot_general(term, expand, (((0,), (0,)), ((), ())), preferred_element_type=F32)
        rest = rest - term.astype(F32)
    bias_ref[...] = total


def _prompt_kernel(g_chunk, n_tiles, x_ref, x_next_ref, p_ref, cos_ref, sin_ref, w_in_ref, w_out_ref, w_gate_ref,
                   w_ple_ref, norm_pre_ref, norm_post_ref, ln_g_ref, gn_ref, sgu_w_ref, sgu_b_ref, mask_ref,
                   tabs_ref, y_ref, state_ref, w_in_b_ref, w_out_b_ref, w_gate_b_ref, w_ple_b_ref,
                   z_ref, z_first_ref, mixin_ref, wpm_ref, bias_ref):
    step = pl.program_id(0)
    first = slice(0, BLOCK_ROWS["prompt"])

    def next_first_in():
        return _in_pieces(x_next_ref[0], norm_pre_ref, w_in_b_ref, z_first_ref)

    @pl.when(step == 0)
    def _():
        _init_tables(CHUNK, sgu_w_ref, sgu_b_ref, mask_ref, wpm_ref, bias_ref)

    @pl.when(step < CAST_STEPS)
    def _():
        for src, dst in ((w_in_ref, w_in_b_ref), (w_out_ref, w_out_b_ref), (w_gate_ref, w_gate_b_ref),
                         (w_ple_ref, w_ple_b_ref)):
            n = src.shape[0]
            dst[pl.ds(pl.multiple_of(step * n, n), n), :] = src[...].astype(BF16)

    @pl.when(step == CAST_STEPS - 1)
    def _():
        _interleave(next_first_in(), [])

    @pl.when(step >= CAST_STEPS)
    def _():
        @pl.when((step - CAST_STEPS) % n_tiles == 0)
        def _():
            state_ref[...] = jnp.zeros_like(state_ref)

        def state_update(h, k_b, vw_b):
            return lax.dot_general(k_b, vw_b, (((0,), (0,)), ((), ())), preferred_element_type=F32)

        def state_output(h, scores_b, qw_b, v_b, u):
            s = state_ref[0, h]
            o = jnp.dot(jnp.concatenate([scores_b, qw_b], axis=1),
                        jnp.concatenate([v_b, s.astype(BF16)], axis=0), preferred_element_type=F32)
            state_ref[0, h] = g_chunk[h] * s + u
            return o

        def in_pieces(blk):
            return _in_pieces(x_ref[0, blk, :], norm_pre_ref, w_in_b_ref, z_ref.at[blk, :])

        def out_pieces(blk):
            return _out_pieces(x_ref[0, blk, :], p_ref[0, blk, :], mixin_ref.at[blk, :], w_out_b_ref,
                               norm_post_ref, w_gate_b_ref, w_ple_b_ref, y_ref.at[0, blk, :])

        def stages(blk):
            out = []
            for r in range(blk.start, blk.stop, CHUNK):
                rows = slice(r, r + CHUNK)
                out += _chunk_stages(z_first_ref if blk == first else z_ref, rows,
                                     lambda rows=rows: cos_ref[rows, :],
                                     lambda rows=rows: sin_ref[rows, :], ln_g_ref, gn_ref, bias_ref, wpm_ref,
                                     tabs_ref, state_update, state_output, mixin_ref, None,
                                     [range(0, RET_HEADS // 2), range(RET_HEADS // 2, RET_HEADS)])
            return out

        rows = BLOCK_ROWS["prompt"]
        _emit_blocks(in_pieces, out_pieces, stages, [slice(r, r + rows) for r in range(0, PROMPT_TILE, rows)],
                     next_in=next_first_in())


def _sample_kernel(g_seq, x_ref, p_ref, cos_ref, sin_ref, st_in_ref, w_in_ref, w_out_ref, w_gate_ref,
                   w_ple_ref, norm_pre_ref, norm_post_ref, ln_g_ref, gn_ref, sgu_w_ref, sgu_b_ref, mask_ref,
                   tabs_ref, y_ref, st_out_ref, vrows_ref, z_ref, mixin_ref, wpm_ref, bias_ref):
    seq_len = cos_ref.shape[0]
    seqs = CHUNK // seq_len

    @pl.when(pl.program_id(0) == 0)
    def _():
        _init_tables(seq_len, sgu_w_ref, sgu_b_ref, mask_ref, wpm_ref, bias_ref)

    pairs = seqs // 2
    per_pair = lambda a: a.reshape(pairs, 2 * seq_len, a.shape[-1])
    first_of_pair = lax.broadcasted_iota(jnp.int32, (1, 2 * seq_len, 1), 1) < seq_len
    tile_rows = lambda ref: jnp.concatenate([ref[...]] * seqs, axis=0)

    def stages(blk):
        pq = slice(blk.start // (2 * seq_len), blk.stop // (2 * seq_len))

        def state_update(h, k_b, vw_b):
            vw = per_pair(vw_b)
            zero = jnp.zeros_like(vw)
            both = jnp.concatenate([jnp.where(first_of_pair, vw, zero), jnp.where(first_of_pair, zero, vw)],
                                   axis=-1)
            u = jnp.einsum('pjd,pje->pde', per_pair(k_b), both, preferred_element_type=F32)
            st_out_ref[pq, 0, h] = g_seq[h] * st_in_ref[pq, 0, h] + u[:, :, :RET_DV]
            st_out_ref[pq, 1, h] = g_seq[h] * st_in_ref[pq, 1, h] + u[:, :, RET_DV:]
            return None

        def state_output(h, scores_b, qw_b, v_b, _):
            o = jnp.dot(scores_b, v_b, preferred_element_type=F32)
            s_pair = jnp.concatenate([st_in_ref[pq, 0, h], st_in_ref[pq, 1, h]], axis=-1).astype(BF16)
            both = jnp.einsum('pid,pde->pie', per_pair(qw_b), s_pair, preferred_element_type=F32)
            o_inter = jnp.where(first_of_pair, both[:, :, :RET_DV], both[:, :, RET_DV:])
            return o + o_inter.reshape(CHUNK, RET_DV)

        return _chunk_stages(z_ref, blk, lambda: tile_rows(cos_ref), lambda: tile_rows(sin_ref), ln_g_ref,
                             gn_ref, bias_ref, wpm_ref, tabs_ref, state_update, state_output, mixin_ref,
                             vrows_ref, [range(RET_HEADS)])

    def in_pieces(blk):
        return _in_pieces(x_ref[blk, :], norm_pre_ref, w_in_ref, z_ref.at[blk, :])

    def out_pieces(blk):
        return _out_pieces(x_ref[blk, :], p_ref[blk, :], mixin_ref.at[blk, :], w_out_ref, norm_post_ref,
                           w_gate_ref, w_ple_ref, y_ref.at[blk, :])

    rows = BLOCK_ROWS["sample"]
    _emit_blocks(in_pieces, out_pieces, stages, [slice(r, r + rows) for r in range(0, SAMPLE_TILE, rows)],
                 first_in=in_pieces(slice(0, rows)))


def _const_spec(shape):
    return pl.BlockSpec(shape, lambda *_: (0,) * len(shape), pipeline_mode=pl.Buffered(1))


def kernel(x_prompt, x_sample, state_ret, p_prompt, p_sample, w_in, w_out, norm_pre, norm_post, sgu_w, sgu_b,
           sgu_ln, ret_gn, w_ple_proj, w_ple_gate):
    batch, seq, _ = x_prompt.shape
    dec_batch, dec_seq, _ = x_sample.shape
    n_tiles = seq // PROMPT_TILE
    n_tok = dec_batch * dec_seq
    assert seq % PROMPT_TILE == 0 and PROMPT_TILE % BLOCK_ROWS["prompt"] == 0 and w_in.shape[0] == 1
    assert CHUNK % dec_seq == 0 and n_tok % SAMPLE_TILE == 0 and SAMPLE_TILE % BLOCK_ROWS["sample"] == 0
    assert D_MODEL % CAST_STEPS == 0 and PLE_DIM % (16 * CAST_STEPS) == 0
    assert (CHUNK // dec_seq) % 2 == 0 and dec_batch % (SAMPLE_TILE // dec_seq) == 0

    small = (norm_pre[0][None, :], norm_post[0][None, :], sgu_ln[0][None, :], ret_gn[0][None, :],
             sgu_w[0], sgu_b[0])
    small_specs = [_const_spec((1, D_MODEL)), _const_spec((1, D_MODEL)), _const_spec((1, SGU_WIDTH)),
                   _const_spec((1, RET_HEADS * RET_DV)), _const_spec((SGU_GROUPS, CHUNK, CHUNK)),
                   _const_spec((SGU_GROUPS, CHUNK)), _const_spec((CHUNK, 2 * CHUNK)),
                   _const_spec((3, RET_HEADS, CHUNK, LANES))]
    table_scratch = [pltpu.VMEM((SGU_GROUPS // 2, CHUNK, 2 * CHUNK), BF16), pltpu.VMEM((CHUNK, SGU_WIDTH), F32)]
    weight_shapes = [(D_MODEL, IN_WIDTH), (D_MODEL, D_MODEL), (D_MODEL, D_MODEL), (PLE_DIM, D_MODEL)]

    tabs_p, g_chunk = _retention_tables(CHUNK)
    cos_p, sin_p = _rotary_tables(np.arange(seq))
    tile_of = lambda s: jnp.maximum(s - CAST_STEPS, 0)
    tile_spec = lambda width: pl.BlockSpec((1, PROMPT_TILE, width),
                                           lambda s: (tile_of(s) // n_tiles, tile_of(s) % n_tiles, 0))
    pos_spec = pl.BlockSpec((PROMPT_TILE, LANES), lambda s: (tile_of(s) % n_tiles, 0))
    blocks_per_tile = PROMPT_TILE // BLOCK_ROWS["prompt"]
    next_tile = lambda s: jnp.minimum(jnp.maximum(s - CAST_STEPS + 1, 0), batch * n_tiles - 1)
    next_spec = pl.BlockSpec((1, BLOCK_ROWS["prompt"], D_MODEL),
                             lambda s: (next_tile(s) // n_tiles, (next_tile(s) % n_tiles) * blocks_per_tile, 0))
    cast_spec = lambda shape: pl.BlockSpec((shape[0] // CAST_STEPS, shape[1]),
                                           lambda s: (jnp.minimum(s, CAST_STEPS - 1), 0))
    y_prompt, st_prompt, w_in_b, w_out_b, w_gate_b, w_ple_b = pl.pallas_call(
        functools.partial(_prompt_kernel, g_chunk, n_tiles),
        grid=(CAST_STEPS + batch * n_tiles,),
        in_specs=[tile_spec(D_MODEL), next_spec, tile_spec(PLE_DIM), pos_spec, pos_spec]
        + [cast_spec(shape) for shape in weight_shapes] + small_specs,
        out_specs=[tile_spec(D_MODEL),
                   pl.BlockSpec((1, RET_HEADS, RET_DK, RET_DV), lambda s: (tile_of(s) // n_tiles, 0, 0, 0))]
        + [pl.BlockSpec(shape, lambda s: (0, 0)) for shape in weight_shapes],
        out_shape=[jax.ShapeDtypeStruct((batch, seq, D_MODEL), F32),
                   jax.ShapeDtypeStruct((batch, RET_HEADS, RET_DK, RET_DV), F32)]
        + [jax.ShapeDtypeStruct(shape, BF16) for shape in weight_shapes],
        scratch_shapes=[pltpu.VMEM((PROMPT_TILE, IN_WIDTH), F32), pltpu.VMEM((BLOCK_ROWS["prompt"], IN_WIDTH), F32),
                        pltpu.VMEM((PROMPT_TILE, D_MODEL), BF16)] + table_scratch,
        compiler_params=pltpu.CompilerParams(dimension_semantics=("arbitrary",),
                                             vmem_limit_bytes=VMEM_LIMIT_BYTES),
        name="prompt_layer",
    )(x_prompt, x_prompt, p_prompt[0], cos_p, sin_p, w_in[0], w_out[0], w_ple_gate[0], w_ple_proj[0], *small,
      _mix_mask(CHUNK), tabs_p)

    tabs_s, g_seq = _retention_tables(dec_seq)
    cos_s, sin_s = _rotary_tables(PAST_LEN + np.arange(dec_seq))
    tile_seqs = SAMPLE_TILE // dec_seq
    tok_spec = lambda width: pl.BlockSpec((SAMPLE_TILE, width), lambda i: (i, 0))
    state_spec = pl.BlockSpec((tile_seqs // 2, 2, RET_HEADS, RET_DK, RET_DV), lambda i: (i, 0, 0, 0, 0))
    paired = (dec_batch // 2, 2, RET_HEADS, RET_DK, RET_DV)
    y_sample, st_sample, v_sample = pl.pallas_call(
        functools.partial(_sample_kernel, g_seq),
        grid=(n_tok // SAMPLE_TILE,),
        in_specs=[tok_spec(D_MODEL), tok_spec(PLE_DIM), _const_spec((dec_seq, LANES)),
                  _const_spec((dec_seq, LANES)), state_spec]
        + [_const_spec(shape) for shape in weight_shapes] + small_specs,
        out_specs=[tok_spec(D_MODEL), state_spec, tok_spec(SGU_WIDTH)],
        out_shape=[jax.ShapeDtypeStruct((n_tok, D_MODEL), F32),
                   jax.ShapeDtypeStruct(paired, F32), jax.ShapeDtypeStruct((n_tok, SGU_WIDTH), F32)],
        scratch_shapes=[pltpu.VMEM((SAMPLE_TILE, IN_WIDTH), F32), pltpu.VMEM((SAMPLE_TILE, D_MODEL), BF16)]
        + table_scratch,
        compiler_params=pltpu.CompilerParams(dimension_semantics=("arbitrary",),
                                             vmem_limit_bytes=VMEM_LIMIT_BYTES),
        name="sample_layer",
    )(x_sample.reshape(n_tok, D_MODEL), p_sample[0].reshape(n_tok, PLE_DIM), cos_s, sin_s,
      state_ret[0].reshape(paired),
      w_in_b, w_out_b, w_gate_b, w_ple_b, *small, _mix_mask(dec_seq), tabs_s)

    return (y_prompt, y_sample.reshape(dec_batch, dec_seq, D_MODEL), st_prompt[None],
            st_sample.reshape(state_ret.shape),
            v_sample.reshape(1, dec_batch, dec_seq, SGU_WIDTH))
```

```python
import functools

import numpy as np
import jax
import jax.numpy as jnp
from jax import lax
from jax.experimental import pallas as pl
from jax.experimental.pallas import tpu as pltpu

F32 = jnp.float32
BF16 = jnp.bfloat16

D_MODEL = 1024
PAST_LEN = 16384
SGU_WIDTH = 512
SGU_GROUPS = 8
SGU_GROUP_DIM = SGU_WIDTH // SGU_GROUPS
RET_HEADS = 4
RET_DK = 128
RET_DV = 128
CHUNK = 128
ROPE_THETA = 10000.0
PLE_DIM = 256
RMS_EPS = 1e-6
LN_EPS = 1e-5
IN_WIDTH = 3 * SGU_WIDTH + RET_HEADS * (2 * RET_DK + 2 * RET_DV)
O_SU, O_SV, O_SG = 0, SGU_WIDTH, 2 * SGU_WIDTH
O_Q = 3 * SGU_WIDTH
O_K = O_Q + RET_HEADS * RET_DK
O_V = O_K + RET_HEADS * RET_DK
O_RG = O_V + RET_HEADS * RET_DV

LANES = 128
VMEM_LIMIT_BYTES = 60 * 1024 * 1024

PROMPT_TILE = 1024
BLOCK_ROWS = {"prompt": 256, "sample": CHUNK}
PIECE_COLS = 512
CAST_STEPS = 8
SAMPLE_TILE = 2 * CHUNK


def _log_gamma():
    return np.log(1.0 - 2.0 ** (-5.0 - np.arange(RET_HEADS, dtype=np.float64)))


def _retention_tables(seg):
    lg = _log_gamma()[:, None, None]
    r = np.arange(CHUNK)
    i, j = r[:, None] % seg, r[None, :] % seg
    same = (r[:, None] // seg) == (r[None, :] // seg)
    decay = np.where(same & (i >= j), np.exp(lg * np.maximum(i - j, 0)), 0.0)
    wq = np.broadcast_to(np.exp(lg * (i + 1.0)), (RET_HEADS, CHUNK, LANES))
    wkv = np.broadcast_to(np.exp(lg * (seg - 1.0 - i)), (RET_HEADS, CHUNK, LANES))
    scale = RET_DK ** -0.5
    tabs = np.stack([decay * scale, wq, wkv * scale]).astype(np.float32)
    g_seg = [float(v) for v in np.exp(_log_gamma() * seg)]
    return tabs, g_seg


def _rotary_tables(pos):
    half = RET_DK // 2
    inv = ROPE_THETA ** (-np.arange(half, dtype=np.float64) / half)
    ang = pos.astype(np.float64)[:, None] * inv[None, :]
    cos, sin = np.cos(ang), np.sin(ang)
    return (np.concatenate([cos, cos], axis=1).astype(np.float32),
            np.concatenate([-sin, sin], axis=1).astype(np.float32))


def _mix_mask(seg):
    r = np.arange(CHUNK)
    same = (r[:, None] // seg) == (r[None, :] // seg)
    m = same & ((r[None, :] % seg) <= (r[:, None] % seg))
    return np.concatenate([m, m], axis=1).astype(np.float32)


def _gelu(x):
    c = float(np.sqrt(2.0 / np.pi))
    half = 0.5 * x
    return half + half * jnp.tanh(x * (c + (c * 0.044715) * (x * x)))


def _silu(x):
    return x * jax.nn.sigmoid(x)


def _rotate(x, cos, sin_signed):
    return x * cos + pltpu.roll(x, RET_DK // 2, 1) * sin_signed


def _normed_bf16(x, norm_ref):
    ms = jnp.mean(x * x, axis=-1, keepdims=True)
    return (x * lax.rsqrt(ms + RMS_EPS) * norm_ref[...]).astype(BF16)


def _in_pieces(x, norm_pre_ref, w_in_ref, z_ref):
    box = {}

    def piece(c0):
        def run():
            if 'h' not in box:
                box['h'] = _normed_bf16(x, norm_pre_ref)
            z_ref[:, c0:c0 + PIECE_COLS] = jnp.dot(box['h'], w_in_ref[:, c0:c0 + PIECE_COLS],
                                                   preferred_element_type=F32)
        return run
    return [piece(c0) for c0 in range(0, IN_WIDTH, PIECE_COLS)]


def _out_pieces(x, p, mixin_ref, w_out_ref, norm_post_ref, w_gate_ref, w_ple_ref, y_ref):
    box = {}
    halves = [slice(c0, c0 + PIECE_COLS) for c0 in range(0, D_MODEL, PIECE_COLS)]

    def ple():
        box['ple'] = jnp.dot(p.astype(BF16), w_ple_ref[...], preferred_element_type=F32)

    def mix(i):
        def run():
            box['mix%d' % i] = jnp.dot(mixin_ref[...], w_out_ref[:, halves[i]], preferred_element_type=F32)
        return run

    def gate(i):
        def run():
            if 'x1' not in box:
                mixes = [box['mix%d' % j] for j in range(len(halves))]
                ms = sum(jnp.sum(m * m, axis=-1, keepdims=True) for m in mixes) * (1.0 / D_MODEL)
                scale = lax.rsqrt(ms + RMS_EPS)
                box['x1'] = [x[:, halves[j]] + mixes[j] * scale * norm_post_ref[:, halves[j]]
                             for j in range(len(halves))]
                box['x1b'] = jnp.concatenate(box['x1'], axis=1).astype(BF16)
            g = jnp.dot(box['x1b'], w_gate_ref[:, halves[i]], preferred_element_type=F32)
            y_ref[:, halves[i]] = box['x1'][i] + jax.nn.sigmoid(g) * box['ple'][:, halves[i]]
        return run
    return [ple] + [mix(i) for i in range(len(halves))] + [gate(i) for i in range(len(halves))]


def _chunk_stages(z_ref, rows, cos, sin_signed, ln_g_ref, gn_ref, bias_ref, wpair_ref, tabs_ref,
                  state_update, state_output, mixin_ref, vrows_ref, head_groups):
    box = {}
    zs = lambda off, i, width: z_ref[rows, off + i * width:off + (i + 1) * width]

    def gating_in():
        sv = _gelu(z_ref[rows, O_SV:O_SV + SGU_WIDTH])
        mu = jnp.mean(sv, axis=-1, keepdims=True)
        cen = sv - mu
        var = jnp.mean(cen * cen, axis=-1, keepdims=True)
        vn = cen * lax.rsqrt(var + LN_EPS) * ln_g_ref[...]
        if vrows_ref is not None:
            vrows_ref[rows, :] = vn
        low_lanes = lax.broadcasted_iota(jnp.int32, (CHUNK, LANES), 1) < SGU_GROUP_DIM
        for m in range(SGU_GROUPS // 2):
            blk = vn[:, m * LANES:(m + 1) * LANES]
            rhs = jnp.concatenate([jnp.where(low_lanes, blk, 0.0), jnp.where(low_lanes, 0.0, blk)],
                                  axis=0).astype(BF16)
            box['mixed%d' % m] = jnp.dot(wpair_ref[m], rhs, preferred_element_type=F32)

    def retention_in(heads):
        c, s = cos(), sin_signed()
        for h in heads:
            q = _rotate(zs(O_Q, h, RET_DK), c, s)
            k = _rotate(zs(O_K, h, RET_DK), c, s)
            v = zs(O_V, h, RET_DV)
            k_b = k.astype(BF16)
            box['scores%d' % h] = lax.dot_general(q.astype(BF16), k_b, (((1,), (1,)), ((), ())),
                                                  preferred_element_type=F32)
            box['qw%d' % h] = (q * tabs_ref[1, h]).astype(BF16)
            box['v%d' % h] = v.astype(BF16)
            box['u%d' % h] = state_update(h, k_b, (v * tabs_ref[2, h]).astype(BF16))

    def gating_out(pairs):
        for m in pairs:
            cols = slice(m * LANES, (m + 1) * LANES)
            mixed = box['mixed%d' % m] + bias_ref[:, cols]
            su = _gelu(zs(O_SU, m, LANES))
            mixin_ref[rows, cols] = (_silu(zs(O_SG, m, LANES)) * (su * mixed)).astype(BF16)

    def retention_mid(heads):
        for h in heads:
            scores_b = (box['scores%d' % h] * tabs_ref[0, h]).astype(BF16)
            box['o%d' % h] = state_output(h, scores_b, box['qw%d' % h], box['v%d' % h], box['u%d' % h])

    def retention_out(heads):
        for h in heads:
            o = box['o%d' % h]
            mu = jnp.mean(o, axis=-1, keepdims=True)
            cen = o - mu
            var = jnp.mean(cen * cen, axis=-1, keepdims=True)
            on = cen * lax.rsqrt(var + LN_EPS) * gn_ref[:, h * RET_DV:(h + 1) * RET_DV]
            mixin_ref[rows, SGU_WIDTH + h * RET_DV:SGU_WIDTH + (h + 1) * RET_DV] = (
                _silu(zs(O_RG, h, RET_DV)) * on).astype(BF16)

    part = functools.partial
    n_pairs = SGU_GROUPS // 2
    pair_groups = [range(n_pairs)] if len(head_groups) <= 2 else [range(n_pairs // 2), range(n_pairs // 2, n_pairs)]
    return ([gating_in] + [part(retention_in, g) for g in head_groups] + [part(gating_out, p) for p in pair_groups]
            + [part(retention_mid, g) for g in head_groups] + [part(retention_out, g) for g in head_groups])


def _interleave(pieces, stages):
    slots = max(len(stages), 1)
    done = 0
    for i in range(slots):
        upto = -(-(i + 1) * len(pieces) // slots)
        for piece in pieces[done:upto]:
            piece()
        done = upto
        if i < len(stages):
            stages[i]()


def _emit_blocks(in_pieces, out_pieces, stages, blocks, first_in=None, next_in=None):
    _interleave(first_in or [], [])
    for i, blk in enumerate(blocks):
        nxt = in_pieces(blocks[i + 1]) if i + 1 < len(blocks) else (next_in or [])
        if i > 0:
            ple, mix0, mix1, gate0, gate1 = out_pieces(blocks[i - 1])
            pieces = [ple, mix0, mix1] + nxt[:3] + [gate0] + nxt[3:4] + [gate1] + nxt[4:]
        else:
            pieces = nxt
        _interleave(pieces, stages(blk))
    _interleave(out_pieces(blocks[-1]), [])


def _init_tables(seg, sgu_w_ref, sgu_b_ref, mask_ref, wpm_ref, bias_ref):
    rows = lax.broadcasted_iota(jnp.int32, (CHUNK, LANES), 0)
    lanes = lax.broadcasted_iota(jnp.int32, (CHUNK, LANES), 1)
    if seg == CHUNK:
        group = lambda g: sgu_w_ref[g]
        bias_rows = sgu_b_ref[...]
    else:
        reps = CHUNK // seg
        select = jnp.where((rows < seg) & (lanes % seg == rows), 1.0, 0.0).astype(BF16)
        first = lambda a: jnp.where(lanes[:seg] < seg, a, 0.0)

        def group(g):
            stacked = jnp.concatenate([first(sgu_w_ref[g, 0:seg, :])] * reps, axis=0)
            return jnp.dot(stacked.astype(BF16), select, preferred_element_type=F32)
        b_first = jnp.where(lax.broadcasted_iota(jnp.int32, (SGU_GROUPS, LANES), 1) < seg, sgu_b_ref[...], 0.0)
        bias_rows = sum(pltpu.roll(b_first, r * seg, 1) for r in range(reps))
    for m in range(SGU_GROUPS // 2):
        pair = jnp.concatenate([group(2 * m), group(2 * m + 1)], axis=1)
        wpm_ref[m] = jnp.where(mask_ref[...] > 0.0, pair, 0.0).astype(BF16)
    group_of_lane = lax.broadcasted_iota(jnp.int32, (SGU_GROUPS, SGU_WIDTH), 1) // SGU_GROUP_DIM
    expand = jnp.where(group_of_lane == lax.broadcasted_iota(jnp.int32, (SGU_GROUPS, SGU_WIDTH), 0),
                       1.0, 0.0).astype(BF16)
    total, rest = jnp.zeros((CHUNK, SGU_WIDTH), F32), bias_rows
    for _ in range(3):
        term = rest.astype(BF16)
        total = total + lax.dot_general(term, expand, (((0,), (0,)), ((), ())), preferred_element_type=F32)
        rest = rest - term.astype(F32)
    bias_ref[...] = total


def _prompt_kernel(g_chunk, n_tiles, dec_seq, x_ref, x_next_ref, p_ref, cos_ref, sin_ref, w_in_ref, w_out_ref,
                   w_gate_ref, w_ple_ref, norm_pre_ref, norm_post_ref, ln_g_ref, gn_ref, sgu_w_ref, sgu_b_ref,
                   mask_ref, tabs_ref, mask_dec_ref, y_ref, state_ref, w_in_b_ref, w_out_b_ref, w_gate_b_ref,
                   w_ple_b_ref, wpm_dec_ref, bias_dec_ref, z_ref, z_first_ref, mixin_ref, wpm_ref, bias_ref):
    step = pl.program_id(0)
    first = slice(0, BLOCK_ROWS["prompt"])

    def next_first_in():
        return _in_pieces(x_next_ref[0], norm_pre_ref, w_in_b_ref, z_first_ref)

    @pl.when(step == 0)
    def _():
        _init_tables(CHUNK, sgu_w_ref, sgu_b_ref, mask_ref, wpm_ref, bias_ref)

    @pl.when(step == 1)
    def _():
        _init_tables(dec_seq, sgu_w_ref, sgu_b_ref, mask_dec_ref, wpm_dec_ref, bias_dec_ref)

    @pl.when(step < CAST_STEPS)
    def _():
        for src, dst in ((w_in_ref, w_in_b_ref), (w_out_ref, w_out_b_ref), (w_gate_ref, w_gate_b_ref),
                         (w_ple_ref, w_ple_b_ref)):
            n = src.shape[0]
            dst[pl.ds(pl.multiple_of(step * n, n), n), :] = src[...].astype(BF16)

    @pl.when(step == CAST_STEPS - 1)
    def _():
        _interleave(next_first_in(), [])

    @pl.when(step >= CAST_STEPS)
    def _():
        @pl.when((step - CAST_STEPS) % n_tiles == 0)
        def _():
            state_ref[...] = jnp.zeros_like(state_ref)

        def state_update(h, k_b, vw_b):
            return lax.dot_general(k_b, vw_b, (((0,), (0,)), ((), ())), preferred_element_type=F32)

        def state_output(h, scores_b, qw_b, v_b, u):
            s = state_ref[0, h]
            o = jnp.dot(jnp.concatenate([scores_b, qw_b], axis=1),
                        jnp.concatenate([v_b, s.astype(BF16)], axis=0), preferred_element_type=F32)
            state_ref[0, h] = g_chunk[h] * s + u
            return o

        def in_pieces(blk):
            return _in_pieces(x_ref[0, blk, :], norm_pre_ref, w_in_b_ref, z_ref.at[blk, :])

        def out_pieces(blk):
            return _out_pieces(x_ref[0, blk, :], p_ref[0, blk, :], mixin_ref.at[blk, :], w_out_b_ref,
                               norm_post_ref, w_gate_b_ref, w_ple_b_ref, y_ref.at[0, blk, :])

        def stages(blk):
            out = []
            for r in range(blk.start, blk.stop, CHUNK):
                rows = slice(r, r + CHUNK)
                out += _chunk_stages(z_first_ref if blk == first else z_ref, rows,
                                     lambda rows=rows: cos_ref[rows, :],
                                     lambda rows=rows: sin_ref[rows, :], ln_g_ref, gn_ref, bias_ref, wpm_ref,
                                     tabs_ref, state_update, state_output, mixin_ref, None,
                                     [range(0, RET_HEADS // 2), range(RET_HEADS // 2, RET_HEADS)])
            return out

        rows = BLOCK_ROWS["prompt"]
        _emit_blocks(in_pieces, out_pieces, stages, [slice(r, r + rows) for r in range(0, PROMPT_TILE, rows)],
                     next_in=next_first_in())


def _sample_kernel(g_seq, x_ref, p_ref, cos_ref, sin_ref, st_in_ref, w_in_ref, w_out_ref, w_gate_ref,
                   w_ple_ref, norm_pre_ref, norm_post_ref, ln_g_ref, gn_ref, wpm_ref, bias_ref, tabs_ref,
                   y_ref, st_out_ref, vrows_ref, z_ref, mixin_ref):
    seq_len = cos_ref.shape[0]
    seqs = CHUNK // seq_len

    pairs = seqs // 2
    per_pair = lambda a: a.reshape(pairs, 2 * seq_len, a.shape[-1])
    first_of_pair = lax.broadcasted_iota(jnp.int32, (1, 2 * seq_len, 1), 1) < seq_len
    tile_rows = lambda ref: jnp.concatenate([ref[...]] * seqs, axis=0)

    def stages(blk):
        pq = slice(blk.start // (2 * seq_len), blk.stop // (2 * seq_len))

        def state_update(h, k_b, vw_b):
            vw = per_pair(vw_b)
            zero = jnp.zeros_like(vw)
            both = jnp.concatenate([jnp.where(first_of_pair, vw, zero), jnp.where(first_of_pair, zero, vw)],
                                   axis=-1)
            u = jnp.einsum('pjd,pje->pde', per_pair(k_b), both, preferred_element_type=F32)
            st_out_ref[pq, 0, h] = g_seq[h] * st_in_ref[pq, 0, h] + u[:, :, :RET_DV]
            st_out_ref[pq, 1, h] = g_seq[h] * st_in_ref[pq, 1, h] + u[:, :, RET_DV:]
            return None

        def state_output(h, scores_b, qw_b, v_b, _):
            o = jnp.dot(scores_b, v_b, preferred_element_type=F32)
            s_pair = jnp.concatenate([st_in_ref[pq, 0, h], st_in_ref[pq, 1, h]], axis=-1).astype(BF16)
            both = jnp.einsum('pid,pde->pie', per_pair(qw_b), s_pair, preferred_element_type=F32)
            o_inter = jnp.where(first_of_pair, both[:, :, :RET_DV], both[:, :, RET_DV:])
            return o + o_inter.reshape(CHUNK, RET_DV)

        return _chunk_stages(z_ref, blk, lambda: tile_rows(cos_ref), lambda: tile_rows(sin_ref), ln_g_ref,
                             gn_ref, bias_ref, wpm_ref, tabs_ref, state_update, state_output, mixin_ref,
                             vrows_ref, [range(RET_HEADS)])

    def in_pieces(blk):
        return _in_pieces(x_ref[blk, :], norm_pre_ref, w_in_ref, z_ref.at[blk, :])

    def out_pieces(blk):
        return _out_pieces(x_ref[blk, :], p_ref[blk, :], mixin_ref.at[blk, :], w_out_ref, norm_post_ref,
                           w_gate_ref, w_ple_ref, y_ref.at[blk, :])

    rows = BLOCK_ROWS["sample"]
    _emit_blocks(in_pieces, out_pieces, stages, [slice(r, r + rows) for r in range(0, SAMPLE_TILE, rows)],
                 first_in=in_pieces(slice(0, rows)))


def _const_spec(shape):
    return pl.BlockSpec(shape, lambda *_: (0,) * len(shape), pipeline_mode=pl.Buffered(1))


def kernel(x_prompt, x_sample, state_ret, p_prompt, p_sample, w_in, w_out, norm_pre, norm_post, sgu_w, sgu_b,
           sgu_ln, ret_gn, w_ple_proj, w_ple_gate):
    batch, seq, _ = x_prompt.shape
    dec_batch, dec_seq, _ = x_sample.shape
    n_tiles = seq // PROMPT_TILE
    n_tok = dec_batch * dec_seq
    assert seq % PROMPT_TILE == 0 and PROMPT_TILE % BLOCK_ROWS["prompt"] == 0 and w_in.shape[0] == 1
    assert CHUNK % dec_seq == 0 and n_tok % SAMPLE_TILE == 0 and SAMPLE_TILE % BLOCK_ROWS["sample"] == 0
    assert D_MODEL % CAST_STEPS == 0 and PLE_DIM % (16 * CAST_STEPS) == 0
    assert (CHUNK // dec_seq) % 2 == 0 and dec_batch % (SAMPLE_TILE // dec_seq) == 0

    small = (norm_pre[0][None, :], norm_post[0][None, :], sgu_ln[0][None, :], ret_gn[0][None, :],
             sgu_w[0], sgu_b[0])
    small_specs = [_const_spec((1, D_MODEL)), _const_spec((1, D_MODEL)), _const_spec((1, SGU_WIDTH)),
                   _const_spec((1, RET_HEADS * RET_DV)), _const_spec((SGU_GROUPS, CHUNK, CHUNK)),
                   _const_spec((SGU_GROUPS, CHUNK)), _const_spec((CHUNK, 2 * CHUNK)),
                   _const_spec((3, RET_HEADS, CHUNK, LANES))]
    table_shapes = [((SGU_GROUPS // 2, CHUNK, 2 * CHUNK), BF16), ((CHUNK, SGU_WIDTH), F32)]
    weight_shapes = [(D_MODEL, IN_WIDTH), (D_MODEL, D_MODEL), (D_MODEL, D_MODEL), (PLE_DIM, D_MODEL)]

    tabs_p, g_chunk = _retention_tables(CHUNK)
    cos_p, sin_p = _rotary_tables(np.arange(seq))
    tile_of = lambda s: jnp.maximum(s - CAST_STEPS, 0)
    tile_spec = lambda width: pl.BlockSpec((1, PROMPT_TILE, width),
                                           lambda s: (tile_of(s) // n_tiles, tile_of(s) % n_tiles, 0))
    pos_spec = pl.BlockSpec((PROMPT_TILE, LANES), lambda s: (tile_of(s) % n_tiles, 0))
    blocks_per_tile = PROMPT_TILE // BLOCK_ROWS["prompt"]
    next_tile = lambda s: jnp.minimum(jnp.maximum(s - CAST_STEPS + 1, 0), batch * n_tiles - 1)
    next_spec = pl.BlockSpec((1, BLOCK_ROWS["prompt"], D_MODEL),
                             lambda s: (next_tile(s) // n_tiles, (next_tile(s) % n_tiles) * blocks_per_tile, 0))
    cast_spec = lambda shape: pl.BlockSpec((shape[0] // CAST_STEPS, shape[1]),
                                           lambda s: (jnp.minimum(s, CAST_STEPS - 1), 0))
    resident = lambda shape: pl.BlockSpec(shape, lambda s: (0,) * len(shape))
    y_prompt, st_prompt, w_in_b, w_out_b, w_gate_b, w_ple_b, wpm_dec, bias_dec = pl.pallas_call(
        functools.partial(_prompt_kernel, g_chunk, n_tiles, dec_seq),
        grid=(CAST_STEPS + batch * n_tiles,),
        in_specs=[tile_spec(D_MODEL), next_spec, tile_spec(PLE_DIM), pos_spec, pos_spec]
        + [cast_spec(shape) for shape in weight_shapes] + small_specs + [_const_spec((CHUNK, 2 * CHUNK))],
        out_specs=[tile_spec(D_MODEL),
                   pl.BlockSpec((1, RET_HEADS, RET_DK, RET_DV), lambda s: (tile_of(s) // n_tiles, 0, 0, 0))]
        + [resident(shape) for shape in weight_shapes] + [resident(shape) for shape, _ in table_shapes],
        out_shape=[jax.ShapeDtypeStruct((batch, seq, D_MODEL), F32),
                   jax.ShapeDtypeStruct((batch, RET_HEADS, RET_DK, RET_DV), F32)]
        + [jax.ShapeDtypeStruct(shape, BF16) for shape in weight_shapes]
        + [jax.ShapeDtypeStruct(shape, dtype) for shape, dtype in table_shapes],
        scratch_shapes=[pltpu.VMEM((PROMPT_TILE, IN_WIDTH), F32), pltpu.VMEM((BLOCK_ROWS["prompt"], IN_WIDTH), F32),
                        pltpu.VMEM((PROMPT_TILE, D_MODEL), BF16)]
        + [pltpu.VMEM(shape, dtype) for shape, dtype in table_shapes],
        compiler_params=pltpu.CompilerParams(dimension_semantics=("arbitrary",),
                                             vmem_limit_bytes=VMEM_LIMIT_BYTES),
        name="prompt_layer",
    )(x_prompt, x_prompt, p_prompt[0], cos_p, sin_p, w_in[0], w_out[0], w_ple_gate[0], w_ple_proj[0], *small,
      _mix_mask(CHUNK), tabs_p, _mix_mask(dec_seq))

    tabs_s, g_seq = _retention_tables(dec_seq)
    cos_s, sin_s = _rotary_tables(PAST_LEN + np.arange(dec_seq))
    tile_seqs = SAMPLE_TILE // dec_seq
    tok_spec = lambda width: pl.BlockSpec((SAMPLE_TILE, width), lambda i: (i, 0))
    state_spec = pl.BlockSpec((tile_seqs // 2, 2, RET_HEADS, RET_DK, RET_DV), lambda i: (i, 0, 0, 0, 0))
    paired = (dec_batch // 2, 2, RET_HEADS, RET_DK, RET_DV)
    y_sample, st_sample, v_sample = pl.pallas_call(
        functools.partial(_sample_kernel, g_seq),
        grid=(n_tok // SAMPLE_TILE,),
        in_specs=[tok_spec(D_MODEL), tok_spec(PLE_DIM), _const_spec((dec_seq, LANES)),
                  _const_spec((dec_seq, LANES)), state_spec]
        + [_const_spec(shape) for shape in weight_shapes] + small_specs[:4]
        + [_const_spec(shape) for shape, _ in table_shapes] + small_specs[-1:],
        out_specs=[tok_spec(D_MODEL), state_spec, tok_spec(SGU_WIDTH)],
        out_shape=[jax.ShapeDtypeStruct((n_tok, D_MODEL), F32),
                   jax.ShapeDtypeStruct(paired, F32), jax.ShapeDtypeStruct((n_tok, SGU_WIDTH), F32)],
        scratch_shapes=[pltpu.VMEM((SAMPLE_TILE, IN_WIDTH), F32), pltpu.VMEM((SAMPLE_TILE, D_MODEL), BF16)],
        compiler_params=pltpu.CompilerParams(dimension_semantics=("arbitrary",),
                                             vmem_limit_bytes=VMEM_LIMIT_BYTES),
        name="sample_layer",
    )(x_sample.reshape(n_tok, D_MODEL), p_sample[0].reshape(n_tok, PLE_DIM), cos_s, sin_s,
      state_ret[0].reshape(paired),
      w_in_b, w_out_b, w_gate_b, w_ple_b, *small[:4], wpm_dec, bias_dec, tabs_s)

    return (y_prompt, y_sample.reshape(dec_batch, dec_seq, D_MODEL), st_prompt[None],
            st_sample.reshape(state_ret.shape),
            v_sample.reshape(1, dec_batch, dec_seq, SGU_WIDTH))
```

```python
import functools

import numpy as np
import jax
import jax.numpy as jnp
from jax import lax
from jax.experimental import pallas as pl
from jax.experimental.pallas import tpu as pltpu

F32 = jnp.float32
BF16 = jnp.bfloat16

D_MODEL = 1024
PAST_LEN = 16384
SGU_WIDTH = 512
SGU_GROUPS = 8
SGU_GROUP_DIM = SGU_WIDTH // SGU_GROUPS
RET_HEADS = 4
RET_DK = 128
RET_DV = 128
CHUNK = 128
ROPE_THETA = 10000.0
PLE_DIM = 256
RMS_EPS = 1e-6
LN_EPS = 1e-5
IN_WIDTH = 3 * SGU_WIDTH + RET_HEADS * (2 * RET_DK + 2 * RET_DV)
O_SU, O_SV, O_SG = 0, SGU_WIDTH, 2 * SGU_WIDTH
O_Q = 3 * SGU_WIDTH
O_K = O_Q + RET_HEADS * RET_DK
O_V = O_K + RET_HEADS * RET_DK
O_RG = O_V + RET_HEADS * RET_DV

LANES = 128
VMEM_LIMIT_BYTES = 60 * 1024 * 1024

PROMPT_TILE = 1024
BLOCK_ROWS = {"prompt": 256, "sample": CHUNK}
PIECE_COLS = 512
CAST_STEPS = 8
SAMPLE_TILE = 2 * CHUNK


def _log_gamma():
    return np.log(1.0 - 2.0 ** (-5.0 - np.arange(RET_HEADS, dtype=np.float64)))


def _retention_tables(seg):
    lg = _log_gamma()[:, None, None]
    r = np.arange(CHUNK)
    i, j = r[:, None] % seg, r[None, :] % seg
    same = (r[:, None] // seg) == (r[None, :] // seg)
    decay = np.where(same & (i >= j), np.exp(lg * np.maximum(i - j, 0)), 0.0)
    wq = np.broadcast_to(np.exp(lg * (i + 1.0)), (RET_HEADS, CHUNK, LANES))
    wkv = np.broadcast_to(np.exp(lg * (seg - 1.0 - i)), (RET_HEADS, CHUNK, LANES))
    scale = RET_DK ** -0.5
    tabs = np.stack([decay * scale, wq, wkv * scale]).astype(np.float32)
    g_seg = [float(v) for v in np.exp(_log_gamma() * seg)]
    return tabs, g_seg


def _rotary_tables(pos):
    half = RET_DK // 2
    inv = ROPE_THETA ** (-np.arange(half, dtype=np.float64) / half)
    ang = pos.astype(np.float64)[:, None] * inv[None, :]
    cos, sin = np.cos(ang), np.sin(ang)
    return (np.concatenate([cos, cos], axis=1).astype(np.float32),
            np.concatenate([-sin, sin], axis=1).astype(np.float32))


def _mix_mask(seg):
    r = np.arange(CHUNK)
    same = (r[:, None] // seg) == (r[None, :] // seg)
    m = same & ((r[None, :] % seg) <= (r[:, None] % seg))
    return np.concatenate([m, m], axis=1).astype(np.float32)


def _gelu(x):
    c = float(np.sqrt(2.0 / np.pi))
    half = 0.5 * x
    return half + half * jnp.tanh(x * (c + (c * 0.044715) * (x * x)))


def _silu(x):
    return x * jax.nn.sigmoid(x)


def _rotate(x, cos, sin_signed):
    return x * cos + pltpu.roll(x, RET_DK // 2, 1) * sin_signed


def _normed_bf16(x, norm_ref):
    ms = jnp.mean(x * x, axis=-1, keepdims=True)
    return (x * lax.rsqrt(ms + RMS_EPS) * norm_ref[...]).astype(BF16)


def _in_pieces(x, norm_pre_ref, w_in_ref, z_ref):
    box = {}

    def piece(c0):
        def run():
            if 'h' not in box:
                box['h'] = _normed_bf16(x, norm_pre_ref)
            z_ref[:, c0:c0 + PIECE_COLS] = jnp.dot(box['h'], w_in_ref[:, c0:c0 + PIECE_COLS],
                                                   preferred_element_type=F32)
        return run
    return [piece(c0) for c0 in range(0, IN_WIDTH, PIECE_COLS)]


def _out_pieces(x, p, mixin_ref, w_out_ref, norm_post_ref, w_gate_ref, w_ple_ref, y_ref):
    box = {}
    halves = [slice(c0, c0 + PIECE_COLS) for c0 in range(0, D_MODEL, PIECE_COLS)]

    def ple():
        box['ple'] = jnp.dot(p.astype(BF16), w_ple_ref[...], preferred_element_type=F32)

    def mix(i):
        def run():
            box['mix%d' % i] = jnp.dot(mixin_ref[...], w_out_ref[:, halves[i]], preferred_element_type=F32)
        return run

    def gate(i):
        def run():
            if 'x1' not in box:
                mixes = [box['mix%d' % j] for j in range(len(halves))]
                ms = sum(jnp.sum(m * m, axis=-1, keepdims=True) for m in mixes) * (1.0 / D_MODEL)
                scale = lax.rsqrt(ms + RMS_EPS)
                box['x1'] = [x[:, halves[j]] + mixes[j] * scale * norm_post_ref[:, halves[j]]
                             for j in range(len(halves))]
                box['x1b'] = jnp.concatenate(box['x1'], axis=1).astype(BF16)
            g = jnp.dot(box['x1b'], w_gate_ref[:, halves[i]], preferred_element_type=F32)
            y_ref[:, halves[i]] = box['x1'][i] + jax.nn.sigmoid(g) * box['ple'][:, halves[i]]
        return run
    return [ple] + [mix(i) for i in range(len(halves))] + [gate(i) for i in range(len(halves))]


def _chunk_stages(z_ref, rows, cos, sin_signed, ln_g_ref, gn_ref, bias_ref, wpair_ref, tabs_ref,
                  state_update, state_output, mixin_ref, vrows_ref, head_groups):
    box = {}
    zs = lambda off, i, width: z_ref[rows, off + i * width:off + (i + 1) * width]

    def gating_in():
        sv = _gelu(z_ref[rows, O_SV:O_SV + SGU_WIDTH])
        mu = jnp.mean(sv, axis=-1, keepdims=True)
        cen = sv - mu
        var = jnp.mean(cen * cen, axis=-1, keepdims=True)
        vn = cen * lax.rsqrt(var + LN_EPS) * ln_g_ref[...]
        if vrows_ref is not None:
            vrows_ref[rows, :] = vn
        low_lanes = lax.broadcasted_iota(jnp.int32, (CHUNK, LANES), 1) < SGU_GROUP_DIM
        for m in range(SGU_GROUPS // 2):
            blk = vn[:, m * LANES:(m + 1) * LANES]
            rhs = jnp.concatenate([jnp.where(low_lanes, blk, 0.0), jnp.where(low_lanes, 0.0, blk)],
                                  axis=0).astype(BF16)
            box['mixed%d' % m] = jnp.dot(wpair_ref[m], rhs, preferred_element_type=F32)

    def retention_in(heads):
        c, s = cos(), sin_signed()
        for h in heads:
            q = _rotate(zs(O_Q, h, RET_DK), c, s)
            k = _rotate(zs(O_K, h, RET_DK), c, s)
            v = zs(O_V, h, RET_DV)
            k_b = k.astype(BF16)
            box['scores%d' % h] = lax.dot_general(q.astype(BF16), k_b, (((1,), (1,)), ((), ())),
                                                  preferred_element_type=F32)
            box['qw%d' % h] = (q * tabs_ref[1, h]).astype(BF16)
            box['v%d' % h] = v.astype(BF16)
            box['u%d' % h] = state_update(h, k_b, (v * tabs_ref[2, h]).astype(BF16))

    def gating_out(pairs):
        for m in pairs:
            cols = slice(m * LANES, (m + 1) * LANES)
            mixed = box['mixed%d' % m] + bias_ref[:, cols]
            su = _gelu(zs(O_SU, m, LANES))
            mixin_ref[rows, cols] = (_silu(zs(O_SG, m, LANES)) * (su * mixed)).astype(BF16)

    def retention_mid(heads):
        for h in heads:
            scores_b = (box['scores%d' % h] * tabs_ref[0, h]).astype(BF16)
            box['o%d' % h] = state_output(h, scores_b, box['qw%d' % h], box['v%d' % h], box['u%d' % h])

    def retention_out(heads):
        for h in heads:
            o = box['o%d' % h]
            mu = jnp.mean(o, axis=-1, keepdims=True)
            cen = o - mu
            var = jnp.mean(cen * cen, axis=-1, keepdims=True)
            on = cen * lax.rsqrt(var + LN_EPS) * gn_ref[:, h * RET_DV:(h + 1) * RET_DV]
            mixin_ref[rows, SGU_WIDTH + h * RET_DV:SGU_WIDTH + (h + 1) * RET_DV] = (
                _silu(zs(O_RG, h, RET_DV)) * on).astype(BF16)

    part = functools.partial
    n_pairs = SGU_GROUPS // 2
    pair_groups = [range(n_pairs)] if len(head_groups) <= 2 else [range(n_pairs // 2), range(n_pairs // 2, n_pairs)]
    return ([gating_in] + [part(retention_in, g) for g in head_groups] + [part(gating_out, p) for p in pair_groups]
            + [part(retention_mid, g) for g in head_groups] + [part(retention_out, g) for g in head_groups])


def _interleave(pieces, stages):
    slots = max(len(stages), 1)
    done = 0
    for i in range(slots):
        upto = -(-(i + 1) * len(pieces) // slots)
        for piece in pieces[done:upto]:
            piece()
        done = upto
        if i < len(stages):
            stages[i]()


def _emit_blocks(in_pieces, out_pieces, stages, blocks, first_in=None, next_in=None):
    _interleave(first_in or [], [])
    for i, blk in enumerate(blocks):
        nxt = in_pieces(blocks[i + 1]) if i + 1 < len(blocks) else (next_in or [])
        if i > 0:
            ple, mix0, mix1, gate0, gate1 = out_pieces(blocks[i - 1])
            pieces = [ple, mix0, mix1] + nxt[:3] + [gate0] + nxt[3:4] + [gate1] + nxt[4:]
        else:
            pieces = nxt
        _interleave(pieces, stages(blk))
    _interleave(out_pieces(blocks[-1]), [])


def _init_tables(seg, sgu_w_ref, sgu_b_ref, mask_ref, wpm_ref, bias_ref):
    rows = lax.broadcasted_iota(jnp.int32, (CHUNK, LANES), 0)
    lanes = lax.broadcasted_iota(jnp.int32, (CHUNK, LANES), 1)
    if seg == CHUNK:
        group = lambda g: sgu_w_ref[g]
        bias_rows = sgu_b_ref[...]
    else:
        reps = CHUNK // seg
        select = jnp.where((rows < seg) & (lanes % seg == rows), 1.0, 0.0).astype(BF16)
        first = lambda a: jnp.where(lanes[:seg] < seg, a, 0.0)

        def group(g):
            stacked = jnp.concatenate([first(sgu_w_ref[g, 0:seg, :])] * reps, axis=0)
            return jnp.dot(stacked.astype(BF16), select, preferred_element_type=F32)
        b_first = jnp.where(lax.broadcasted_iota(jnp.int32, (SGU_GROUPS, LANES), 1) < seg, sgu_b_ref[...], 0.0)
        bias_rows = sum(pltpu.roll(b_first, r * seg, 1) for r in range(reps))
    for m in range(SGU_GROUPS // 2):
        pair = jnp.concatenate([group(2 * m), group(2 * m + 1)], axis=1)
        wpm_ref[m] = jnp.where(mask_ref[...] > 0.0, pair, 0.0).astype(BF16)
    group_of_lane = lax.broadcasted_iota(jnp.int32, (SGU_GROUPS, SGU_WIDTH), 1) // SGU_GROUP_DIM
    expand = jnp.where(group_of_lane == lax.broadcasted_iota(jnp.int32, (SGU_GROUPS, SGU_WIDTH), 0),
                       1.0, 0.0).astype(BF16)
    total, rest = jnp.zeros((CHUNK, SGU_WIDTH), F32), bias_rows
    for _ in range(3):
        term = rest.astype(BF16)
        total = total + lax.dot_general(term, expand, (((0,), (0,)), ((), ())), preferred_element_type=F32)
        rest = rest - term.astype(F32)
    bias_ref[...] = total


def _prompt_kernel(g_chunk, n_tiles, dec_seq, x_ref, x_next_ref, p_ref, cos_ref, sin_ref, w_in_ref, w_out_ref,
                   w_gate_ref, w_ple_ref, norm_pre_ref, norm_post_ref, ln_g_ref, gn_ref, sgu_w_ref, sgu_b_ref,
                   mask_ref, tabs_ref, mask_dec_ref, y_ref, state_ref, w_in_hbm, w_out_hbm, w_gate_hbm,
                   w_ple_hbm, wpm_dec_ref, bias_dec_ref, z_ref, z_first_ref, mixin_ref, wpm_ref, bias_ref,
                   w_in_b_ref, w_out_b_ref, w_gate_b_ref, w_ple_b_ref, w_sem):
    step = pl.program_id(0)
    first = slice(0, BLOCK_ROWS["prompt"])
    w_copies = [pltpu.make_async_copy(src, dst, w_sem.at[i]) for i, (src, dst) in enumerate(
        ((w_in_b_ref, w_in_hbm), (w_out_b_ref, w_out_hbm), (w_gate_b_ref, w_gate_hbm), (w_ple_b_ref, w_ple_hbm)))]

    def next_first_in():
        return _in_pieces(x_next_ref[0], norm_pre_ref, w_in_b_ref, z_first_ref)

    @pl.when(step == 0)
    def _():
        _init_tables(CHUNK, sgu_w_ref, sgu_b_ref, mask_ref, wpm_ref, bias_ref)

    @pl.when(step == 1)
    def _():
        _init_tables(dec_seq, sgu_w_ref, sgu_b_ref, mask_dec_ref, wpm_dec_ref, bias_dec_ref)

    @pl.when(step < CAST_STEPS)
    def _():
        for src, dst in ((w_in_ref, w_in_b_ref), (w_out_ref, w_out_b_ref), (w_gate_ref, w_gate_b_ref),
                         (w_ple_ref, w_ple_b_ref)):
            n = src.shape[0]
            dst[pl.ds(pl.multiple_of(step * n, n), n), :] = src[...].astype(BF16)

    @pl.when(step == CAST_STEPS - 1)
    def _():
        _interleave(next_first_in(), [])

    @pl.when(step == CAST_STEPS)
    def _():
        for copy in w_copies:
            copy.start()

    @pl.when(step >= CAST_STEPS)
    def _():
        @pl.when((step - CAST_STEPS) % n_tiles == 0)
        def _():
            state_ref[...] = jnp.zeros_like(state_ref)

        def state_update(h, k_b, vw_b):
            return lax.dot_general(k_b, vw_b, (((0,), (0,)), ((), ())), preferred_element_type=F32)

        def state_output(h, scores_b, qw_b, v_b, u):
            s = state_ref[0, h]
            o = jnp.dot(jnp.concatenate([scores_b, qw_b], axis=1),
                        jnp.concatenate([v_b, s.astype(BF16)], axis=0), preferred_element_type=F32)
            state_ref[0, h] = g_chunk[h] * s + u
            return o

        def in_pieces(blk):
            return _in_pieces(x_ref[0, blk, :], norm_pre_ref, w_in_b_ref, z_ref.at[blk, :])

        def out_pieces(blk):
            return _out_pieces(x_ref[0, blk, :], p_ref[0, blk, :], mixin_ref.at[blk, :], w_out_b_ref,
                               norm_post_ref, w_gate_b_ref, w_ple_b_ref, y_ref.at[0, blk, :])

        def stages(blk):
            out = []
            for r in range(blk.start, blk.stop, CHUNK):
                rows = slice(r, r + CHUNK)
                out += _chunk_stages(z_first_ref if blk == first else z_ref, rows,
                                     lambda rows=rows: cos_ref[rows, :],
                                     lambda rows=rows: sin_ref[rows, :], ln_g_ref, gn_ref, bias_ref, wpm_ref,
                                     tabs_ref, state_update, state_output, mixin_ref, None,
                                     [range(0, RET_HEADS // 2), range(RET_HEADS // 2, RET_HEADS)])
            return out

        rows = BLOCK_ROWS["prompt"]
        _emit_blocks(in_pieces, out_pieces, stages, [slice(r, r + rows) for r in range(0, PROMPT_TILE, rows)],
                     next_in=next_first_in())

    @pl.when(step == pl.num_programs(0) - 1)
    def _():
        for copy in w_copies:
            copy.wait()


def _sample_kernel(g_seq, x_ref, p_ref, cos_ref, sin_ref, st_in_ref, w_in_ref, w_out_ref, w_gate_ref,
                   w_ple_ref, norm_pre_ref, norm_post_ref, ln_g_ref, gn_ref, wpm_ref, bias_ref, tabs_ref,
                   y_ref, st_out_ref, vrows_ref, z_ref, mixin_ref):
    seq_len = cos_ref.shape[0]
    seqs = CHUNK // seq_len

    pairs = seqs // 2
    per_pair = lambda a: a.reshape(pairs, 2 * seq_len, a.shape[-1])
    first_of_pair = lax.broadcasted_iota(jnp.int32, (1, 2 * seq_len, 1), 1) < seq_len
    tile_rows = lambda ref: jnp.concatenate([ref[...]] * seqs, axis=0)

    def stages(blk):
        pq = slice(blk.start // (2 * seq_len), blk.stop // (2 * seq_len))

        def state_update(h, k_b, vw_b):
            vw = per_pair(vw_b)
            zero = jnp.zeros_like(vw)
            both = jnp.concatenate([jnp.where(first_of_pair, vw, zero), jnp.where(first_of_pair, zero, vw)],
                                   axis=-1)
            u = jnp.einsum('pjd,pje->pde', per_pair(k_b), both, preferred_element_type=F32)
            st_out_ref[pq, 0, h] = g_seq[h] * st_in_ref[pq, 0, h] + u[:, :, :RET_DV]
            st_out_ref[pq, 1, h] = g_seq[h] * st_in_ref[pq, 1, h] + u[:, :, RET_DV:]
            return None

        def state_output(h, scores_b, qw_b, v_b, _):
            o = jnp.dot(scores_b, v_b, preferred_element_type=F32)
            s_pair = jnp.concatenate([st_in_ref[pq, 0, h], st_in_ref[pq, 1, h]], axis=-1).astype(BF16)
            both = jnp.einsum('pid,pde->pie', per_pair(qw_b), s_pair, preferred_element_type=F32)
            o_inter = jnp.where(first_of_pair, both[:, :, :RET_DV], both[:, :, RET_DV:])
            return o + o_inter.reshape(CHUNK, RET_DV)

        return _chunk_stages(z_ref, blk, lambda: tile_rows(cos_ref), lambda: tile_rows(sin_ref), ln_g_ref,
                             gn_ref, bias_ref, wpm_ref, tabs_ref, state_update, state_output, mixin_ref,
                             vrows_ref, [range(RET_HEADS)])

    def in_pieces(blk):
        return _in_pieces(x_ref[blk, :], norm_pre_ref, w_in_ref, z_ref.at[blk, :])

    def out_pieces(blk):
        return _out_pieces(x_ref[blk, :], p_ref[blk, :], mixin_ref.at[blk, :], w_out_ref, norm_post_ref,
                           w_gate_ref, w_ple_ref, y_ref.at[blk, :])

    rows = BLOCK_ROWS["sample"]
    _emit_blocks(in_pieces, out_pieces, stages, [slice(r, r + rows) for r in range(0, SAMPLE_TILE, rows)],
                 first_in=in_pieces(slice(0, rows)))


def _const_spec(shape):
    return pl.BlockSpec(shape, lambda *_: (0,) * len(shape), pipeline_mode=pl.Buffered(1))


def kernel(x_prompt, x_sample, state_ret, p_prompt, p_sample, w_in, w_out, norm_pre, norm_post, sgu_w, sgu_b,
           sgu_ln, ret_gn, w_ple_proj, w_ple_gate):
    batch, seq, _ = x_prompt.shape
    dec_batch, dec_seq, _ = x_sample.shape
    n_tiles = seq // PROMPT_TILE
    n_tok = dec_batch * dec_seq
    assert seq % PROMPT_TILE == 0 and PROMPT_TILE % BLOCK_ROWS["prompt"] == 0 and w_in.shape[0] == 1
    assert CHUNK % dec_seq == 0 and n_tok % SAMPLE_TILE == 0 and SAMPLE_TILE % BLOCK_ROWS["sample"] == 0
    assert D_MODEL % CAST_STEPS == 0 and PLE_DIM % (16 * CAST_STEPS) == 0
    assert (CHUNK // dec_seq) % 2 == 0 and dec_batch % (SAMPLE_TILE // dec_seq) == 0

    small = (norm_pre[0][None, :], norm_post[0][None, :], sgu_ln[0][None, :], ret_gn[0][None, :],
             sgu_w[0], sgu_b[0])
    small_specs = [_const_spec((1, D_MODEL)), _const_spec((1, D_MODEL)), _const_spec((1, SGU_WIDTH)),
                   _const_spec((1, RET_HEADS * RET_DV)), _const_spec((SGU_GROUPS, CHUNK, CHUNK)),
                   _const_spec((SGU_GROUPS, CHUNK)), _const_spec((CHUNK, 2 * CHUNK)),
                   _const_spec((3, RET_HEADS, CHUNK, LANES))]
    table_shapes = [((SGU_GROUPS // 2, CHUNK, 2 * CHUNK), BF16), ((CHUNK, SGU_WIDTH), F32)]
    weight_shapes = [(D_MODEL, IN_WIDTH), (D_MODEL, D_MODEL), (D_MODEL, D_MODEL), (PLE_DIM, D_MODEL)]

    tabs_p, g_chunk = _retention_tables(CHUNK)
    cos_p, sin_p = _rotary_tables(np.arange(seq))
    tile_of = lambda s: jnp.maximum(s - CAST_STEPS, 0)
    tile_spec = lambda width: pl.BlockSpec((1, PROMPT_TILE, width),
                                           lambda s: (tile_of(s) // n_tiles, tile_of(s) % n_tiles, 0))
    pos_spec = pl.BlockSpec((PROMPT_TILE, LANES), lambda s: (tile_of(s) % n_tiles, 0))
    blocks_per_tile = PROMPT_TILE // BLOCK_ROWS["prompt"]
    next_tile = lambda s: jnp.minimum(jnp.maximum(s - CAST_STEPS + 1, 0), batch * n_tiles - 1)
    next_spec = pl.BlockSpec((1, BLOCK_ROWS["prompt"], D_MODEL),
                             lambda s: (next_tile(s) // n_tiles, (next_tile(s) % n_tiles) * blocks_per_tile, 0))
    cast_spec = lambda shape: pl.BlockSpec((shape[0] // CAST_STEPS, shape[1]),
                                           lambda s: (jnp.minimum(s, CAST_STEPS - 1), 0))
    resident = lambda shape: pl.BlockSpec(shape, lambda s: (0,) * len(shape))
    y_prompt, st_prompt, w_in_b, w_out_b, w_gate_b, w_ple_b, wpm_dec, bias_dec = pl.pallas_call(
        functools.partial(_prompt_kernel, g_chunk, n_tiles, dec_seq),
        grid=(CAST_STEPS + batch * n_tiles,),
        in_specs=[tile_spec(D_MODEL), next_spec, tile_spec(PLE_DIM), pos_spec, pos_spec]
        + [cast_spec(shape) for shape in weight_shapes] + small_specs + [_const_spec((CHUNK, 2 * CHUNK))],
        out_specs=[tile_spec(D_MODEL),
                   pl.BlockSpec((1, RET_HEADS, RET_DK, RET_DV), lambda s: (tile_of(s) // n_tiles, 0, 0, 0))]
        + [pl.BlockSpec(memory_space=pl.ANY)] * len(weight_shapes) + [resident(shape) for shape, _ in table_shapes],
        out_shape=[jax.ShapeDtypeStruct((batch, seq, D_MODEL), F32),
                   jax.ShapeDtypeStruct((batch, RET_HEADS, RET_DK, RET_DV), F32)]
        + [jax.ShapeDtypeStruct(shape, BF16) for shape in weight_shapes]
        + [jax.ShapeDtypeStruct(shape, dtype) for shape, dtype in table_shapes],
        scratch_shapes=[pltpu.VMEM((PROMPT_TILE, IN_WIDTH), F32), pltpu.VMEM((BLOCK_ROWS["prompt"], IN_WIDTH), F32),
                        pltpu.VMEM((PROMPT_TILE, D_MODEL), BF16)]
        + [pltpu.VMEM(shape, dtype) for shape, dtype in table_shapes]
        + [pltpu.VMEM(shape, BF16) for shape in weight_shapes] + [pltpu.SemaphoreType.DMA((len(weight_shapes),))],
        compiler_params=pltpu.CompilerParams(dimension_semantics=("arbitrary",),
                                             vmem_limit_bytes=VMEM_LIMIT_BYTES),
        name="prompt_layer",
    )(x_prompt, x_prompt, p_prompt[0], cos_p, sin_p, w_in[0], w_out[0], w_ple_gate[0], w_ple_proj[0], *small,
      _mix_mask(CHUNK), tabs_p, _mix_mask(dec_seq))

    tabs_s, g_seq = _retention_tables(dec_seq)
    cos_s, sin_s = _rotary_tables(PAST_LEN + np.arange(dec_seq))
    tile_seqs = SAMPLE_TILE // dec_seq
    tok_spec = lambda width: pl.BlockSpec((SAMPLE_TILE, width), lambda i: (i, 0))
    state_spec = pl.BlockSpec((tile_seqs // 2, 2, RET_HEADS, RET_DK, RET_DV), lambda i: (i, 0, 0, 0, 0))
    paired = (dec_batch // 2, 2, RET_HEADS, RET_DK, RET_DV)
    y_sample, st_sample, v_sample = pl.pallas_call(
        functools.partial(_sample_kernel, g_seq),
        grid=(n_tok // SAMPLE_TILE,),
        in_specs=[tok_spec(D_MODEL), tok_spec(PLE_DIM), _const_spec((dec_seq, LANES)),
                  _const_spec((dec_seq, LANES)), state_spec]
        + [_const_spec(shape) for shape in weight_shapes] + small_specs[:4]
        + [_const_spec(shape) for shape, _ in table_shapes] + small_specs[-1:],
        out_specs=[tok_spec(D_MODEL), state_spec, tok_spec(SGU_WIDTH)],
        out_shape=[jax.ShapeDtypeStruct((n_tok, D_MODEL), F32),
                   jax.ShapeDtypeStruct(paired, F32), jax.ShapeDtypeStruct((n_tok, SGU_WIDTH), F32)],
        scratch_shapes=[pltpu.VMEM((SAMPLE_TILE, IN_WIDTH), F32), pltpu.VMEM((SAMPLE_TILE, D_MODEL), BF16)],
        compiler_params=pltpu.CompilerParams(dimension_semantics=("arbitrary",),
                                             vmem_limit_bytes=VMEM_LIMIT_BYTES),
        name="sample_layer",
    )(x_sample.reshape(n_tok, D_MODEL), p_sample[0].reshape(n_tok, PLE_DIM), cos_s, sin_s,
      state_ret[0].reshape(paired),
      w_in_b, w_out_b, w_gate_b, w_ple_b, *small[:4], wpm_dec, bias_dec, tabs_s)

    return (y_prompt, y_sample.reshape(dec_batch, dec_seq, D_MODEL), st_prompt[None],
            st_sample.reshape(state_ret.shape),
            v_sample.reshape(1, dec_batch, dec_seq, SGU_WIDTH))
```

```python
import functools

import numpy as np
import jax
import jax.numpy as jnp
from jax import lax
from jax.experimental import pallas as pl
from jax.experimental.pallas import tpu as pltpu

F32 = jnp.float32
BF16 = jnp.bfloat16

D_MODEL = 1024
PAST_LEN = 16384
SGU_WIDTH = 512
SGU_GROUPS = 8
SGU_GROUP_DIM = SGU_WIDTH // SGU_GROUPS
RET_HEADS = 4
RET_DK = 128
RET_DV = 128
CHUNK = 128
ROPE_THETA = 10000.0
PLE_DIM = 256
RMS_EPS = 1e-6
LN_EPS = 1e-5
IN_WIDTH = 3 * SGU_WIDTH + RET_HEADS * (2 * RET_DK + 2 * RET_DV)
O_SU, O_SV, O_SG = 0, SGU_WIDTH, 2 * SGU_WIDTH
O_Q = 3 * SGU_WIDTH
O_K = O_Q + RET_HEADS * RET_DK
O_V = O_K + RET_HEADS * RET_DK
O_RG = O_V + RET_HEADS * RET_DV

LANES = 128
VMEM_LIMIT_BYTES = 60 * 1024 * 1024

PROMPT_TILE = 1024
BLOCK_ROWS = {"prompt": 256, "sample": CHUNK}
PIECE_COLS = 512
CAST_STEPS = 8
SAMPLE_TILE = 2 * CHUNK


def _log_gamma():
    return np.log(1.0 - 2.0 ** (-5.0 - np.arange(RET_HEADS, dtype=np.float64)))


def _retention_tables(seg):
    lg = _log_gamma()[:, None, None]
    r = np.arange(CHUNK)
    i, j = r[:, None] % seg, r[None, :] % seg
    same = (r[:, None] // seg) == (r[None, :] // seg)
    decay = np.where(same & (i >= j), np.exp(lg * np.maximum(i - j, 0)), 0.0)
    wq = np.broadcast_to(np.exp(lg * (i + 1.0)), (RET_HEADS, CHUNK, LANES))
    wkv = np.broadcast_to(np.exp(lg * (seg - 1.0 - i)), (RET_HEADS, CHUNK, LANES))
    scale = RET_DK ** -0.5
    tabs = np.stack([decay * scale, wq, wkv * scale]).astype(np.float32)
    g_seg = [float(v) for v in np.exp(_log_gamma() * seg)]
    return tabs, g_seg


def _rotary_tables(pos):
    half = RET_DK // 2
    inv = ROPE_THETA ** (-np.arange(half, dtype=np.float64) / half)
    ang = pos.astype(np.float64)[:, None] * inv[None, :]
    cos, sin = np.cos(ang), np.sin(ang)
    return (np.concatenate([cos, cos], axis=1).astype(np.float32),
            np.concatenate([-sin, sin], axis=1).astype(np.float32))


def _mix_mask(seg):
    r = np.arange(CHUNK)
    same = (r[:, None] // seg) == (r[None, :] // seg)
    m = same & ((r[None, :] % seg) <= (r[:, None] % seg))
    return np.concatenate([m, m], axis=1).astype(np.float32)


def _gelu(x):
    c = float(np.sqrt(2.0 / np.pi))
    half = 0.5 * x
    return half + half * jnp.tanh(x * (c + (c * 0.044715) * (x * x)))


def _silu(x):
    return x * jax.nn.sigmoid(x)


def _rotate(x, cos, sin_signed):
    return x * cos + pltpu.roll(x, RET_DK // 2, 1) * sin_signed


def _normed_bf16(x, norm_ref):
    ms = jnp.mean(x * x, axis=-1, keepdims=True)
    return (x * lax.rsqrt(ms + RMS_EPS) * norm_ref[...]).astype(BF16)


def _in_pieces(x, norm_pre_ref, w_in_ref, z_ref):
    box = {}

    def piece(c0):
        def run():
            if 'h' not in box:
                box['h'] = _normed_bf16(x, norm_pre_ref)
            z_ref[:, c0:c0 + PIECE_COLS] = jnp.dot(box['h'], w_in_ref[:, c0:c0 + PIECE_COLS],
                                                   preferred_element_type=F32)
        return run
    return [piece(c0) for c0 in range(0, IN_WIDTH, PIECE_COLS)]


def _out_pieces(x, p, mixin_ref, w_out_ref, norm_post_ref, w_gate_ref, w_ple_ref, y_ref):
    box = {}
    halves = [slice(c0, c0 + PIECE_COLS) for c0 in range(0, D_MODEL, PIECE_COLS)]

    def ple():
        box['ple'] = jnp.dot(p.astype(BF16), w_ple_ref[...], preferred_element_type=F32)

    def mix(i):
        def run():
            box['mix%d' % i] = jnp.dot(mixin_ref[...], w_out_ref[:, halves[i]], preferred_element_type=F32)
        return run

    def gate(i):
        def run():
            if 'x1' not in box:
                mixes = [box['mix%d' % j] for j in range(len(halves))]
                ms = sum(jnp.sum(m * m, axis=-1, keepdims=True) for m in mixes) * (1.0 / D_MODEL)
                scale = lax.rsqrt(ms + RMS_EPS)
                box['x1'] = [x[:, halves[j]] + mixes[j] * scale * norm_post_ref[:, halves[j]]
                             for j in range(len(halves))]
                box['x1b'] = jnp.concatenate(box['x1'], axis=1).astype(BF16)
            g = jnp.dot(box['x1b'], w_gate_ref[:, halves[i]], preferred_element_type=F32)
            y_ref[:, halves[i]] = box['x1'][i] + jax.nn.sigmoid(g) * box['ple'][:, halves[i]]
        return run
    return [ple] + [mix(i) for i in range(len(halves))] + [gate(i) for i in range(len(halves))]


def _chunk_stages(z_ref, rows, cos, sin_signed, ln_g_ref, gn_ref, bias_ref, wpair_ref, tabs_ref,
                  state_update, state_output, mixin_ref, vrows_ref, head_groups):
    box = {}
    zs = lambda off, i, width: z_ref[rows, off + i * width:off + (i + 1) * width]

    def gating_in():
        sv = _gelu(z_ref[rows, O_SV:O_SV + SGU_WIDTH])
        mu = jnp.mean(sv, axis=-1, keepdims=True)
        cen = sv - mu
        var = jnp.mean(cen * cen, axis=-1, keepdims=True)
        vn = cen * lax.rsqrt(var + LN_EPS) * ln_g_ref[...]
        if vrows_ref is not None:
            vrows_ref[rows, :] = vn
        low_lanes = lax.broadcasted_iota(jnp.int32, (CHUNK, LANES), 1) < SGU_GROUP_DIM
        for m in range(SGU_GROUPS // 2):
            blk = vn[:, m * LANES:(m + 1) * LANES]
            rhs = jnp.concatenate([jnp.where(low_lanes, blk, 0.0), jnp.where(low_lanes, 0.0, blk)],
                                  axis=0).astype(BF16)
            box['mixed%d' % m] = jnp.dot(wpair_ref[m], rhs, preferred_element_type=F32)

    def retention_in(heads):
        c, s = cos(), sin_signed()
        for h in heads:
            q = _rotate(zs(O_Q, h, RET_DK), c, s)
            k = _rotate(zs(O_K, h, RET_DK), c, s)
            v = zs(O_V, h, RET_DV)
            k_b = k.astype(BF16)
            box['scores%d' % h] = lax.dot_general(q.astype(BF16), k_b, (((1,), (1,)), ((), ())),
                                                  preferred_element_type=F32)
            box['qw%d' % h] = (q * tabs_ref[1, h]).astype(BF16)
            box['v%d' % h] = v.astype(BF16)
            box['u%d' % h] = state_update(h, k_b, (v * tabs_ref[2, h]).astype(BF16))

    def gating_out(pairs):
        for m in pairs:
            cols = slice(m * LANES, (m + 1) * LANES)
            mixed = box['mixed%d' % m] + bias_ref[:, cols]
            su = _gelu(zs(O_SU, m, LANES))
            mixin_ref[rows, cols] = (_silu(zs(O_SG, m, LANES)) * (su * mixed)).astype(BF16)

    def retention_mid(heads):
        for h in heads:
            scores_b = (box['scores%d' % h] * tabs_ref[0, h]).astype(BF16)
            box['o%d' % h] = state_output(h, scores_b, box['qw%d' % h], box['v%d' % h], box['u%d' % h])

    def retention_out(heads):
        for h in heads:
            o = box['o%d' % h]
            mu = jnp.mean(o, axis=-1, keepdims=True)
            cen = o - mu
            var = jnp.mean(cen * cen, axis=-1, keepdims=True)
            on = cen * lax.rsqrt(var + LN_EPS) * gn_ref[:, h * RET_DV:(h + 1) * RET_DV]
            mixin_ref[rows, SGU_WIDTH + h * RET_DV:SGU_WIDTH + (h + 1) * RET_DV] = (
                _silu(zs(O_RG, h, RET_DV)) * on).astype(BF16)

    part = functools.partial
    n_pairs = SGU_GROUPS // 2
    pair_groups = [range(n_pairs)] if len(head_groups) <= 2 else [range(n_pairs // 2), range(n_pairs // 2, n_pairs)]
    return ([gating_in] + [part(retention_in, g) for g in head_groups] + [part(gating_out, p) for p in pair_groups]
            + [part(retention_mid, g) for g in head_groups] + [part(retention_out, g) for g in head_groups])


def _interleave(pieces, stages):
    slots = max(len(stages), 1)
    done = 0
    for i in range(slots):
        upto = -(-(i + 1) * len(pieces) // slots)
        for piece in pieces[done:upto]:
            piece()
        done = upto
        if i < len(stages):
            stages[i]()


def _emit_blocks(in_pieces, out_pieces, stages, blocks, first_in=None, next_in=None, after_out=None):
    after_out = after_out or (lambda blk: None)
    _interleave(first_in or [], [])
    for i, blk in enumerate(blocks):
        nxt = in_pieces(blocks[i + 1]) if i + 1 < len(blocks) else (next_in or [])
        if i > 0:
            ple, mix0, mix1, gate0, gate1 = out_pieces(blocks[i - 1])
            pieces = [ple, mix0, mix1] + nxt[:3] + [gate0] + nxt[3:4] + [gate1] + nxt[4:]
        else:
            pieces = nxt
        _interleave(pieces, stages(blk))
        if i > 0:
            after_out(blocks[i - 1])
    _interleave(out_pieces(blocks[-1]), [])
    after_out(blocks[-1])


def _init_tables(seg, sgu_w_ref, sgu_b_ref, mask_ref, wpm_ref, bias_ref):
    rows = lax.broadcasted_iota(jnp.int32, (CHUNK, LANES), 0)
    lanes = lax.broadcasted_iota(jnp.int32, (CHUNK, LANES), 1)
    if seg == CHUNK:
        group = lambda g: sgu_w_ref[g]
        bias_rows = sgu_b_ref[...]
    else:
        reps = CHUNK // seg
        select = jnp.where((rows < seg) & (lanes % seg == rows), 1.0, 0.0).astype(BF16)
        first = lambda a: jnp.where(lanes[:seg] < seg, a, 0.0)

        def group(g):
            stacked = jnp.concatenate([first(sgu_w_ref[g, 0:seg, :])] * reps, axis=0)
            return jnp.dot(stacked.astype(BF16), select, preferred_element_type=F32)
        b_first = jnp.where(lax.broadcasted_iota(jnp.int32, (SGU_GROUPS, LANES), 1) < seg, sgu_b_ref[...], 0.0)
        bias_rows = sum(pltpu.roll(b_first, r * seg, 1) for r in range(reps))
    for m in range(SGU_GROUPS // 2):
        pair = jnp.concatenate([group(2 * m), group(2 * m + 1)], axis=1)
        wpm_ref[m] = jnp.where(mask_ref[...] > 0.0, pair, 0.0).astype(BF16)
    group_of_lane = lax.broadcasted_iota(jnp.int32, (SGU_GROUPS, SGU_WIDTH), 1) // SGU_GROUP_DIM
    expand = jnp.where(group_of_lane == lax.broadcasted_iota(jnp.int32, (SGU_GROUPS, SGU_WIDTH), 0),
                       1.0, 0.0).astype(BF16)
    total, rest = jnp.zeros((CHUNK, SGU_WIDTH), F32), bias_rows
    for _ in range(3):
        term = rest.astype(BF16)
        total = total + lax.dot_general(term, expand, (((0,), (0,)), ((), ())), preferred_element_type=F32)
        rest = rest - term.astype(F32)
    bias_ref[...] = total


def _prompt_kernel(g_chunk, n_tiles, dec_seq, x_ref, x_next_ref, p_ref, cos_ref, sin_ref, w_in_ref, w_out_ref,
                   w_gate_ref, w_ple_ref, norm_pre_ref, norm_post_ref, ln_g_ref, gn_ref, sgu_w_ref, sgu_b_ref,
                   mask_ref, tabs_ref, mask_dec_ref, y_hbm, state_ref, w_in_hbm, w_out_hbm, w_gate_hbm,
                   w_ple_hbm, wpm_dec_ref, bias_dec_ref, z_ref, z_first_ref, mixin_ref, wpm_ref, bias_ref,
                   w_in_b_ref, w_out_b_ref, w_gate_b_ref, w_ple_b_ref, w_sem, y_scr, y_sem):
    step = pl.program_id(0)
    last = pl.num_programs(0) - 1
    rows = BLOCK_ROWS["prompt"]
    blocks = [slice(r, r + rows) for r in range(0, PROMPT_TILE, rows)]
    first = blocks[0]

    def y_copy(s, blk):
        t = jnp.maximum(s - CAST_STEPS, 0)
        dst = y_hbm.at[t // n_tiles, pl.ds((t % n_tiles) * PROMPT_TILE + blk.start, rows), :]
        return pltpu.make_async_copy(y_scr.at[t % 2, blk, :], dst, y_sem.at[t % 2, blk.start // rows])

    w_copies = [pltpu.make_async_copy(src, dst, w_sem.at[i]) for i, (src, dst) in enumerate(
        ((w_in_b_ref, w_in_hbm), (w_out_b_ref, w_out_hbm), (w_gate_b_ref, w_gate_hbm), (w_ple_b_ref, w_ple_hbm)))]

    def next_first_in():
        return _in_pieces(x_next_ref[0], norm_pre_ref, w_in_b_ref, z_first_ref)

    @pl.when(step == 0)
    def _():
        _init_tables(CHUNK, sgu_w_ref, sgu_b_ref, mask_ref, wpm_ref, bias_ref)

    @pl.when(step == 1)
    def _():
        _init_tables(dec_seq, sgu_w_ref, sgu_b_ref, mask_dec_ref, wpm_dec_ref, bias_dec_ref)

    @pl.when(step < CAST_STEPS)
    def _():
        for src, dst in ((w_in_ref, w_in_b_ref), (w_out_ref, w_out_b_ref), (w_gate_ref, w_gate_b_ref),
                         (w_ple_ref, w_ple_b_ref)):
            n = src.shape[0]
            dst[pl.ds(pl.multiple_of(step * n, n), n), :] = src[...].astype(BF16)

    @pl.when(step == CAST_STEPS - 1)
    def _():
        _interleave(next_first_in(), [])

    @pl.when(step == CAST_STEPS)
    def _():
        for copy in w_copies:
            copy.start()

    @pl.when(step >= CAST_STEPS)
    def _():
        @pl.when((step - CAST_STEPS) % n_tiles == 0)
        def _():
            state_ref[...] = jnp.zeros_like(state_ref)

        def state_update(h, k_b, vw_b):
            return lax.dot_general(k_b, vw_b, (((0,), (0,)), ((), ())), preferred_element_type=F32)

        def state_output(h, scores_b, qw_b, v_b, u):
            s = state_ref[0, h]
            o = jnp.dot(jnp.concatenate([scores_b, qw_b], axis=1),
                        jnp.concatenate([v_b, s.astype(BF16)], axis=0), preferred_element_type=F32)
            state_ref[0, h] = g_chunk[h] * s + u
            return o

        def in_pieces(blk):
            return _in_pieces(x_ref[0, blk, :], norm_pre_ref, w_in_b_ref, z_ref.at[blk, :])

        def out_pieces(blk):
            return _out_pieces(x_ref[0, blk, :], p_ref[0, blk, :], mixin_ref.at[blk, :], w_out_b_ref,
                               norm_post_ref, w_gate_b_ref, w_ple_b_ref, y_scr.at[(step - CAST_STEPS) % 2, blk, :])

        def stages(blk):
            out = []
            for r in range(blk.start, blk.stop, CHUNK):
                rows = slice(r, r + CHUNK)
                out += _chunk_stages(z_first_ref if blk == first else z_ref, rows,
                                     lambda rows=rows: cos_ref[rows, :],
                                     lambda rows=rows: sin_ref[rows, :], ln_g_ref, gn_ref, bias_ref, wpm_ref,
                                     tabs_ref, state_update, state_output, mixin_ref, None,
                                     [range(0, RET_HEADS // 2), range(RET_HEADS // 2, RET_HEADS)])
            return out

        _emit_blocks(in_pieces, out_pieces, stages, blocks, next_in=next_first_in(),
                     after_out=lambda blk: y_copy(step, blk).start())

    @pl.when(step > CAST_STEPS)
    def _():
        for blk in blocks:
            y_copy(step - 1, blk).wait()

    @pl.when(step == last)
    def _():
        for blk in blocks:
            y_copy(step, blk).wait()
        for copy in w_copies:
            copy.wait()


def _sample_kernel(g_seq, x_ref, p_ref, cos_ref, sin_ref, st_in_ref, w_in_ref, w_out_ref, w_gate_ref,
                   w_ple_ref, norm_pre_ref, norm_post_ref, ln_g_ref, gn_ref, wpm_ref, bias_ref, tabs_ref,
                   y_ref, st_out_ref, vrows_ref, z_ref, mixin_ref):
    seq_len = cos_ref.shape[0]
    seqs = CHUNK // seq_len

    pairs = seqs // 2
    per_pair = lambda a: a.reshape(pairs, 2 * seq_len, a.shape[-1])
    first_of_pair = lax.broadcasted_iota(jnp.int32, (1, 2 * seq_len, 1), 1) < seq_len
    tile_rows = lambda ref: jnp.concatenate([ref[...]] * seqs, axis=0)

    def stages(blk):
        pq = slice(blk.start // (2 * seq_len), blk.stop // (2 * seq_len))

        def state_update(h, k_b, vw_b):
            vw = per_pair(vw_b)
            zero = jnp.zeros_like(vw)
            both = jnp.concatenate([jnp.where(first_of_pair, vw, zero), jnp.where(first_of_pair, zero, vw)],
                                   axis=-1)
            u = jnp.einsum('pjd,pje->pde', per_pair(k_b), both, preferred_element_type=F32)
            st_out_ref[pq, 0, h] = g_seq[h] * st_in_ref[pq, 0, h] + u[:, :, :RET_DV]
            st_out_ref[pq, 1, h] = g_seq[h] * st_in_ref[pq, 1, h] + u[:, :, RET_DV:]
            return None

        def state_output(h, scores_b, qw_b, v_b, _):
            o = jnp.dot(scores_b, v_b, preferred_element_type=F32)
            s_pair = jnp.concatenate([st_in_ref[pq, 0, h], st_in_ref[pq, 1, h]], axis=-1).astype(BF16)
            both = jnp.einsum('pid,pde->pie', per_pair(qw_b), s_pair, preferred_element_type=F32)
            o_inter = jnp.where(first_of_pair, both[:, :, :RET_DV], both[:, :, RET_DV:])
            return o + o_inter.reshape(CHUNK, RET_DV)

        return _chunk_stages(z_ref, blk, lambda: tile_rows(cos_ref), lambda: tile_rows(sin_ref), ln_g_ref,
                             gn_ref, bias_ref, wpm_ref, tabs_ref, state_update, state_output, mixin_ref,
                             vrows_ref, [range(RET_HEADS)])

    def in_pieces(blk):
        return _in_pieces(x_ref[blk, :], norm_pre_ref, w_in_ref, z_ref.at[blk, :])

    def out_pieces(blk):
        return _out_pieces(x_ref[blk, :], p_ref[blk, :], mixin_ref.at[blk, :], w_out_ref, norm_post_ref,
                           w_gate_ref, w_ple_ref, y_ref.at[blk, :])

    rows = BLOCK_ROWS["sample"]
    _emit_blocks(in_pieces, out_pieces, stages, [slice(r, r + rows) for r in range(0, SAMPLE_TILE, rows)],
                 first_in=in_pieces(slice(0, rows)))


def _const_spec(shape):
    return pl.BlockSpec(shape, lambda *_: (0,) * len(shape), pipeline_mode=pl.Buffered(1))


def kernel(x_prompt, x_sample, state_ret, p_prompt, p_sample, w_in, w_out, norm_pre, norm_post, sgu_w, sgu_b,
           sgu_ln, ret_gn, w_ple_proj, w_ple_gate):
    batch, seq, _ = x_prompt.shape
    dec_batch, dec_seq, _ = x_sample.shape
    n_tiles = seq // PROMPT_TILE
    n_tok = dec_batch * dec_seq
    assert seq % PROMPT_TILE == 0 and PROMPT_TILE % BLOCK_ROWS["prompt"] == 0 and w_in.shape[0] == 1
    assert CHUNK % dec_seq == 0 and n_tok % SAMPLE_TILE == 0 and SAMPLE_TILE % BLOCK_ROWS["sample"] == 0
    assert D_MODEL % CAST_STEPS == 0 and PLE_DIM % (16 * CAST_STEPS) == 0
    assert (CHUNK // dec_seq) % 2 == 0 and dec_batch % (SAMPLE_TILE // dec_seq) == 0

    small = (norm_pre[0][None, :], norm_post[0][None, :], sgu_ln[0][None, :], ret_gn[0][None, :],
             sgu_w[0], sgu_b[0])
    small_specs = [_const_spec((1, D_MODEL)), _const_spec((1, D_MODEL)), _const_spec((1, SGU_WIDTH)),
                   _const_spec((1, RET_HEADS * RET_DV)), _const_spec((SGU_GROUPS, CHUNK, CHUNK)),
                   _const_spec((SGU_GROUPS, CHUNK)), _const_spec((CHUNK, 2 * CHUNK)),
                   _const_spec((3, RET_HEADS, CHUNK, LANES))]
    table_shapes = [((SGU_GROUPS // 2, CHUNK, 2 * CHUNK), BF16), ((CHUNK, SGU_WIDTH), F32)]
    weight_shapes = [(D_MODEL, IN_WIDTH), (D_MODEL, D_MODEL), (D_MODEL, D_MODEL), (PLE_DIM, D_MODEL)]

    tabs_p, g_chunk = _retention_tables(CHUNK)
    cos_p, sin_p = _rotary_tables(np.arange(seq))
    tile_of = lambda s: jnp.maximum(s - CAST_STEPS, 0)
    tile_spec = lambda width: pl.BlockSpec((1, PROMPT_TILE, width),
                                           lambda s: (tile_of(s) // n_tiles, tile_of(s) % n_tiles, 0))
    pos_spec = pl.BlockSpec((PROMPT_TILE, LANES), lambda s: (tile_of(s) % n_tiles, 0))
    blocks_per_tile = PROMPT_TILE // BLOCK_ROWS["prompt"]
    next_tile = lambda s: jnp.minimum(jnp.maximum(s - CAST_STEPS + 1, 0), batch * n_tiles - 1)
    next_spec = pl.BlockSpec((1, BLOCK_ROWS["prompt"], D_MODEL),
                             lambda s: (next_tile(s) // n_tiles, (next_tile(s) % n_tiles) * blocks_per_tile, 0))
    cast_spec = lambda shape: pl.BlockSpec((shape[0] // CAST_STEPS, shape[1]),
                                           lambda s: (jnp.minimum(s, CAST_STEPS - 1), 0))
    resident = lambda shape: pl.BlockSpec(shape, lambda s: (0,) * len(shape))
    y_prompt, st_prompt, w_in_b, w_out_b, w_gate_b, w_ple_b, wpm_dec, bias_dec = pl.pallas_call(
        functools.partial(_prompt_kernel, g_chunk, n_tiles, dec_seq),
        grid=(CAST_STEPS + batch * n_tiles,),
        in_specs=[tile_spec(D_MODEL), next_spec, tile_spec(PLE_DIM), pos_spec, pos_spec]
        + [cast_spec(shape) for shape in weight_shapes] + small_specs + [_const_spec((CHUNK, 2 * CHUNK))],
        out_specs=[pl.BlockSpec(memory_space=pl.ANY),
                   pl.BlockSpec((1, RET_HEADS, RET_DK, RET_DV), lambda s: (tile_of(s) // n_tiles, 0, 0, 0))]
        + [pl.BlockSpec(memory_space=pl.ANY)] * len(weight_shapes) + [resident(shape) for shape, _ in table_shapes],
        out_shape=[jax.ShapeDtypeStruct((batch, seq, D_MODEL), F32),
                   jax.ShapeDtypeStruct((batch, RET_HEADS, RET_DK, RET_DV), F32)]
        + [jax.ShapeDtypeStruct(shape, BF16) for shape in weight_shapes]
        + [jax.ShapeDtypeStruct(shape, dtype) for shape, dtype in table_shapes],
        scratch_shapes=[pltpu.VMEM((PROMPT_TILE, IN_WIDTH), F32), pltpu.VMEM((BLOCK_ROWS["prompt"], IN_WIDTH), F32),
                        pltpu.VMEM((PROMPT_TILE, D_MODEL), BF16)]
        + [pltpu.VMEM(shape, dtype) for shape, dtype in table_shapes]
        + [pltpu.VMEM(shape, BF16) for shape in weight_shapes] + [pltpu.SemaphoreType.DMA((len(weight_shapes),))]
        + [pltpu.VMEM((2, PROMPT_TILE, D_MODEL), F32), pltpu.SemaphoreType.DMA((2, blocks_per_tile))],
        compiler_params=pltpu.CompilerParams(dimension_semantics=("arbitrary",),
                                             vmem_limit_bytes=VMEM_LIMIT_BYTES),
        name="prompt_layer",
    )(x_prompt, x_prompt, p_prompt[0], cos_p, sin_p, w_in[0], w_out[0], w_ple_gate[0], w_ple_proj[0], *small,
      _mix_mask(CHUNK), tabs_p, _mix_mask(dec_seq))

    tabs_s, g_seq = _retention_tables(dec_seq)
    cos_s, sin_s = _rotary_tables(PAST_LEN + np.arange(dec_seq))
    tile_seqs = SAMPLE_TILE // dec_seq
    tok_spec = lambda width: pl.BlockSpec((SAMPLE_TILE, width), lambda i: (i, 0))
    state_spec = pl.BlockSpec((tile_seqs // 2, 2, RET_HEADS, RET_DK, RET_DV), lambda i: (i, 0, 0, 0, 0))
    paired = (dec_batch // 2, 2, RET_HEADS, RET_DK, RET_DV)
    y_sample, st_sample, v_sample = pl.pallas_call(
        functools.partial(_sample_kernel, g_seq),
        grid=(n_tok // SAMPLE_TILE,),
        in_specs=[tok_spec(D_MODEL), tok_spec(PLE_DIM), _const_spec((dec_seq, LANES)),
                  _const_spec((dec_seq, LANES)), state_spec]
        + [_const_spec(shape) for shape in weight_shapes] + small_specs[:4]
        + [_const_spec(shape) for shape, _ in table_shapes] + small_specs[-1:],
        out_specs=[tok_spec(D_MODEL), state_spec, tok_spec(SGU_WIDTH)],
        out_shape=[jax.ShapeDtypeStruct((n_tok, D_MODEL), F32),
                   jax.ShapeDtypeStruct(paired, F32), jax.ShapeDtypeStruct((n_tok, SGU_WIDTH), F32)],
        scratch_shapes=[pltpu.VMEM((SAMPLE_TILE, IN_WIDTH), F32), pltpu.VMEM((SAMPLE_TILE, D_MODEL), BF16)],
        compiler_params=pltpu.CompilerParams(dimension_semantics=("arbitrary",),
                                             vmem_limit_bytes=VMEM_LIMIT_BYTES),
        name="sample_layer",
    )(x_sample.reshape(n_tok, D_MODEL), p_sample[0].reshape(n_tok, PLE_DIM), cos_s, sin_s,
      state_ret[0].reshape(paired),
      w_in_b, w_out_b, w_gate_b, w_ple_b, *small[:4], wpm_dec, bias_dec, tabs_s)

    return (y_prompt, y_sample.reshape(dec_batch, dec_seq, D_MODEL), st_prompt[None],
            st_sample.reshape(state_ret.shape),
            v_sample.reshape(1, dec_batch, dec_seq, SGU_WIDTH))
```

```python
import functools

import numpy as np
import jax
import jax.numpy as jnp
from jax import lax
from jax.experimental import pallas as pl
from jax.experimental.pallas import tpu as pltpu

F32 = jnp.float32
BF16 = jnp.bfloat16

D_MODEL = 1024
PAST_LEN = 16384
SGU_WIDTH = 512
SGU_GROUPS = 8
SGU_GROUP_DIM = SGU_WIDTH // SGU_GROUPS
RET_HEADS = 4
RET_DK = 128
RET_DV = 128
CHUNK = 128
ROPE_THETA = 10000.0
PLE_DIM = 256
RMS_EPS = 1e-6
LN_EPS = 1e-5
IN_WIDTH = 3 * SGU_WIDTH + RET_HEADS * (2 * RET_DK + 2 * RET_DV)
O_SU, O_SV, O_SG = 0, SGU_WIDTH, 2 * SGU_WIDTH
O_Q = 3 * SGU_WIDTH
O_K = O_Q + RET_HEADS * RET_DK
O_V = O_K + RET_HEADS * RET_DK
O_RG = O_V + RET_HEADS * RET_DV

LANES = 128
VMEM_LIMIT_BYTES = 60 * 1024 * 1024

PROMPT_TILE = 1024
BLOCK_ROWS = {"prompt": 256, "sample": CHUNK}
PIECE_COLS = 512
CAST_STEPS = 8
SAMPLE_TILE = 2 * CHUNK


def _log_gamma():
    return np.log(1.0 - 2.0 ** (-5.0 - np.arange(RET_HEADS, dtype=np.float64)))


def _retention_tables(seg):
    lg = _log_gamma()[:, None, None]
    r = np.arange(CHUNK)
    i, j = r[:, None] % seg, r[None, :] % seg
    same = (r[:, None] // seg) == (r[None, :] // seg)
    decay = np.where(same & (i >= j), np.exp(lg * np.maximum(i - j, 0)), 0.0)
    wq = np.broadcast_to(np.exp(lg * (i + 1.0)), (RET_HEADS, CHUNK, LANES))
    wkv = np.broadcast_to(np.exp(lg * (seg - 1.0 - i)), (RET_HEADS, CHUNK, LANES))
    scale = RET_DK ** -0.5
    tabs = np.stack([decay * scale, wq, wkv * scale]).astype(np.float32)
    g_seg = [float(v) for v in np.exp(_log_gamma() * seg)]
    return tabs, g_seg


def _rotary_tables(pos):
    half = RET_DK // 2
    inv = ROPE_THETA ** (-np.arange(half, dtype=np.float64) / half)
    ang = pos.astype(np.float64)[:, None] * inv[None, :]
    cos, sin = np.cos(ang), np.sin(ang)
    return (np.concatenate([cos, cos], axis=1).astype(np.float32),
            np.concatenate([-sin, sin], axis=1).astype(np.float32))


def _mix_mask(seg):
    r = np.arange(CHUNK)
    same = (r[:, None] // seg) == (r[None, :] // seg)
    m = same & ((r[None, :] % seg) <= (r[:, None] % seg))
    return np.concatenate([m, m], axis=1).astype(np.float32)


def _gelu(x):
    c = float(np.sqrt(2.0 / np.pi))
    half = 0.5 * x
    return half + half * jnp.tanh(x * (c + (c * 0.044715) * (x * x)))


def _silu(x):
    return x * jax.nn.sigmoid(x)


def _rotate(x, cos, sin_signed):
    return x * cos + pltpu.roll(x, RET_DK // 2, 1) * sin_signed


def _normed_bf16(x, norm_ref):
    ms = jnp.mean(x * x, axis=-1, keepdims=True)
    return (x * lax.rsqrt(ms + RMS_EPS) * norm_ref[...]).astype(BF16)


def _in_pieces(x, norm_pre_ref, w_in_ref, z_ref):
    box = {}

    def piece(c0):
        def run():
            if 'h' not in box:
                box['h'] = _normed_bf16(x, norm_pre_ref)
            z_ref[:, c0:c0 + PIECE_COLS] = jnp.dot(box['h'], w_in_ref[:, c0:c0 + PIECE_COLS],
                                                   preferred_element_type=F32)
        return run
    return [piece(c0) for c0 in range(0, IN_WIDTH, PIECE_COLS)]


def _out_pieces(x, p, mixin_ref, w_out_ref, norm_post_ref, w_gate_ref, w_ple_ref, y_ref):
    box = {}
    halves = [slice(c0, c0 + PIECE_COLS) for c0 in range(0, D_MODEL, PIECE_COLS)]

    def ple():
        box['ple'] = jnp.dot(p.astype(BF16), w_ple_ref[...], preferred_element_type=F32)

    def mix(i):
        def run():
            box['mix%d' % i] = jnp.dot(mixin_ref[...], w_out_ref[:, halves[i]], preferred_element_type=F32)
        return run

    def gate(i):
        def run():
            if 'x1' not in box:
                mixes = [box['mix%d' % j] for j in range(len(halves))]
                ms = sum(jnp.sum(m * m, axis=-1, keepdims=True) for m in mixes) * (1.0 / D_MODEL)
                scale = lax.rsqrt(ms + RMS_EPS)
                box['x1'] = [x[:, halves[j]] + mixes[j] * scale * norm_post_ref[:, halves[j]]
                             for j in range(len(halves))]
                box['x1b'] = jnp.concatenate(box['x1'], axis=1).astype(BF16)
            g = jnp.dot(box['x1b'], w_gate_ref[:, halves[i]], preferred_element_type=F32)
            y_ref[:, halves[i]] = box['x1'][i] + jax.nn.sigmoid(g) * box['ple'][:, halves[i]]
        return run
    return [ple] + [mix(i) for i in range(len(halves))] + [gate(i) for i in range(len(halves))]


def _chunk_stages(z_ref, rows, cos, sin_signed, ln_g_ref, gn_ref, bias_ref, wpair_ref, tabs_ref,
                  state_update, state_output, mixin_ref, vrows_ref, head_groups):
    box = {}
    zs = lambda off, i, width: z_ref[rows, off + i * width:off + (i + 1) * width]

    def gating_in():
        sv = _gelu(z_ref[rows, O_SV:O_SV + SGU_WIDTH])
        mu = jnp.mean(sv, axis=-1, keepdims=True)
        cen = sv - mu
        var = jnp.mean(cen * cen, axis=-1, keepdims=True)
        vn = cen * lax.rsqrt(var + LN_EPS) * ln_g_ref[...]
        if vrows_ref is not None:
            vrows_ref[rows, :] = vn
        low_lanes = lax.broadcasted_iota(jnp.int32, (CHUNK, LANES), 1) < SGU_GROUP_DIM
        for m in range(SGU_GROUPS // 2):
            blk = vn[:, m * LANES:(m + 1) * LANES]
            rhs = jnp.concatenate([jnp.where(low_lanes, blk, 0.0), jnp.where(low_lanes, 0.0, blk)],
                                  axis=0).astype(BF16)
            box['mixed%d' % m] = jnp.dot(wpair_ref[m], rhs, preferred_element_type=F32)

    def retention_in(heads):
        c, s = cos(), sin_signed()
        for h in heads:
            q = _rotate(zs(O_Q, h, RET_DK), c, s)
            k = _rotate(zs(O_K, h, RET_DK), c, s)
            v = zs(O_V, h, RET_DV)
            k_b = k.astype(BF16)
            box['scores%d' % h] = lax.dot_general(q.astype(BF16), k_b, (((1,), (1,)), ((), ())),
                                                  preferred_element_type=F32)
            box['qw%d' % h] = (q * tabs_ref[1, h]).astype(BF16)
            box['v%d' % h] = v.astype(BF16)
            box['u%d' % h] = state_update(h, k_b, (v * tabs_ref[2, h]).astype(BF16))

    def gating_out(pairs):
        for m in pairs:
            cols = slice(m * LANES, (m + 1) * LANES)
            mixed = box['mixed%d' % m] + bias_ref[:, cols]
            su = _gelu(zs(O_SU, m, LANES))
            mixin_ref[rows, cols] = (_silu(zs(O_SG, m, LANES)) * (su * mixed)).astype(BF16)

    def retention_mid(heads):
        for h in heads:
            scores_b = (box['scores%d' % h] * tabs_ref[0, h]).astype(BF16)
            box['o%d' % h] = state_output(h, scores_b, box['qw%d' % h], box['v%d' % h], box['u%d' % h])

    def retention_out(heads):
        for h in heads:
            o = box['o%d' % h]
            mu = jnp.mean(o, axis=-1, keepdims=True)
            cen = o - mu
            var = jnp.mean(cen * cen, axis=-1, keepdims=True)
            on = cen * lax.rsqrt(var + LN_EPS) * gn_ref[:, h * RET_DV:(h + 1) * RET_DV]
            mixin_ref[rows, SGU_WIDTH + h * RET_DV:SGU_WIDTH + (h + 1) * RET_DV] = (
                _silu(zs(O_RG, h, RET_DV)) * on).astype(BF16)

    part = functools.partial
    n_pairs = SGU_GROUPS // 2
    pair_groups = [range(n_pairs)] if len(head_groups) <= 2 else [range(n_pairs // 2), range(n_pairs // 2, n_pairs)]
    return ([gating_in] + [part(retention_in, g) for g in head_groups] + [part(gating_out, p) for p in pair_groups]
            + [part(retention_mid, g) for g in head_groups] + [part(retention_out, g) for g in head_groups])


def _interleave(pieces, stages):
    slots = max(len(stages), 1)
    done = 0
    for i in range(slots):
        upto = -(-(i + 1) * len(pieces) // slots)
        for piece in pieces[done:upto]:
            piece()
        done = upto
        if i < len(stages):
            stages[i]()


def _emit_blocks(in_pieces, out_pieces, stages, blocks, first_in=None, next_in=None, after_out=None,
                 after_stages=None):
    after_out = after_out or (lambda blk: None)
    after_stages = after_stages or (lambda blk: None)
    _interleave(first_in or [], [])
    for i, blk in enumerate(blocks):
        nxt = in_pieces(blocks[i + 1]) if i + 1 < len(blocks) else (next_in or [])
        if i > 0:
            ple, mix0, mix1, gate0, gate1 = out_pieces(blocks[i - 1])
            pieces = [ple, mix0, mix1] + nxt[:3] + [gate0] + nxt[3:4] + [gate1] + nxt[4:]
        else:
            pieces = nxt
        _interleave(pieces, stages(blk))
        after_stages(blk)
        if i > 0:
            after_out(blocks[i - 1])
    _interleave(out_pieces(blocks[-1]), [])
    after_out(blocks[-1])


def _init_tables(seg, sgu_w_ref, sgu_b_ref, mask_ref, wpm_ref, bias_ref):
    rows = lax.broadcasted_iota(jnp.int32, (CHUNK, LANES), 0)
    lanes = lax.broadcasted_iota(jnp.int32, (CHUNK, LANES), 1)
    if seg == CHUNK:
        group = lambda g: sgu_w_ref[g]
        bias_rows = sgu_b_ref[...]
    else:
        reps = CHUNK // seg
        select = jnp.where((rows < seg) & (lanes % seg == rows), 1.0, 0.0).astype(BF16)
        first = lambda a: jnp.where(lanes[:seg] < seg, a, 0.0)

        def group(g):
            stacked = jnp.concatenate([first(sgu_w_ref[g, 0:seg, :])] * reps, axis=0)
            return jnp.dot(stacked.astype(BF16), select, preferred_element_type=F32)
        b_first = jnp.where(lax.broadcasted_iota(jnp.int32, (SGU_GROUPS, LANES), 1) < seg, sgu_b_ref[...], 0.0)
        bias_rows = sum(pltpu.roll(b_first, r * seg, 1) for r in range(reps))
    for m in range(SGU_GROUPS // 2):
        pair = jnp.concatenate([group(2 * m), group(2 * m + 1)], axis=1)
        wpm_ref[m] = jnp.where(mask_ref[...] > 0.0, pair, 0.0).astype(BF16)
    group_of_lane = lax.broadcasted_iota(jnp.int32, (SGU_GROUPS, SGU_WIDTH), 1) // SGU_GROUP_DIM
    expand = jnp.where(group_of_lane == lax.broadcasted_iota(jnp.int32, (SGU_GROUPS, SGU_WIDTH), 0),
                       1.0, 0.0).astype(BF16)
    total, rest = jnp.zeros((CHUNK, SGU_WIDTH), F32), bias_rows
    for _ in range(3):
        term = rest.astype(BF16)
        total = total + lax.dot_general(term, expand, (((0,), (0,)), ((), ())), preferred_element_type=F32)
        rest = rest - term.astype(F32)
    bias_ref[...] = total


def _prompt_kernel(g_chunk, n_tiles, dec_seq, x_ref, x_next_ref, p_ref, cos_ref, sin_ref, w_in_ref, w_out_ref,
                   w_gate_ref, w_ple_ref, norm_pre_ref, norm_post_ref, ln_g_ref, gn_ref, sgu_w_ref, sgu_b_ref,
                   mask_ref, tabs_ref, mask_dec_ref, y_hbm, state_ref, w_in_hbm, w_out_hbm, w_gate_hbm,
                   w_ple_hbm, wpm_dec_ref, bias_dec_ref, z_ref, z_first_ref, mixin_ref, wpm_ref, bias_ref,
                   w_in_b_ref, w_out_b_ref, w_gate_b_ref, w_ple_b_ref, w_sem, y_scr, y_sem):
    step = pl.program_id(0)
    last = pl.num_programs(0) - 1
    rows = BLOCK_ROWS["prompt"]
    blocks = [slice(r, r + rows) for r in range(0, PROMPT_TILE, rows)]
    first = blocks[0]

    def y_copy(s, blk):
        t = jnp.maximum(s - CAST_STEPS, 0)
        dst = y_hbm.at[t // n_tiles, pl.ds((t % n_tiles) * PROMPT_TILE + blk.start, rows), :]
        return pltpu.make_async_copy(y_scr.at[t % 2, blk, :], dst, y_sem.at[t % 2, blk.start // rows])

    w_copies = [pltpu.make_async_copy(src, dst, w_sem.at[i]) for i, (src, dst) in enumerate(
        ((w_in_b_ref, w_in_hbm), (w_out_b_ref, w_out_hbm), (w_gate_b_ref, w_gate_hbm), (w_ple_b_ref, w_ple_hbm)))]

    def next_first_in():
        return _in_pieces(x_next_ref[0], norm_pre_ref, w_in_b_ref, z_first_ref)

    @pl.when(step == 0)
    def _():
        _init_tables(CHUNK, sgu_w_ref, sgu_b_ref, mask_ref, wpm_ref, bias_ref)

    @pl.when(step == 1)
    def _():
        _init_tables(dec_seq, sgu_w_ref, sgu_b_ref, mask_dec_ref, wpm_dec_ref, bias_dec_ref)

    @pl.when(step < CAST_STEPS)
    def _():
        for src, dst in ((w_in_ref, w_in_b_ref), (w_out_ref, w_out_b_ref), (w_gate_ref, w_gate_b_ref),
                         (w_ple_ref, w_ple_b_ref)):
            n = src.shape[0]
            dst[pl.ds(pl.multiple_of(step * n, n), n), :] = src[...].astype(BF16)

    @pl.when(step == CAST_STEPS - 1)
    def _():
        _interleave(next_first_in(), [])

    @pl.when(step == CAST_STEPS)
    def _():
        for copy in w_copies:
            copy.start()

    @pl.when(step >= CAST_STEPS)
    def _():
        @pl.when((step - CAST_STEPS) % n_tiles == 0)
        def _():
            state_ref[...] = jnp.zeros_like(state_ref)

        def state_update(h, k_b, vw_b):
            return lax.dot_general(k_b, vw_b, (((0,), (0,)), ((), ())), preferred_element_type=F32)

        def state_output(h, scores_b, qw_b, v_b, u):
            s = state_ref[0, h]
            o = jnp.dot(jnp.concatenate([scores_b, qw_b], axis=1),
                        jnp.concatenate([v_b, s.astype(BF16)], axis=0), preferred_element_type=F32)
            state_ref[0, h] = g_chunk[h] * s + u
            return o

        def in_pieces(blk):
            return _in_pieces(x_ref[0, blk, :], norm_pre_ref, w_in_b_ref, z_ref.at[blk, :])

        def out_pieces(blk):
            return _out_pieces(x_ref[0, blk, :], p_ref[0, blk, :], mixin_ref.at[blk, :], w_out_b_ref,
                               norm_post_ref, w_gate_b_ref, w_ple_b_ref, y_scr.at[(step - CAST_STEPS) % 2, blk, :])

        def stages(blk):
            out = []
            for r in range(blk.start, blk.stop, CHUNK):
                rows = slice(r, r + CHUNK)
                out += _chunk_stages(z_first_ref if blk == first else z_ref, rows,
                                     lambda rows=rows: cos_ref[rows, :],
                                     lambda rows=rows: sin_ref[rows, :], ln_g_ref, gn_ref, bias_ref, wpm_ref,
                                     tabs_ref, state_update, state_output, mixin_ref, None,
                                     [range(0, RET_HEADS // 2), range(RET_HEADS // 2, RET_HEADS)])
            return out

        _emit_blocks(in_pieces, out_pieces, stages, blocks, next_in=next_first_in(),
                     after_out=lambda blk: y_copy(step, blk).start())

    @pl.when(step > CAST_STEPS)
    def _():
        for blk in blocks:
            y_copy(step - 1, blk).wait()

    @pl.when(step == last)
    def _():
        for blk in blocks:
            y_copy(step, blk).wait()
        for copy in w_copies:
            copy.wait()


def _sample_kernel(g_seq, x_ref, p_ref, cos_ref, sin_ref, st_in_ref, w_in_ref, w_out_ref, w_gate_ref,
                   w_ple_ref, norm_pre_ref, norm_post_ref, ln_g_ref, gn_ref, wpm_ref, bias_ref, tabs_ref,
                   y_ref, st_hbm, vrows_ref, z_ref, mixin_ref, st_scr, st_sem):
    seq_len = cos_ref.shape[0]
    seqs = CHUNK // seq_len
    step = pl.program_id(0)
    rows = BLOCK_ROWS["sample"]
    blocks = [slice(r, r + rows) for r in range(0, SAMPLE_TILE, rows)]
    pairs_of = lambda blk: slice(blk.start // (2 * seq_len), blk.stop // (2 * seq_len))
    st_out_ref = st_scr.at[step % 2]

    def st_copy(s, blk):
        pq = pairs_of(blk)
        dst = st_hbm.at[pl.ds(s * st_scr.shape[1] + pq.start, pq.stop - pq.start)]
        return pltpu.make_async_copy(st_scr.at[s % 2, pq], dst, st_sem.at[s % 2, blk.start // rows])

    pairs = seqs // 2
    per_pair = lambda a: a.reshape(pairs, 2 * seq_len, a.shape[-1])
    first_of_pair = lax.broadcasted_iota(jnp.int32, (1, 2 * seq_len, 1), 1) < seq_len
    tile_rows = lambda ref: jnp.concatenate([ref[...]] * seqs, axis=0)

    def stages(blk):
        pq = pairs_of(blk)

        def state_update(h, k_b, vw_b):
            vw = per_pair(vw_b)
            zero = jnp.zeros_like(vw)
            both = jnp.concatenate([jnp.where(first_of_pair, vw, zero), jnp.where(first_of_pair, zero, vw)],
                                   axis=-1)
            u = jnp.einsum('pjd,pje->pde', per_pair(k_b), both, preferred_element_type=F32)
            st_out_ref[pq, 0, h] = g_seq[h] * st_in_ref[pq, 0, h] + u[:, :, :RET_DV]
            st_out_ref[pq, 1, h] = g_seq[h] * st_in_ref[pq, 1, h] + u[:, :, RET_DV:]
            return None

        def state_output(h, scores_b, qw_b, v_b, _):
            o = jnp.dot(scores_b, v_b, preferred_element_type=F32)
            s_pair = jnp.concatenate([st_in_ref[pq, 0, h], st_in_ref[pq, 1, h]], axis=-1).astype(BF16)
            both = jnp.einsum('pid,pde->pie', per_pair(qw_b), s_pair, preferred_element_type=F32)
            o_inter = jnp.where(first_of_pair, both[:, :, :RET_DV], both[:, :, RET_DV:])
            return o + o_inter.reshape(CHUNK, RET_DV)

        return _chunk_stages(z_ref, blk, lambda: tile_rows(cos_ref), lambda: tile_rows(sin_ref), ln_g_ref,
                             gn_ref, bias_ref, wpm_ref, tabs_ref, state_update, state_output, mixin_ref,
                             vrows_ref, [range(RET_HEADS)])

    def in_pieces(blk):
        return _in_pieces(x_ref[blk, :], norm_pre_ref, w_in_ref, z_ref.at[blk, :])

    def out_pieces(blk):
        return _out_pieces(x_ref[blk, :], p_ref[blk, :], mixin_ref.at[blk, :], w_out_ref, norm_post_ref,
                           w_gate_ref, w_ple_ref, y_ref.at[blk, :])

    _emit_blocks(in_pieces, out_pieces, stages, blocks, first_in=in_pieces(blocks[0]),
                 after_stages=lambda blk: st_copy(step, blk).start())

    @pl.when(step > 0)
    def _():
        for blk in blocks:
            st_copy(step - 1, blk).wait()

    @pl.when(step == pl.num_programs(0) - 1)
    def _():
        for blk in blocks:
            st_copy(step, blk).wait()


def _const_spec(shape):
    return pl.BlockSpec(shape, lambda *_: (0,) * len(shape), pipeline_mode=pl.Buffered(1))


def kernel(x_prompt, x_sample, state_ret, p_prompt, p_sample, w_in, w_out, norm_pre, norm_post, sgu_w, sgu_b,
           sgu_ln, ret_gn, w_ple_proj, w_ple_gate):
    batch, seq, _ = x_prompt.shape
    dec_batch, dec_seq, _ = x_sample.shape
    n_tiles = seq // PROMPT_TILE
    n_tok = dec_batch * dec_seq
    assert seq % PROMPT_TILE == 0 and PROMPT_TILE % BLOCK_ROWS["prompt"] == 0 and w_in.shape[0] == 1
    assert CHUNK % dec_seq == 0 and n_tok % SAMPLE_TILE == 0 and SAMPLE_TILE % BLOCK_ROWS["sample"] == 0
    assert D_MODEL % CAST_STEPS == 0 and PLE_DIM % (16 * CAST_STEPS) == 0
    assert (CHUNK // dec_seq) % 2 == 0 and dec_batch % (SAMPLE_TILE // dec_seq) == 0

    small = (norm_pre[0][None, :], norm_post[0][None, :], sgu_ln[0][None, :], ret_gn[0][None, :],
             sgu_w[0], sgu_b[0])
    small_specs = [_const_spec((1, D_MODEL)), _const_spec((1, D_MODEL)), _const_spec((1, SGU_WIDTH)),
                   _const_spec((1, RET_HEADS * RET_DV)), _const_spec((SGU_GROUPS, CHUNK, CHUNK)),
                   _const_spec((SGU_GROUPS, CHUNK)), _const_spec((CHUNK, 2 * CHUNK)),
                   _const_spec((3, RET_HEADS, CHUNK, LANES))]
    table_shapes = [((SGU_GROUPS // 2, CHUNK, 2 * CHUNK), BF16), ((CHUNK, SGU_WIDTH), F32)]
    weight_shapes = [(D_MODEL, IN_WIDTH), (D_MODEL, D_MODEL), (D_MODEL, D_MODEL), (PLE_DIM, D_MODEL)]

    tabs_p, g_chunk = _retention_tables(CHUNK)
    cos_p, sin_p = _rotary_tables(np.arange(seq))
    tile_of = lambda s: jnp.maximum(s - CAST_STEPS, 0)
    tile_spec = lambda width: pl.BlockSpec((1, PROMPT_TILE, width),
                                           lambda s: (tile_of(s) // n_tiles, tile_of(s) % n_tiles, 0))
    pos_spec = pl.BlockSpec((PROMPT_TILE, LANES), lambda s: (tile_of(s) % n_tiles, 0))
    blocks_per_tile = PROMPT_TILE // BLOCK_ROWS["prompt"]
    next_tile = lambda s: jnp.minimum(jnp.maximum(s - CAST_STEPS + 1, 0), batch * n_tiles - 1)
    next_spec = pl.BlockSpec((1, BLOCK_ROWS["prompt"], D_MODEL),
                             lambda s: (next_tile(s) // n_tiles, (next_tile(s) % n_tiles) * blocks_per_tile, 0))
    cast_spec = lambda shape: pl.BlockSpec((shape[0] // CAST_STEPS, shape[1]),
                                           lambda s: (jnp.minimum(s, CAST_STEPS - 1), 0))
    resident = lambda shape: pl.BlockSpec(shape, lambda s: (0,) * len(shape))
    y_prompt, st_prompt, w_in_b, w_out_b, w_gate_b, w_ple_b, wpm_dec, bias_dec = pl.pallas_call(
        functools.partial(_prompt_kernel, g_chunk, n_tiles, dec_seq),
        grid=(CAST_STEPS + batch * n_tiles,),
        in_specs=[tile_spec(D_MODEL), next_spec, tile_spec(PLE_DIM), pos_spec, pos_spec]
        + [cast_spec(shape) for shape in weight_shapes] + small_specs + [_const_spec((CHUNK, 2 * CHUNK))],
        out_specs=[pl.BlockSpec(memory_space=pl.ANY),
                   pl.BlockSpec((1, RET_HEADS, RET_DK, RET_DV), lambda s: (tile_of(s) // n_tiles, 0, 0, 0))]
        + [pl.BlockSpec(memory_space=pl.ANY)] * len(weight_shapes) + [resident(shape) for shape, _ in table_shapes],
        out_shape=[jax.ShapeDtypeStruct((batch, seq, D_MODEL), F32),
                   jax.ShapeDtypeStruct((batch, RET_HEADS, RET_DK, RET_DV), F32)]
        + [jax.ShapeDtypeStruct(shape, BF16) for shape in weight_shapes]
        + [jax.ShapeDtypeStruct(shape, dtype) for shape, dtype in table_shapes],
        scratch_shapes=[pltpu.VMEM((PROMPT_TILE, IN_WIDTH), F32), pltpu.VMEM((BLOCK_ROWS["prompt"], IN_WIDTH), F32),
                        pltpu.VMEM((PROMPT_TILE, D_MODEL), BF16)]
        + [pltpu.VMEM(shape, dtype) for shape, dtype in table_shapes]
        + [pltpu.VMEM(shape, BF16) for shape in weight_shapes] + [pltpu.SemaphoreType.DMA((len(weight_shapes),))]
        + [pltpu.VMEM((2, PROMPT_TILE, D_MODEL), F32), pltpu.SemaphoreType.DMA((2, blocks_per_tile))],
        compiler_params=pltpu.CompilerParams(dimension_semantics=("arbitrary",),
                                             vmem_limit_bytes=VMEM_LIMIT_BYTES),
        name="prompt_layer",
    )(x_prompt, x_prompt, p_prompt[0], cos_p, sin_p, w_in[0], w_out[0], w_ple_gate[0], w_ple_proj[0], *small,
      _mix_mask(CHUNK), tabs_p, _mix_mask(dec_seq))

    tabs_s, g_seq = _retention_tables(dec_seq)
    cos_s, sin_s = _rotary_tables(PAST_LEN + np.arange(dec_seq))
    tile_seqs = SAMPLE_TILE // dec_seq
    tok_spec = lambda width: pl.BlockSpec((SAMPLE_TILE, width), lambda i: (i, 0))
    state_spec = pl.BlockSpec((tile_seqs // 2, 2, RET_HEADS, RET_DK, RET_DV), lambda i: (i, 0, 0, 0, 0))
    paired = (dec_batch // 2, 2, RET_HEADS, RET_DK, RET_DV)
    y_sample, st_sample, v_sample = pl.pallas_call(
        functools.partial(_sample_kernel, g_seq),
        grid=(n_tok // SAMPLE_TILE,),
        in_specs=[tok_spec(D_MODEL), tok_spec(PLE_DIM), _const_spec((dec_seq, LANES)),
                  _const_spec((dec_seq, LANES)), state_spec]
        + [_const_spec(shape) for shape in weight_shapes] + small_specs[:4]
        + [_const_spec(shape) for shape, _ in table_shapes] + small_specs[-1:],
        out_specs=[tok_spec(D_MODEL), pl.BlockSpec(memory_space=pl.ANY), tok_spec(SGU_WIDTH)],
        out_shape=[jax.ShapeDtypeStruct((n_tok, D_MODEL), F32),
                   jax.ShapeDtypeStruct(paired, F32), jax.ShapeDtypeStruct((n_tok, SGU_WIDTH), F32)],
        scratch_shapes=[pltpu.VMEM((SAMPLE_TILE, IN_WIDTH), F32), pltpu.VMEM((SAMPLE_TILE, D_MODEL), BF16),
                        pltpu.VMEM((2, tile_seqs // 2, 2, RET_HEADS, RET_DK, RET_DV), F32),
                        pltpu.SemaphoreType.DMA((2, SAMPLE_TILE // BLOCK_ROWS["sample"]))],
        compiler_params=pltpu.CompilerParams(dimension_semantics=("arbitrary",),
                                             vmem_limit_bytes=VMEM_LIMIT_BYTES),
        name="sample_layer",
    )(x_sample.reshape(n_tok, D_MODEL), p_sample[0].reshape(n_tok, PLE_DIM), cos_s, sin_s,
      state_ret[0].reshape(paired),
      w_in_b, w_out_b, w_gate_b, w_ple_b, *small[:4], wpm_dec, bias_dec, tabs_s)

    return (y_prompt, y_sample.reshape(dec_batch, dec_seq, D_MODEL), st_prompt[None],
            st_sample.reshape(state_ret.shape),
            v_sample.reshape(1, dec_batch, dec_seq, SGU_WIDTH))
```

```python
import functools

import numpy as np
import jax
import jax.numpy as jnp
from jax import lax
from jax.experimental import pallas as pl
from jax.experimental.pallas import tpu as pltpu

F32 = jnp.float32
BF16 = jnp.bfloat16

D_MODEL = 1024
PAST_LEN = 16384
SGU_WIDTH = 512
SGU_GROUPS = 8
SGU_GROUP_DIM = SGU_WIDTH // SGU_GROUPS
RET_HEADS = 4
RET_DK = 128
RET_DV = 128
CHUNK = 128
ROPE_THETA = 10000.0
PLE_DIM = 256
RMS_EPS = 1e-6
LN_EPS = 1e-5
IN_WIDTH = 3 * SGU_WIDTH + RET_HEADS * (2 * RET_DK + 2 * RET_DV)
O_SU, O_SV, O_SG = 0, SGU_WIDTH, 2 * SGU_WIDTH
O_Q = 3 * SGU_WIDTH
O_K = O_Q + RET_HEADS * RET_DK
O_V = O_K + RET_HEADS * RET_DK
O_RG = O_V + RET_HEADS * RET_DV

LANES = 128
VMEM_LIMIT_BYTES = 60 * 1024 * 1024

PROMPT_TILE = 1024
BLOCK_ROWS = {"prompt": 256, "sample": CHUNK}
PIECE_COLS = 512
CAST_STEPS = 8
SAMPLE_TILE = 2 * CHUNK


def _log_gamma():
    return np.log(1.0 - 2.0 ** (-5.0 - np.arange(RET_HEADS, dtype=np.float64)))


def _retention_tables(seg):
    lg = _log_gamma()[:, None, None]
    r = np.arange(CHUNK)
    i, j = r[:, None] % seg, r[None, :] % seg
    same = (r[:, None] // seg) == (r[None, :] // seg)
    decay = np.where(same & (i >= j), np.exp(lg * np.maximum(i - j, 0)), 0.0)
    wq = np.broadcast_to(np.exp(lg * (i + 1.0)), (RET_HEADS, CHUNK, LANES))
    wkv = np.broadcast_to(np.exp(lg * (seg - 1.0 - i)), (RET_HEADS, CHUNK, LANES))
    scale = RET_DK ** -0.5
    tabs = np.stack([decay * scale, wq, wkv * scale]).astype(np.float32)
    g_seg = [float(v) for v in np.exp(_log_gamma() * seg)]
    return tabs, g_seg


def _rotary_tables(pos):
    half = RET_DK // 2
    inv = ROPE_THETA ** (-np.arange(half, dtype=np.float64) / half)
    ang = pos.astype(np.float64)[:, None] * inv[None, :]
    cos, sin = np.cos(ang), np.sin(ang)
    return (np.concatenate([cos, cos], axis=1).astype(np.float32),
            np.concatenate([-sin, sin], axis=1).astype(np.float32))


def _mix_mask(seg):
    r = np.arange(CHUNK)
    same = (r[:, None] // seg) == (r[None, :] // seg)
    m = same & ((r[None, :] % seg) <= (r[:, None] % seg))
    return np.concatenate([m, m], axis=1).astype(np.float32)


def _gelu(x):
    c = float(np.sqrt(2.0 / np.pi))
    half = 0.5 * x
    return half + half * jnp.tanh(x * (c + (c * 0.044715) * (x * x)))


def _silu(x):
    return x * jax.nn.sigmoid(x)


def _rotate(x, cos, sin_signed):
    return x * cos + pltpu.roll(x, RET_DK // 2, 1) * sin_signed


def _normed_bf16(x, norm_ref):
    ms = jnp.mean(x * x, axis=-1, keepdims=True)
    return (x * lax.rsqrt(ms + RMS_EPS) * norm_ref[...]).astype(BF16)


def _in_pieces(x, norm_pre_ref, w_in_ref, z_ref):
    box = {}

    def piece(c0):
        def run():
            if 'h' not in box:
                box['h'] = _normed_bf16(x, norm_pre_ref)
            z_ref[:, c0:c0 + PIECE_COLS] = jnp.dot(box['h'], w_in_ref[:, c0:c0 + PIECE_COLS],
                                                   preferred_element_type=F32)
        return run
    return [piece(c0) for c0 in range(0, IN_WIDTH, PIECE_COLS)]


def _out_pieces(x, p, mixin_ref, w_out_ref, norm_post_ref, w_gate_ref, w_ple_ref, y_ref):
    box = {}
    halves = [slice(c0, c0 + PIECE_COLS) for c0 in range(0, D_MODEL, PIECE_COLS)]

    def ple():
        box['ple'] = jnp.dot(p.astype(BF16), w_ple_ref[...], preferred_element_type=F32)

    def mix(i):
        def run():
            box['mix%d' % i] = jnp.dot(mixin_ref[...], w_out_ref[:, halves[i]], preferred_element_type=F32)
        return run

    def gate(i):
        def run():
            if 'x1' not in box:
                mixes = [box['mix%d' % j] for j in range(len(halves))]
                ms = sum(jnp.sum(m * m, axis=-1, keepdims=True) for m in mixes) * (1.0 / D_MODEL)
                scale = lax.rsqrt(ms + RMS_EPS)
                box['x1'] = [x[:, halves[j]] + mixes[j] * scale * norm_post_ref[:, halves[j]]
                             for j in range(len(halves))]
                box['x1b'] = jnp.concatenate(box['x1'], axis=1).astype(BF16)
            g = jnp.dot(box['x1b'], w_gate_ref[:, halves[i]], preferred_element_type=F32)
            y_ref[:, halves[i]] = box['x1'][i] + jax.nn.sigmoid(g) * box['ple'][:, halves[i]]
        return run
    return [ple] + [mix(i) for i in range(len(halves))] + [gate(i) for i in range(len(halves))]


def _chunk_stages(z_ref, rows, cos, sin_signed, ln_g_ref, gn_ref, bias_ref, wpair_ref, tabs_ref,
                  state_update, state_output, mixin_ref, vrows_ref, head_groups):
    box = {}
    zs = lambda off, i, width: z_ref[rows, off + i * width:off + (i + 1) * width]

    def gating_in():
        sv = _gelu(z_ref[rows, O_SV:O_SV + SGU_WIDTH])
        mu = jnp.mean(sv, axis=-1, keepdims=True)
        cen = sv - mu
        var = jnp.mean(cen * cen, axis=-1, keepdims=True)
        vn = cen * lax.rsqrt(var + LN_EPS) * ln_g_ref[...]
        if vrows_ref is not None:
            vrows_ref[rows, :] = vn
        low_lanes = lax.broadcasted_iota(jnp.int32, (CHUNK, LANES), 1) < SGU_GROUP_DIM
        for m in range(SGU_GROUPS // 2):
            blk = vn[:, m * LANES:(m + 1) * LANES]
            rhs = jnp.concatenate([jnp.where(low_lanes, blk, 0.0), jnp.where(low_lanes, 0.0, blk)],
                                  axis=0).astype(BF16)
            box['mixed%d' % m] = jnp.dot(wpair_ref[m], rhs, preferred_element_type=F32)

    def retention_in(heads):
        c, s = cos(), sin_signed()
        for h in heads:
            q = _rotate(zs(O_Q, h, RET_DK), c, s)
            k = _rotate(zs(O_K, h, RET_DK), c, s)
            v = zs(O_V, h, RET_DV)
            k_b = k.astype(BF16)
            box['scores%d' % h] = lax.dot_general(q.astype(BF16), k_b, (((1,), (1,)), ((), ())),
                                                  preferred_element_type=F32)
            box['qw%d' % h] = (q * tabs_ref[1, h]).astype(BF16)
            box['v%d' % h] = v.astype(BF16)
            box['u%d' % h] = state_update(h, k_b, (v * tabs_ref[2, h]).astype(BF16))

    def gating_out(pairs):
        for m in pairs:
            cols = slice(m * LANES, (m + 1) * LANES)
            mixed = box['mixed%d' % m] + bias_ref[:, cols]
            su = _gelu(zs(O_SU, m, LANES))
            mixin_ref[rows, cols] = (_silu(zs(O_SG, m, LANES)) * (su * mixed)).astype(BF16)

    def retention_mid(heads):
        for h in heads:
            scores_b = (box['scores%d' % h] * tabs_ref[0, h]).astype(BF16)
            box['o%d' % h] = state_output(h, scores_b, box['qw%d' % h], box['v%d' % h], box['u%d' % h])

    def retention_out(heads):
        for h in heads:
            o = box['o%d' % h]
            mu = jnp.mean(o, axis=-1, keepdims=True)
            cen = o - mu
            var = jnp.mean(cen * cen, axis=-1, keepdims=True)
            on = cen * lax.rsqrt(var + LN_EPS) * gn_ref[:, h * RET_DV:(h + 1) * RET_DV]
            mixin_ref[rows, SGU_WIDTH + h * RET_DV:SGU_WIDTH + (h + 1) * RET_DV] = (
                _silu(zs(O_RG, h, RET_DV)) * on).astype(BF16)

    part = functools.partial
    n_pairs = SGU_GROUPS // 2
    pair_groups = [range(n_pairs)] if len(head_groups) < 2 else [range(n_pairs // 2), range(n_pairs // 2, n_pairs)]
    return ([gating_in] + [part(retention_in, g) for g in head_groups] + [part(gating_out, p) for p in pair_groups]
            + [part(retention_mid, g) for g in head_groups] + [part(retention_out, g) for g in head_groups])


def _interleave(pieces, stages):
    slots = max(len(stages), 1)
    done = 0
    for i in range(slots):
        upto = -(-(i + 1) * len(pieces) // slots)
        for piece in pieces[done:upto]:
            piece()
        done = upto
        if i < len(stages):
            stages[i]()


def _emit_blocks(in_pieces, out_pieces, stages, blocks, first_in=None, next_in=None, after_out=None):
    after_out = after_out or (lambda blk: None)
    _interleave(first_in or [], [])
    for i, blk in enumerate(blocks):
        nxt = in_pieces(blocks[i + 1]) if i + 1 < len(blocks) else (next_in or [])
        if i > 0:
            ple, mix0, mix1, gate0, gate1 = out_pieces(blocks[i - 1])
            pieces = [ple, mix0, mix1] + nxt[:3] + [gate0] + nxt[3:4] + [gate1] + nxt[4:]
        else:
            pieces = nxt
        _interleave(pieces, stages(blk))
        if i > 0:
            after_out(blocks[i - 1])
    _interleave(out_pieces(blocks[-1]), [])
    after_out(blocks[-1])


def _init_tables(seg, sgu_w_ref, sgu_b_ref, mask_ref, wpm_ref, bias_ref):
    rows = lax.broadcasted_iota(jnp.int32, (CHUNK, LANES), 0)
    lanes = lax.broadcasted_iota(jnp.int32, (CHUNK, LANES), 1)
    if seg == CHUNK:
        group = lambda g: sgu_w_ref[g]
        bias_rows = sgu_b_ref[...]
    else:
        reps = CHUNK // seg
        select = jnp.where((rows < seg) & (lanes % seg == rows), 1.0, 0.0).astype(BF16)
        first = lambda a: jnp.where(lanes[:seg] < seg, a, 0.0)

        def group(g):
            stacked = jnp.concatenate([first(sgu_w_ref[g, 0:seg, :])] * reps, axis=0)
            return jnp.dot(stacked.astype(BF16), select, preferred_element_type=F32)
        b_first = jnp.where(lax.broadcasted_iota(jnp.int32, (SGU_GROUPS, LANES), 1) < seg, sgu_b_ref[...], 0.0)
        bias_rows = sum(pltpu.roll(b_first, r * seg, 1) for r in range(reps))
    for m in range(SGU_GROUPS // 2):
        pair = jnp.concatenate([group(2 * m), group(2 * m + 1)], axis=1)
        wpm_ref[m] = jnp.where(mask_ref[...] > 0.0, pair, 0.0).astype(BF16)
    group_of_lane = lax.broadcasted_iota(jnp.int32, (SGU_GROUPS, SGU_WIDTH), 1) // SGU_GROUP_DIM
    expand = jnp.where(group_of_lane == lax.broadcasted_iota(jnp.int32, (SGU_GROUPS, SGU_WIDTH), 0),
                       1.0, 0.0).astype(BF16)
    total, rest = jnp.zeros((CHUNK, SGU_WIDTH), F32), bias_rows
    for _ in range(3):
        term = rest.astype(BF16)
        total = total + lax.dot_general(term, expand, (((0,), (0,)), ((), ())), preferred_element_type=F32)
        rest = rest - term.astype(F32)
    bias_ref[...] = total


def _prompt_kernel(g_chunk, n_tiles, dec_seq, x_ref, x_next_ref, p_ref, cos_ref, sin_ref, w_in_ref, w_out_ref,
                   w_gate_ref, w_ple_ref, norm_pre_ref, norm_post_ref, ln_g_ref, gn_ref, sgu_w_ref, sgu_b_ref,
                   mask_ref, tabs_ref, mask_dec_ref, y_hbm, state_ref, w_in_hbm, w_out_hbm, w_gate_hbm,
                   w_ple_hbm, wpm_dec_ref, bias_dec_ref, z_ref, z_first_ref, mixin_ref, wpm_ref, bias_ref,
                   w_in_b_ref, w_out_b_ref, w_gate_b_ref, w_ple_b_ref, w_sem, y_scr, y_sem):
    step = pl.program_id(0)
    last = pl.num_programs(0) - 1
    rows = BLOCK_ROWS["prompt"]
    blocks = [slice(r, r + rows) for r in range(0, PROMPT_TILE, rows)]
    first = blocks[0]

    def y_copy(s, blk):
        t = jnp.maximum(s - CAST_STEPS, 0)
        dst = y_hbm.at[t // n_tiles, pl.ds((t % n_tiles) * PROMPT_TILE + blk.start, rows), :]
        return pltpu.make_async_copy(y_scr.at[t % 2, blk, :], dst, y_sem.at[t % 2, blk.start // rows])

    w_copies = [pltpu.make_async_copy(src, dst, w_sem.at[i]) for i, (src, dst) in enumerate(
        ((w_in_b_ref, w_in_hbm), (w_out_b_ref, w_out_hbm), (w_gate_b_ref, w_gate_hbm), (w_ple_b_ref, w_ple_hbm)))]

    def next_first_in():
        return _in_pieces(x_next_ref[0], norm_pre_ref, w_in_b_ref, z_first_ref)

    @pl.when(step == 0)
    def _():
        _init_tables(CHUNK, sgu_w_ref, sgu_b_ref, mask_ref, wpm_ref, bias_ref)

    @pl.when(step == 1)
    def _():
        _init_tables(dec_seq, sgu_w_ref, sgu_b_ref, mask_dec_ref, wpm_dec_ref, bias_dec_ref)

    @pl.when(step < CAST_STEPS)
    def _():
        for src, dst in ((w_in_ref, w_in_b_ref), (w_out_ref, w_out_b_ref), (w_gate_ref, w_gate_b_ref),
                         (w_ple_ref, w_ple_b_ref)):
            n = src.shape[0]
            dst[pl.ds(pl.multiple_of(step * n, n), n), :] = src[...].astype(BF16)

    @pl.when(step == CAST_STEPS - 1)
    def _():
        _interleave(next_first_in(), [])

    @pl.when(step == CAST_STEPS)
    def _():
        for copy in w_copies:
            copy.start()

    @pl.when(step >= CAST_STEPS)
    def _():
        @pl.when((step - CAST_STEPS) % n_tiles == 0)
        def _():
            state_ref[...] = jnp.zeros_like(state_ref)

        def state_update(h, k_b, vw_b):
            return lax.dot_general(k_b, vw_b, (((0,), (0,)), ((), ())), preferred_element_type=F32)

        def state_output(h, scores_b, qw_b, v_b, u):
            s = state_ref[0, h]
            o = jnp.dot(jnp.concatenate([scores_b, qw_b], axis=1),
                        jnp.concatenate([v_b, s.astype(BF16)], axis=0), preferred_element_type=F32)
            state_ref[0, h] = g_chunk[h] * s + u
            return o

        def in_pieces(blk):
            return _in_pieces(x_ref[0, blk, :], norm_pre_ref, w_in_b_ref, z_ref.at[blk, :])

        def out_pieces(blk):
            return _out_pieces(x_ref[0, blk, :], p_ref[0, blk, :], mixin_ref.at[blk, :], w_out_b_ref,
                               norm_post_ref, w_gate_b_ref, w_ple_b_ref, y_scr.at[(step - CAST_STEPS) % 2, blk, :])

        def stages(blk):
            out = []
            for r in range(blk.start, blk.stop, CHUNK):
                rows = slice(r, r + CHUNK)
                out += _chunk_stages(z_first_ref if blk == first else z_ref, rows,
                                     lambda rows=rows: cos_ref[rows, :],
                                     lambda rows=rows: sin_ref[rows, :], ln_g_ref, gn_ref, bias_ref, wpm_ref,
                                     tabs_ref, state_update, state_output, mixin_ref, None,
                                     [range(0, RET_HEADS // 2), range(RET_HEADS // 2, RET_HEADS)])
            return out

        _emit_blocks(in_pieces, out_pieces, stages, blocks, next_in=next_first_in(),
                     after_out=lambda blk: y_copy(step, blk).start())

    @pl.when(step > CAST_STEPS)
    def _():
        for blk in blocks:
            y_copy(step - 1, blk).wait()

    @pl.when(step == last)
    def _():
        for blk in blocks:
            y_copy(step, blk).wait()
        for copy in w_copies:
            copy.wait()


def _sample_kernel(g_seq, x_ref, p_ref, cos_ref, sin_ref, st_in_ref, w_in_ref, w_out_ref, w_gate_ref,
                   w_ple_ref, norm_pre_ref, norm_post_ref, ln_g_ref, gn_ref, wpm_ref, bias_ref, tabs_ref,
                   y_ref, st_out_ref, vrows_ref, z_ref, mixin_ref):
    seq_len = cos_ref.shape[0]
    seqs = CHUNK // seq_len

    pairs = seqs // 2
    per_pair = lambda a: a.reshape(pairs, 2 * seq_len, a.shape[-1])
    first_of_pair = lax.broadcasted_iota(jnp.int32, (1, 2 * seq_len, 1), 1) < seq_len
    tile_rows = lambda ref: jnp.concatenate([ref[...]] * seqs, axis=0)

    def stages(blk):
        pq = slice(blk.start // (2 * seq_len), blk.stop // (2 * seq_len))

        def state_update(h, k_b, vw_b):
            vw = per_pair(vw_b)
            zero = jnp.zeros_like(vw)
            both = jnp.concatenate([jnp.where(first_of_pair, vw, zero), jnp.where(first_of_pair, zero, vw)],
                                   axis=-1)
            u = jnp.einsum('pjd,pje->pde', per_pair(k_b), both, preferred_element_type=F32)
            st_out_ref[pq, 0, h] = g_seq[h] * st_in_ref[pq, 0, h] + u[:, :, :RET_DV]
            st_out_ref[pq, 1, h] = g_seq[h] * st_in_ref[pq, 1, h] + u[:, :, RET_DV:]
            return None

        def state_output(h, scores_b, qw_b, v_b, _):
            o = jnp.dot(scores_b, v_b, preferred_element_type=F32)
            s_pair = jnp.concatenate([st_in_ref[pq, 0, h], st_in_ref[pq, 1, h]], axis=-1).astype(BF16)
            both = jnp.einsum('pid,pde->pie', per_pair(qw_b), s_pair, preferred_element_type=F32)
            o_inter = jnp.where(first_of_pair, both[:, :, :RET_DV], both[:, :, RET_DV:])
            return o + o_inter.reshape(CHUNK, RET_DV)

        return _chunk_stages(z_ref, blk, lambda: tile_rows(cos_ref), lambda: tile_rows(sin_ref), ln_g_ref,
                             gn_ref, bias_ref, wpm_ref, tabs_ref, state_update, state_output, mixin_ref,
                             vrows_ref, [range(RET_HEADS)])

    def in_pieces(blk):
        return _in_pieces(x_ref[blk, :], norm_pre_ref, w_in_ref, z_ref.at[blk, :])

    def out_pieces(blk):
        return _out_pieces(x_ref[blk, :], p_ref[blk, :], mixin_ref.at[blk, :], w_out_ref, norm_post_ref,
                           w_gate_ref, w_ple_ref, y_ref.at[blk, :])

    rows = BLOCK_ROWS["sample"]
    _emit_blocks(in_pieces, out_pieces, stages, [slice(r, r + rows) for r in range(0, SAMPLE_TILE, rows)],
                 first_in=in_pieces(slice(0, rows)))


def _const_spec(shape):
    return pl.BlockSpec(shape, lambda *_: (0,) * len(shape), pipeline_mode=pl.Buffered(1))


def kernel(x_prompt, x_sample, state_ret, p_prompt, p_sample, w_in, w_out, norm_pre, norm_post, sgu_w, sgu_b,
           sgu_ln, ret_gn, w_ple_proj, w_ple_gate):
    batch, seq, _ = x_prompt.shape
    dec_batch, dec_seq, _ = x_sample.shape
    n_tiles = seq // PROMPT_TILE
    n_tok = dec_batch * dec_seq
    assert seq % PROMPT_TILE == 0 and PROMPT_TILE % BLOCK_ROWS["prompt"] == 0 and w_in.shape[0] == 1
    assert CHUNK % dec_seq == 0 and n_tok % SAMPLE_TILE == 0 and SAMPLE_TILE % BLOCK_ROWS["sample"] == 0
    assert D_MODEL % CAST_STEPS == 0 and PLE_DIM % (16 * CAST_STEPS) == 0
    assert (CHUNK // dec_seq) % 2 == 0 and dec_batch % (SAMPLE_TILE // dec_seq) == 0

    small = (norm_pre[0][None, :], norm_post[0][None, :], sgu_ln[0][None, :], ret_gn[0][None, :],
             sgu_w[0], sgu_b[0])
    small_specs = [_const_spec((1, D_MODEL)), _const_spec((1, D_MODEL)), _const_spec((1, SGU_WIDTH)),
                   _const_spec((1, RET_HEADS * RET_DV)), _const_spec((SGU_GROUPS, CHUNK, CHUNK)),
                   _const_spec((SGU_GROUPS, CHUNK)), _const_spec((CHUNK, 2 * CHUNK)),
                   _const_spec((3, RET_HEADS, CHUNK, LANES))]
    table_shapes = [((SGU_GROUPS // 2, CHUNK, 2 * CHUNK), BF16), ((CHUNK, SGU_WIDTH), F32)]
    weight_shapes = [(D_MODEL, IN_WIDTH), (D_MODEL, D_MODEL), (D_MODEL, D_MODEL), (PLE_DIM, D_MODEL)]

    tabs_p, g_chunk = _retention_tables(CHUNK)
    cos_p, sin_p = _rotary_tables(np.arange(seq))
    tile_of = lambda s: jnp.maximum(s - CAST_STEPS, 0)
    tile_spec = lambda width: pl.BlockSpec((1, PROMPT_TILE, width),
                                           lambda s: (tile_of(s) // n_tiles, tile_of(s) % n_tiles, 0))
    pos_spec = pl.BlockSpec((PROMPT_TILE, LANES), lambda s: (tile_of(s) % n_tiles, 0))
    blocks_per_tile = PROMPT_TILE // BLOCK_ROWS["prompt"]
    next_tile = lambda s: jnp.minimum(jnp.maximum(s - CAST_STEPS + 1, 0), batch * n_tiles - 1)
    next_spec = pl.BlockSpec((1, BLOCK_ROWS["prompt"], D_MODEL),
                             lambda s: (next_tile(s) // n_tiles, (next_tile(s) % n_tiles) * blocks_per_tile, 0))
    cast_spec = lambda shape: pl.BlockSpec((shape[0] // CAST_STEPS, shape[1]),
                                           lambda s: (jnp.minimum(s, CAST_STEPS - 1), 0))
    resident = lambda shape: pl.BlockSpec(shape, lambda s: (0,) * len(shape))
    y_prompt, st_prompt, w_in_b, w_out_b, w_gate_b, w_ple_b, wpm_dec, bias_dec = pl.pallas_call(
        functools.partial(_prompt_kernel, g_chunk, n_tiles, dec_seq),
        grid=(CAST_STEPS + batch * n_tiles,),
        in_specs=[tile_spec(D_MODEL), next_spec, tile_spec(PLE_DIM), pos_spec, pos_spec]
        + [cast_spec(shape) for shape in weight_shapes] + small_specs + [_const_spec((CHUNK, 2 * CHUNK))],
        out_specs=[pl.BlockSpec(memory_space=pl.ANY),
                   pl.BlockSpec((1, RET_HEADS, RET_DK, RET_DV), lambda s: (tile_of(s) // n_tiles, 0, 0, 0))]
        + [pl.BlockSpec(memory_space=pl.ANY)] * len(weight_shapes) + [resident(shape) for shape, _ in table_shapes],
        out_shape=[jax.ShapeDtypeStruct((batch, seq, D_MODEL), F32),
                   jax.ShapeDtypeStruct((batch, RET_HEADS, RET_DK, RET_DV), F32)]
        + [jax.ShapeDtypeStruct(shape, BF16) for shape in weight_shapes]
        + [jax.ShapeDtypeStruct(shape, dtype) for shape, dtype in table_shapes],
        scratch_shapes=[pltpu.VMEM((PROMPT_TILE, IN_WIDTH), F32), pltpu.VMEM((BLOCK_ROWS["prompt"], IN_WIDTH), F32),
                        pltpu.VMEM((PROMPT_TILE, D_MODEL), BF16)]
        + [pltpu.VMEM(shape, dtype) for shape, dtype in table_shapes]
        + [pltpu.VMEM(shape, BF16) for shape in weight_shapes] + [pltpu.SemaphoreType.DMA((len(weight_shapes),))]
        + [pltpu.VMEM((2, PROMPT_TILE, D_MODEL), F32), pltpu.SemaphoreType.DMA((2, blocks_per_tile))],
        compiler_params=pltpu.CompilerParams(dimension_semantics=("arbitrary",),
                                             vmem_limit_bytes=VMEM_LIMIT_BYTES),
        name="prompt_layer",
    )(x_prompt, x_prompt, p_prompt[0], cos_p, sin_p, w_in[0], w_out[0], w_ple_gate[0], w_ple_proj[0], *small,
      _mix_mask(CHUNK), tabs_p, _mix_mask(dec_seq))

    tabs_s, g_seq = _retention_tables(dec_seq)
    cos_s, sin_s = _rotary_tables(PAST_LEN + np.arange(dec_seq))
    tile_seqs = SAMPLE_TILE // dec_seq
    tok_spec = lambda width: pl.BlockSpec((SAMPLE_TILE, width), lambda i: (i, 0))
    state_spec = pl.BlockSpec((tile_seqs // 2, 2, RET_HEADS, RET_DK, RET_DV), lambda i: (i, 0, 0, 0, 0))
    paired = (dec_batch // 2, 2, RET_HEADS, RET_DK, RET_DV)
    y_sample, st_sample, v_sample = pl.pallas_call(
        functools.partial(_sample_kernel, g_seq),
        grid=(n_tok // SAMPLE_TILE,),
        in_specs=[tok_spec(D_MODEL), tok_spec(PLE_DIM), _const_spec((dec_seq, LANES)),
                  _const_spec((dec_seq, LANES)), state_spec]
        + [_const_spec(shape) for shape in weight_shapes] + small_specs[:4]
        + [_const_spec(shape) for shape, _ in table_shapes] + small_specs[-1:],
        out_specs=[tok_spec(D_MODEL), state_spec, tok_spec(SGU_WIDTH)],
        out_shape=[jax.ShapeDtypeStruct((n_tok, D_MODEL), F32),
                   jax.ShapeDtypeStruct(paired, F32), jax.ShapeDtypeStruct((n_tok, SGU_WIDTH), F32)],
        scratch_shapes=[pltpu.VMEM((SAMPLE_TILE, IN_WIDTH), F32), pltpu.VMEM((SAMPLE_TILE, D_MODEL), BF16)],
        compiler_params=pltpu.CompilerParams(dimension_semantics=("arbitrary",),
                                             vmem_limit_bytes=VMEM_LIMIT_BYTES),
        name="sample_layer",
    )(x_sample.reshape(n_tok, D_MODEL), p_sample[0].reshape(n_tok, PLE_DIM), cos_s, sin_s,
      state_ret[0].reshape(paired),
      w_in_b, w_out_b, w_gate_b, w_ple_b, *small[:4], wpm_dec, bias_dec, tabs_s)

    return (y_prompt, y_sample.reshape(dec_batch, dec_seq, D_MODEL), st_prompt[None],
            st_sample.reshape(state_ret.shape),
            v_sample.reshape(1, dec_batch, dec_seq, SGU_WIDTH))
```
